```python
import jax, jax.numpy as jnp
from jax import lax
import numpy as np

D_MODEL = 1024
BATCH = 2
SEQ = 16384
DEPTH = 1

N_META = 16
N_HEADS = 8
HEAD_DIM = 128
ATTN_WIDTH = N_HEADS * HEAD_DIM
N_IDX_HEADS = 8
IDX_DIM = 64
TOPK_MAX = 256
CONV_WIDTH = D_MODEL
CONV_K = 3
D_FF = ((8 * D_MODEL // 3 + 255) // 256) * 256
Q_BLOCK = 128
EPS = 1e-6
IDX_SCALE = (N_IDX_HEADS ** -0.5) * (IDX_DIM ** -0.5)

_PROJ_SIZES = [ATTN_WIDTH, ATTN_WIDTH, ATTN_WIDTH,
               N_IDX_HEADS * IDX_DIM, IDX_DIM, N_IDX_HEADS,
               CONV_WIDTH, CONV_WIDTH, CONV_WIDTH,
               D_MODEL, D_MODEL]
PROJ_WIDTH = int(sum(_PROJ_SIZES))
PROJ_SPLITS = [int(s) for s in np.cumsum(_PROJ_SIZES)[:-1]]

kernel_name = "hybrid_dsa_shortconv_gated_block"


def rmsnorm(x, g):
    xf = x.astype(jnp.float32)
    y = xf * lax.rsqrt(jnp.mean(xf * xf, axis=-1, keepdims=True) + EPS)
    return (y * g.astype(jnp.float32)).astype(x.dtype)


def dsa_attention(q, k, v, q_idx, k_idx, w_idx):
    b, l = q.shape[0], q.shape[1]
    topk = min(TOPK_MAX, l // 4)
    n_blk = -(-l // Q_BLOCK)
    lp = n_blk * Q_BLOCK

    def pad(a):
        return jnp.pad(a, [(0, 0), (0, lp - l)] + [(0, 0)] * (a.ndim - 2))

    q, k, v, q_idx, k_idx, w_idx = (pad(a) for a in (q, k, v, q_idx, k_idx, w_idx))
    key_pos = jnp.arange(lp)

    def to_blocks(a):
        return jnp.moveaxis(a.reshape((b, n_blk, Q_BLOCK) + a.shape[2:]), 1, 0)

    def one_block(args):
        q_b, qi_b, w_b, start = args
        q_pos = start + jnp.arange(Q_BLOCK)
        visible = key_pos[None, :] <= q_pos[:, None]
        rel = jax.nn.relu(jnp.einsum('bthd,bsd->bths', qi_b, k_idx).astype(jnp.float32))
        score = jnp.einsum('bths,bth->bts', rel, w_b.astype(jnp.float32) * IDX_SCALE)
        score = jnp.where(visible[None], score, -jnp.inf)
        _, sel = lax.top_k(score, topk)
        k_sel = jax.vmap(lambda kb, ib: kb[ib])(k, sel)
        v_sel = jax.vmap(lambda vb, ib: vb[ib])(v, sel)
        valid = sel <= q_pos[None, :, None]
        logits = jnp.einsum('bthd,btkhd->bthk', q_b, k_sel).astype(jnp.float32) * (HEAD_DIM ** -0.5)
        logits = jnp.where(valid[:, :, None, :], logits, -jnp.inf)
        p = jax.nn.softmax(logits, axis=-1).astype(v_sel.dtype)
        return jnp.einsum('bthk,btkhd->bthd', p, v_sel)

    starts = jnp.arange(n_blk) * Q_BLOCK
    out = lax.map(one_block, (to_blocks(q), to_blocks(q_idx), to_blocks(w_idx), starts))
    out = jnp.moveaxis(out, 0, 1).reshape(b, lp, N_HEADS, HEAD_DIM)
    return out[:, :l]


def short_conv(u, w):
    return lax.conv_general_dilated(
        u, w[:, None, :].astype(u.dtype), window_strides=(1,), padding=[(CONV_K - 1, 0)],
        dimension_numbers=('NWC', 'WIO', 'NWC'), feature_group_count=u.shape[-1])


def setup_inputs(seed: int = 0) -> dict:
    key = jax.random.key(seed)
    ks = jax.random.split(key, 16)
    f32 = jnp.float32
    nrm = lambda k, shape, scale: (jax.random.normal(k, shape, f32) * scale)
    gain = lambda k: 1.0 + 0.02 * jax.random.normal(k, (DEPTH, D_MODEL), f32)
    return {
        "x": nrm(ks[0], (BATCH, SEQ, D_MODEL), 1.0),
        "meta_tokens": nrm(ks[1], (N_META, D_MODEL), 1.0),
        "norm_mix_g": gain(ks[2]),
        "w_in": nrm(ks[3], (DEPTH, D_MODEL, PROJ_WIDTH), D_MODEL ** -0.5),
        "w_attn_out": nrm(ks[4], (DEPTH, ATTN_WIDTH, D_MODEL), ATTN_WIDTH ** -0.5),
        "conv_w": nrm(ks[5], (DEPTH, CONV_K, CONV_WIDTH), CONV_K ** -0.5),
        "w_conv_out": nrm(ks[6], (DEPTH, CONV_WIDTH, D_MODEL), CONV_WIDTH ** -0.5),
        "w_out": nrm(ks[7], (DEPTH, D_MODEL, D_MODEL), D_MODEL ** -0.5),
        "norm_ffn_g": gain(ks[8]),
        "w_gate": nrm(ks[9], (DEPTH, D_MODEL, D_FF), D_MODEL ** -0.5),
        "w_up": nrm(ks[10], (DEPTH, D_MODEL, D_FF), D_MODEL ** -0.5),
        "w_down": nrm(ks[11], (DEPTH, D_FF, D_MODEL), D_FF ** -0.5),
        "norm_final_g": 1.0 + 0.02 * jax.random.normal(ks[12], (D_MODEL,), f32),
    }


def reference(x, meta_tokens, norm_mix_g, w_in, w_attn_out, conv_w, w_conv_out, w_out,
              norm_ffn_g, w_gate, w_up, w_down, norm_final_g):
    b = x.shape[0]
    meta = jnp.broadcast_to(meta_tokens[None].astype(x.dtype), (b, N_META, x.shape[-1]))
    h = jnp.concatenate([meta, x], axis=1)
    l = h.shape[1]
    for i in range(DEPTH):
        a = rmsnorm(h, norm_mix_g[i])
        proj = a @ w_in[i]
        q, k, v, qi, ki, wi, cu, cb, cc, ga, gb = jnp.split(proj, PROJ_SPLITS, axis=-1)
        y_attn = dsa_attention(q.reshape(b, l, N_HEADS, HEAD_DIM),
                               k.reshape(b, l, N_HEADS, HEAD_DIM),
                               v.reshape(b, l, N_HEADS, HEAD_DIM),
                               qi.reshape(b, l, N_IDX_HEADS, IDX_DIM), ki, wi)
        y_attn = y_attn.reshape(b, l, ATTN_WIDTH) @ w_attn_out[i]
        y_conv = (cb * short_conv(cc * cu, conv_w[i])) @ w_conv_out[i]
        mixed = jax.nn.sigmoid(ga) * y_attn + jax.nn.sigmoid(gb) * y_conv
        h = h + mixed @ w_out[i]
        f = rmsnorm(h, norm_ffn_g[i])
        h = h + (jax.nn.silu(f @ w_gate[i]) * (f @ w_up[i])) @ w_down[i]
    out = rmsnorm(h, norm_final_g)
    return out[:, N_META:]
```

```python
import functools

import jax
import jax.numpy as jnp
from jax import lax
from jax.experimental import pallas as pl
from jax.experimental.pallas import tpu as pltpu

N_META = 16
N_HEADS = 8
HEAD_DIM = 128
ATTN_WIDTH = N_HEADS * HEAD_DIM
N_IDX_HEADS = 8
IDX_DIM = 64
TOPK_MAX = 256
CONV_K = 3
EPS = 1e-6
IDX_SCALE = (N_IDX_HEADS ** -0.5) * (IDX_DIM ** -0.5)

ROW_TILE = 256
FRONT_PAD = ROW_TILE - N_META
MAX_KV_TILES_PER_STEP = 5
BISECT_CAP = 40
NEG = -1e30
VMEM_LIMIT = 56 * 1024 * 1024

f32 = jnp.float32
bf16 = jnp.bfloat16


def _rms(x, g):
    return (x * lax.rsqrt(jnp.mean(x * x, axis=-1, keepdims=True) + EPS)) * g


def _sigmoid(x):
    return 1.0 / (1.0 + jnp.exp(-x))


def _rms_proj_kernel(h_ref, g_ref, w_ref, cs_ref, o_ref):
    a = _rms(h_ref[...], g_ref[...]).astype(bf16)
    y = jnp.dot(a, w_ref[...], preferred_element_type=f32)
    o_ref[...] = (y * cs_ref[...]).astype(o_ref.dtype)


def _rms_proj(h2d, g, w, col_scale, out_dtype, tm):
    rows, d = h2d.shape
    n = w.shape[1]
    return pl.pallas_call(
        _rms_proj_kernel,
        grid=(rows // tm,),
        in_specs=[
            pl.BlockSpec((tm, d), lambda i: (i, 0)),
            pl.BlockSpec((1, d), lambda i: (0, 0)),
            pl.BlockSpec((d, n), lambda i: (0, 0)),
            pl.BlockSpec((1, n), lambda i: (0, 0)),
        ],
        out_specs=pl.BlockSpec((tm, n), lambda i: (i, 0)),
        out_shape=jax.ShapeDtypeStruct((rows, n), out_dtype),
        compiler_params=pltpu.CompilerParams(
            dimension_semantics=("arbitrary",), vmem_limit_bytes=VMEM_LIMIT),
        name="rms_proj",
    )(h2d, g, w, col_scale)


def _attn_kernel(qi_ref, wi_ref, kit_ref, q_ref, k_ref, v_ref, o_ref,
                 s_scr, bias_scr, thr_scr, cut_scr, m_scr, l_scr, acc_scr, flag_scr,
                 *, topk, g_tiles, lp):
    tq = ROW_TILE
    kt = ROW_TILE
    rg = 64
    iq = pl.program_id(1)
    j = pl.program_id(2)
    nkb = pl.num_programs(2)
    i = iq + 1
    n_tiles = i + 1
    kf = float(topk)

    def count_rows(fn):
        outs = []
        for g in range(tq // rg):
            rows = slice(g * rg, (g + 1) * rg)

            def body(t, acc, rows=rows, g=g):
                m = fn(s_scr[t, rows, :], t, g)
                return acc + m[:, :128] + m[:, 128:]

            acc = lax.fori_loop(0, n_tiles, body, jnp.zeros((rg, 128), f32))
            outs.append(jnp.sum(acc, axis=1, keepdims=True))
        return jnp.concatenate(outs, axis=0)

    def rowmax_rows(fn):
        outs = []
        for g in range(tq // rg):
            rows = slice(g * rg, (g + 1) * rg)

            def body(t, acc, rows=rows, g=g):
                m = fn(s_scr[t, rows, :], t, g)
                return jnp.maximum(acc, jnp.maximum(m[:, :128], m[:, 128:]))

            acc = lax.fori_loop(0, n_tiles, body, jnp.full((rg, 128), -jnp.inf, f32))
            outs.append(jnp.max(acc, axis=1, keepdims=True))
        return jnp.concatenate(outs, axis=0)

    def grp(v, g):
        return v[g * rg:(g + 1) * rg]

    @pl.when(j == 0)
    def _scores_and_threshold():
        m_scr[...] = jnp.full(m_scr.shape, NEG, f32)
        l_scr[...] = jnp.zeros(l_scr.shape, f32)
        acc_scr[...] = jnp.zeros(acc_scr.shape, f32)

        w = wi_ref[0] * IDX_SCALE
        qpos = i * tq + lax.broadcasted_iota(jnp.int32, (tq, 1), 0)

        def tile_body(t, carry):
            rmax, rmin = carry
            kt_tile = kit_ref[0, t]
            acc = jnp.zeros((tq, kt), f32)
            for h in range(N_IDX_HEADS):
                x = jnp.dot(qi_ref[0, h], kt_tile, preferred_element_type=f32)
                acc = acc + jnp.maximum(x, 0.0) * w[:, h:h + 1]
            kpos = t * kt + lax.broadcasted_iota(jnp.int32, (1, kt), 1)
            vis = (kpos >= FRONT_PAD) & (kpos <= qpos)
            s_scr[t] = jnp.where(vis, acc, -jnp.inf)
            rmax = jnp.maximum(rmax, jnp.max(jnp.where(vis, acc, -jnp.inf), axis=1, keepdims=True))
            rmin = jnp.minimum(rmin, jnp.min(jnp.where(vis, acc, jnp.inf), axis=1, keepdims=True))
            return rmax, rmin

        rmax, rmin = lax.fori_loop(
            0, n_tiles, tile_body,
            (jnp.full((tq, 1), -jnp.inf, f32), jnp.full((tq, 1), jnp.inf, f32)))

        n_vis = (qpos - (FRONT_PAD - 1)).astype(f32)
        short = n_vis <= kf

        def bis_cond(st):
            it, _, _, _, _, active = st
            return jnp.logical_and(it < BISECT_CAP, jnp.max(active) > 0.0)

        def bis_body(st):
            it, lo, hi, thr, resolved, active = st
            piv = lo + (hi - lo) * 0.5
            splits = jnp.logical_and(piv > lo, piv < hi)
            c = count_rows(lambda blk, t, g: jnp.where(blk >= grp(piv, g), 1.0, 0.0))
            act = active > 0.0
            upd = jnp.logical_and(act, splits)
            hit = jnp.logical_and(upd, c == kf)
            thr = jnp.where(hit, piv, thr)
            resolved = jnp.where(hit, 1.0, resolved)
            lo = jnp.where(jnp.logical_and(upd, c > kf), piv, lo)
            hi = jnp.where(jnp.logical_and(upd, c < kf), piv, hi)
            active = jnp.where(jnp.logical_and(upd, jnp.logical_not(hit)), 1.0, 0.0)
            return it + 1, lo, hi, thr, resolved, active

        res0 = jnp.where(short, 1.0, 0.0)
        st = (jnp.int32(0), rmin, rmax, rmin, res0, 1.0 - res0)
        _, lo, hi, thr, resolved, _ = lax.while_loop(bis_cond, bis_body, st)

        thr_scr[...] = thr
        cut_scr[...] = jnp.full((tq, 1), float(lp), f32)
        flag_scr[0] = jnp.int32(0)
        unresolved = resolved < 0.5

        @pl.when(jnp.max(jnp.where(unresolved, 1.0, 0.0)) > 0.0)
        def _fallback():
            def counts(v):
                cge = count_rows(lambda blk, t, g: jnp.where(blk >= grp(v, g), 1.0, 0.0))
                cgt = count_rows(lambda blk, t, g: jnp.where(blk > grp(v, g), 1.0, 0.0))
                return cge, cgt

            v0 = rowmax_rows(lambda blk, t, g: jnp.where(blk <= grp(hi, g), blk, -jnp.inf))
            cge0, cgt0 = counts(v0)

            def fb_cond(st):
                _, cge, _ = st
                more = jnp.logical_and(unresolved, cge < kf)
                return jnp.max(jnp.where(more, 1.0, 0.0)) > 0.0

            def fb_body(st):
                v, cge, cgt = st
                more = jnp.logical_and(unresolved, cge < kf)
                v2 = rowmax_rows(lambda blk, t, g: jnp.where(blk < grp(v, g), blk, -jnp.inf))
                cge2, cgt2 = counts(v2)
                return (jnp.where(more, v2, v), jnp.where(more, cge2, cge),
                        jnp.where(more, cgt2, cgt))

            v, cge, cgt = lax.while_loop(fb_cond, fb_body, (v0, cge0, cgt0))
            need = kf - cgt
            partial = jnp.logical_and(unresolved, (cge - cgt) > need)
            thr2 = jnp.where(unresolved, v, thr)
            thr_scr[...] = thr2

            @pl.when(jnp.max(jnp.where(partial, 1.0, 0.0)) > 0.0)
            def _tie_cut():
                def kidx(t):
                    return (t * kt + lax.broadcasted_iota(jnp.int32, (1, kt), 1)).astype(f32)

                def cut_body(_, st):
                    lo_i, hi_i = st
                    mid = jnp.floor((lo_i + hi_i) * 0.5)
                    c = count_rows(lambda blk, t, g: jnp.where(
                        blk == grp(thr2, g), jnp.where(kidx(t) <= grp(mid, g), 1.0, 0.0), 0.0))
                    ge = c >= need
                    return jnp.where(ge, lo_i, mid), jnp.where(ge, mid, hi_i)

                n_steps = max(1, int(lp).bit_length())
                _, hi_i = lax.fori_loop(
                    0, n_steps, cut_body,
                    (jnp.full((tq, 1), -1.0, f32), jnp.full((tq, 1), float(lp - 1), f32)))
                cut_scr[...] = jnp.where(partial, hi_i, float(lp))
                flag_scr[0] = jnp.int32(1)

    n_sub = jnp.minimum(g_tiles, n_tiles - j * g_tiles)

    @pl.when(n_sub > 0)
    def _attend():
        thr = thr_scr[...]

        def sub_body(s, carry):
            t = j * g_tiles + s
            sc = s_scr[t]
            tie_path = flag_scr[0] > 0

            @pl.when(jnp.logical_not(tie_path))
            def _():
                bias_scr[...] = jnp.where(sc >= thr, 0.0, NEG)

            @pl.when(tie_path)
            def _():
                kidx = (t * kt + lax.broadcasted_iota(jnp.int32, (1, kt), 1)).astype(f32)
                keep_tie = jnp.where(kidx <= cut_scr[...], 0.0, NEG)
                bias_scr[...] = jnp.where(sc > thr, 0.0, jnp.where(sc == thr, keep_tie, NEG))

            bias = bias_scr[...]
            row0 = pl.multiple_of(s * kt, kt)
            for h in range(N_HEADS):
                cols = slice(h * HEAD_DIM, (h + 1) * HEAD_DIM)
                qh = q_ref[0, :, cols]
                kh = k_ref[0, pl.ds(row0, kt), cols]
                vh = v_ref[0, pl.ds(row0, kt), cols]
                lg = lax.dot_general(qh, kh, (((1,), (1,)), ((), ())),
                                     preferred_element_type=f32) + bias
                m_old = m_scr[h]
                m_new = jnp.maximum(m_old, jnp.max(lg, axis=1, keepdims=True))
                alpha = jnp.exp(m_old - m_new)
                p = jnp.exp(lg - m_new)
                l_scr[h] = alpha * l_scr[h] + jnp.sum(p, axis=1, keepdims=True)
                acc_scr[:, cols] = alpha * acc_scr[:, cols] + jnp.dot(
                    p.astype(bf16), vh, preferred_element_type=f32)
                m_scr[h] = m_new
            return carry

        lax.fori_loop(0, n_sub, sub_body, 0)

    @pl.when(j == nkb - 1)
    def _finish():
        for h in range(N_HEADS):
            cols = slice(h * HEAD_DIM, (h + 1) * HEAD_DIM)
            o_ref[0, :, cols] = (acc_scr[:, cols] / l_scr[h]).astype(o_ref.dtype)


def _attention(qi4, wi, kit, qkv, *, seq, lp, topk):
    b = qkv.shape[0]
    tq = ROW_TILE
    nt = lp // tq
    g_tiles = max(g for g in range(1, MAX_KV_TILES_PER_STEP + 1) if nt % g == 0)
    nkb = nt // g_tiles
    nq = seq // tq

    def kv_block(iq, j):
        return jnp.minimum(j, (iq + 1) // g_tiles)

    kern = functools.partial(_attn_kernel, topk=topk, g_tiles=g_tiles, lp=lp)
    return pl.pallas_call(
        kern,
        grid=(b, nq, nkb),
        in_specs=[
            pl.BlockSpec((1, N_IDX_HEADS, tq, IDX_DIM), lambda bb, iq, j: (bb, 0, iq + 1, 0)),
            pl.BlockSpec((1, tq, N_IDX_HEADS), lambda bb, iq, j: (bb, iq + 1, 0)),
            pl.BlockSpec((1, nt, IDX_DIM, tq), lambda bb, iq, j: (bb, 0, 0, 0)),
            pl.BlockSpec((1, tq, ATTN_WIDTH), lambda bb, iq, j: (bb, iq + 1, 0)),
            pl.BlockSpec((1, g_tiles * tq, ATTN_WIDTH), lambda bb, iq, j: (bb, kv_block(iq, j), 1)),
            pl.BlockSpec((1, g_tiles * tq, ATTN_WIDTH), lambda bb, iq, j: (bb, kv_block(iq, j), 2)),
        ],
        out_specs=pl.BlockSpec((1, tq, ATTN_WIDTH), lambda bb, iq, j: (bb, iq, 0)),
        out_shape=jax.ShapeDtypeStruct((b, seq, ATTN_WIDTH), bf16),
        scratch_shapes=[
            pltpu.VMEM((nt, tq, tq), f32),
            pltpu.VMEM((tq, tq), f32),
            pltpu.VMEM((tq, 1), f32),
            pltpu.VMEM((tq, 1), f32),
            pltpu.VMEM((N_HEADS, tq, 1), f32),
            pltpu.VMEM((N_HEADS, tq, 1), f32),
            pltpu.VMEM((tq, ATTN_WIDTH), f32),
            pltpu.SMEM((1,), jnp.int32),
        ],
        compiler_params=pltpu.CompilerParams(
            dimension_semantics=("arbitrary", "arbitrary", "arbitrary"),
            vmem_limit_bytes=VMEM_LIMIT),
        name="dsa_attention",
    )(qi4, wi, kit, qkv, qkv, qkv)


def _mix_kernel(x_ref, ya_ref, cv_ref, cvp_ref, gt_ref, cw_ref, wao_ref, wco_ref, wout_ref,
                o_ref, u_scr):
    tm = x_ref.shape[1]
    c = cw_ref.shape[1]
    cu = cv_ref[0, :, 0:c]
    cb = cv_ref[0, :, c:2 * c]
    cc = cv_ref[0, :, 2 * c:3 * c]
    u_scr[0:8, :] = cvp_ref[0, :, 2 * c:3 * c] * cvp_ref[0, :, 0:c]
    u_scr[8:8 + tm, :] = cc * cu
    w = cw_ref[...]
    conv = (w[2:3] * u_scr[8:8 + tm, :] + w[1:2] * u_scr[7:7 + tm, :]
            + w[0:1] * u_scr[6:6 + tm, :])
    y_conv = jnp.dot((cb * conv).astype(bf16), wco_ref[...], preferred_element_type=f32)
    y_attn = jnp.dot(ya_ref[0], wao_ref[...], preferred_element_type=f32)
    d = wout_ref.shape[0]
    mixed = _sigmoid(gt_ref[0, :, 0:d]) * y_attn + _sigmoid(gt_ref[0, :, d:2 * d]) * y_conv
    o_ref[0] = x_ref[0] + jnp.dot(mixed.astype(bf16), wout_ref[...], preferred_element_type=f32)


def _mix(x, y_attn, conv3, gates3, conv_w, wao, wco, wout):
    b, seq, d = x.shape
    tm = ROW_TILE
    c = conv_w.shape[1]
    const = lambda bb, i: (0, 0)
    return pl.pallas_call(
        _mix_kernel,
        grid=(b, seq // tm),
        in_specs=[
            pl.BlockSpec((1, tm, d), lambda bb, i: (bb, i, 0)),
            pl.BlockSpec((1, tm, ATTN_WIDTH), lambda bb, i: (bb, i, 0)),
            pl.BlockSpec((1, tm, 3 * c), lambda bb, i: (bb, i + 1, 0)),
            pl.BlockSpec((1, 8, 3 * c), lambda bb, i: (bb, (i + 1) * (tm // 8) - 1, 0)),
            pl.BlockSpec((1, tm, 2 * d), lambda bb, i: (bb, i + 1, 0)),
            pl.BlockSpec((CONV_K, c), const),
            pl.BlockSpec(wao.shape, const),
            pl.BlockSpec(wco.shape, const),
            pl.BlockSpec(wout.shape, const),
        ],
        out_specs=pl.BlockSpec((1, tm, d), lambda bb, i: (bb, i, 0)),
        out_shape=jax.ShapeDtypeStruct((b, seq, d), f32),
        scratch_shapes=[pltpu.VMEM((tm + 8, c), f32)],
        compiler_params=pltpu.CompilerParams(
            dimension_semantics=("arbitrary", "arbitrary"), vmem_limit_bytes=VMEM_LIMIT),
        name="mix_merge",
    )(x, y_attn, conv3, conv3, gates3, conv_w, wao, wco, wout)


def _ffn_kernel(h_ref, g1_ref, wg_ref, wu_ref, wd_ref, g2_ref, o_ref):
    h = h_ref[0]
    f = _rms(h, g1_ref[...]).astype(bf16)
    gate = jnp.dot(f, wg_ref[...], preferred_element_type=f32)
    up = jnp.dot(f, wu_ref[...], preferred_element_type=f32)
    act = (gate * _sigmoid(gate)) * up
    h2 = h + jnp.dot(act.astype(bf16), wd_ref[...], preferred_element_type=f32)
    o_ref[0] = _rms(h2, g2_ref[...])


def _ffn(h1, g1, wg, wu, wd, g2):
    b, seq, d = h1.shape
    tm = ROW_TILE
    const = lambda bb, i: (0, 0)
    return pl.pallas_call(
        _ffn_kernel,
        grid=(b, seq // tm),
        in_specs=[
            pl.BlockSpec((1, tm, d), lambda bb, i: (bb, i, 0)),
            pl.BlockSpec((1, d), const),
            pl.BlockSpec(wg.shape, const),
            pl.BlockSpec(wu.shape, const),
            pl.BlockSpec(wd.shape, const),
            pl.BlockSpec((1, d), const),
        ],
        out_specs=pl.BlockSpec((1, tm, d), lambda bb, i: (bb, i, 0)),
        out_shape=jax.ShapeDtypeStruct((b, seq, d), f32),
        compiler_params=pltpu.CompilerParams(
            dimension_semantics=("arbitrary", "arbitrary"), vmem_limit_bytes=VMEM_LIMIT),
        name="ffn_final",
    )(h1, g1, wg, wu, wd, g2)


def kernel(x, meta_tokens, norm_mix_g, w_in, w_attn_out, conv_w, w_conv_out, w_out,
           norm_ffn_g, w_gate, w_up, w_down, norm_final_g):
    b, seq, d = x.shape
    assert w_in.shape[0] == 1, "single-layer block"
    assert seq % ROW_TILE == 0 and meta_tokens.shape[0] == N_META
    c = conv_w.shape[2]
    lp = ROW_TILE + seq
    l_real = N_META + seq
    topk = min(TOPK_MAX, l_real // 4)
    nt = lp // ROW_TILE

    meta = jnp.broadcast_to(meta_tokens[None].astype(x.dtype), (b, N_META, d))
    hp = jnp.concatenate([jnp.zeros((b, FRONT_PAD, d), x.dtype), meta, x], axis=1)
    h2d = hp.reshape(b * lp, d)

    w = w_in[0]
    o_idx = 3 * ATTN_WIDTH
    n_idx = N_IDX_HEADS * IDX_DIM + IDX_DIM + N_IDX_HEADS
    o_conv = o_idx + n_idx
    o_gate = o_conv + 3 * c
    n_idx_pad = -(-n_idx // 128) * 128
    w_qkv = w[:, :o_idx].astype(bf16)
    w_idx = jnp.pad(w[:, o_idx:o_conv], ((0, 0), (0, n_idx_pad - n_idx))).astype(bf16)
    w_conv = w[:, o_conv:o_gate].astype(bf16)
    w_gates = w[:, o_gate:].astype(bf16)
    g_mix = norm_mix_g[0][None]

    tm = 512 if (b * lp) % 512 == 0 else ROW_TILE
    qkv_scale = jnp.concatenate([jnp.full((1, ATTN_WIDTH), HEAD_DIM ** -0.5, f32),
                                 jnp.ones((1, 2 * ATTN_WIDTH), f32)], axis=1)
    qkv = _rms_proj(h2d, g_mix, w_qkv, qkv_scale, bf16, tm).reshape(b, lp, 3 * ATTN_WIDTH)
    idx = _rms_proj(h2d, g_mix, w_idx, jnp.ones((1, n_idx_pad), f32), f32, tm)
    conv3 = _rms_proj(h2d, g_mix, w_conv, jnp.ones((1, 3 * c), f32), f32, tm).reshape(b, lp, 3 * c)
    gates3 = _rms_proj(h2d, g_mix, w_gates, jnp.ones((1, 2 * d), f32), f32, tm).reshape(b, lp, 2 * d)

    nqi = N_IDX_HEADS * IDX_DIM
    qi4 = idx[:, :nqi].astype(bf16).reshape(b, lp, N_IDX_HEADS, IDX_DIM).transpose(0, 2, 1, 3)
    kit = idx[:, nqi:nqi + IDX_DIM].astype(bf16).reshape(b, nt, ROW_TILE, IDX_DIM).transpose(0, 1, 3, 2)
    wi = idx[:, nqi + IDX_DIM:n_idx].reshape(b, lp, N_IDX_HEADS)

    y_attn = _attention(qi4, wi, kit, qkv, seq=seq, lp=lp, topk=topk)

    h1 = _mix(x, y_attn, conv3, gates3, conv_w[0], w_attn_out[0].astype(bf16),
              w_conv_out[0].astype(bf16), w_out[0].astype(bf16))
    return _ffn(h1, norm_ffn_g[0][None], w_gate[0].astype(bf16), w_up[0].astype(bf16),
                w_down[0].astype(bf16), norm_final_g[None])
```

```python
import functools

import jax
import jax.numpy as jnp
from jax import lax
from jax.experimental import pallas as pl
from jax.experimental.pallas import tpu as pltpu

N_META = 16
N_HEADS = 8
HEAD_DIM = 128
ATTN_WIDTH = N_HEADS * HEAD_DIM
N_IDX_HEADS = 8
IDX_DIM = 64
TOPK_MAX = 256
CONV_K = 3
EPS = 1e-6
IDX_SCALE = (N_IDX_HEADS ** -0.5) * (IDX_DIM ** -0.5)

ROW_TILE = 256
FRONT_PAD = ROW_TILE - N_META
MAX_KV_TILES_PER_STEP = 5
BISECT_CAP = 40
NEG = -1e30
LOG2E = 1.4426950408889634
VMEM_LIMIT = 56 * 1024 * 1024

f32 = jnp.float32
bf16 = jnp.bfloat16


def _rms(x, g):
    return (x * lax.rsqrt(jnp.mean(x * x, axis=-1, keepdims=True) + EPS)) * g


def _sigmoid(x):
    return 1.0 / (1.0 + jnp.exp(-x))


def _rms_proj_kernel(h_ref, g_ref, w_ref, cs_ref, o_ref):
    a = _rms(h_ref[...], g_ref[...]).astype(bf16)
    y = jnp.dot(a, w_ref[...], preferred_element_type=f32)
    o_ref[...] = (y * cs_ref[...]).astype(o_ref.dtype)


def _rms_proj(h2d, g, w, col_scale, out_dtype, tm):
    rows, d = h2d.shape
    n = w.shape[1]
    return pl.pallas_call(
        _rms_proj_kernel,
        grid=(rows // tm,),
        in_specs=[
            pl.BlockSpec((tm, d), lambda i: (i, 0)),
            pl.BlockSpec((1, d), lambda i: (0, 0)),
            pl.BlockSpec((d, n), lambda i: (0, 0)),
            pl.BlockSpec((1, n), lambda i: (0, 0)),
        ],
        out_specs=pl.BlockSpec((tm, n), lambda i: (i, 0)),
        out_shape=jax.ShapeDtypeStruct((rows, n), out_dtype),
        compiler_params=pltpu.CompilerParams(
            dimension_semantics=("arbitrary",), vmem_limit_bytes=VMEM_LIMIT),
        name="rms_proj",
    )(h2d, g, w, col_scale)


def _attn_kernel(qi_ref, wi_ref, kit_ref, q_ref, k_ref, v_ref, o_ref,
                 s_scr, bias_scr, wb_scr, thr_scr, cut_scr, m_scr, l_scr, acc_scr, flag_scr,
                 *, topk, g_tiles, lp):
    tq = ROW_TILE
    kt = ROW_TILE
    rg = 128
    nh = kt // 128
    iq = pl.program_id(1)
    j = pl.program_id(2)
    nkb = pl.num_programs(2)
    i = iq + 1
    n_tiles = i + 1
    kf = float(topk)

    def reduce_tiles(vecs, fn, init, comb, lane_red):
        outs = []
        for g in range(tq // rg):
            rows = slice(g * rg, (g + 1) * rg)
            bv = [jnp.broadcast_to(v[rows], (rg, 128)) for v in vecs]

            def body(t, acc, rows=rows, bv=bv):
                for hl in range(nh):
                    blk = s_scr[t, rows, hl * 128:(hl + 1) * 128]
                    acc = comb(acc, fn(blk, bv, t * kt + hl * 128))
                return acc

            acc = lax.fori_loop(0, n_tiles, body, jnp.full((rg, 128), init, f32))
            outs.append(lane_red(acc, axis=1, keepdims=True))
        return jnp.concatenate(outs, axis=0)

    def count_ge(v):
        return reduce_tiles([v], lambda blk, bv, k0: jnp.where(blk >= bv[0], 1.0, 0.0),
                            0.0, jnp.add, jnp.sum)

    def count_gt(v):
        return reduce_tiles([v], lambda blk, bv, k0: jnp.where(blk > bv[0], 1.0, 0.0),
                            0.0, jnp.add, jnp.sum)

    def key_index(k0):
        return (k0 + lax.broadcasted_iota(jnp.int32, (1, 128), 1)).astype(f32)

    @pl.when(j == 0)
    def _scores_and_threshold():
        m_scr[...] = jnp.full(m_scr.shape, NEG, f32)
        l_scr[...] = jnp.zeros(l_scr.shape, f32)
        acc_scr[...] = jnp.zeros(acc_scr.shape, f32)

        w = wi_ref[0] * IDX_SCALE
        for h in range(N_IDX_HEADS):
            wb_scr[h] = jnp.broadcast_to(w[:, h:h + 1], (tq, 128))
        qpos = i * tq + lax.broadcasted_iota(jnp.int32, (tq, 1), 0)

        def tile_body(t, carry):
            rmax, rmin = carry
            kt_tile = kit_ref[0, t]
            acc = None
            for h in range(N_IDX_HEADS):
                x = jnp.dot(qi_ref[0, h], kt_tile, preferred_element_type=f32)
                wb = wb_scr[h]
                term = jnp.maximum(x, 0.0) * jnp.concatenate([wb] * nh, axis=1)
                acc = term if acc is None else acc + term
            kpos = t * kt + lax.broadcasted_iota(jnp.int32, (1, kt), 1)
            vis = (kpos >= FRONT_PAD) & (kpos <= qpos)
            s_hi = jnp.where(vis, acc, -jnp.inf)
            s_lo = jnp.where(vis, acc, jnp.inf)
            s_scr[t] = s_hi
            for hl in range(nh):
                rmax = jnp.maximum(rmax, s_hi[:, hl * 128:(hl + 1) * 128])
                rmin = jnp.minimum(rmin, s_lo[:, hl * 128:(hl + 1) * 128])
            return rmax, rmin

        rmax, rmin = lax.fori_loop(
            0, n_tiles, tile_body,
            (jnp.full((tq, 128), -jnp.inf, f32), jnp.full((tq, 128), jnp.inf, f32)))
        rmax = jnp.max(rmax, axis=1, keepdims=True)
        rmin = jnp.min(rmin, axis=1, keepdims=True)

        n_vis = (qpos - (FRONT_PAD - 1)).astype(f32)
        short = n_vis <= kf

        def bis_cond(st):
            it, _, _, _, _, active = st
            return jnp.logical_and(it < BISECT_CAP, jnp.max(active) > 0.0)

        def bis_body(st):
            it, lo, hi, thr, resolved, active = st
            piv = lo + (hi - lo) * 0.5
            splits = jnp.logical_and(piv > lo, piv < hi)
            c = count_ge(piv)
            upd = jnp.logical_and(active > 0.0, splits)
            hit = jnp.logical_and(upd, c == kf)
            thr = jnp.where(hit, piv, thr)
            resolved = jnp.where(hit, 1.0, resolved)
            lo = jnp.where(jnp.logical_and(upd, c > kf), piv, lo)
            hi = jnp.where(jnp.logical_and(upd, c < kf), piv, hi)
            active = jnp.where(jnp.logical_and(upd, jnp.logical_not(hit)), 1.0, 0.0)
            return it + 1, lo, hi, thr, resolved, active

        res0 = jnp.where(short, 1.0, 0.0)
        st = (jnp.int32(0), rmin, rmax, rmin, res0, 1.0 - res0)
        _, lo, hi, thr, resolved, _ = lax.while_loop(bis_cond, bis_body, st)

        thr_scr[...] = jnp.broadcast_to(thr, (tq, 128))
        cut_scr[...] = jnp.full((tq, 128), float(lp), f32)
        flag_scr[0] = jnp.int32(0)
        unresolved = resolved < 0.5

        @pl.when(jnp.max(jnp.where(unresolved, 1.0, 0.0)) > 0.0)
        def _fallback():
            v0 = reduce_tiles([hi], lambda blk, bv, k0: jnp.where(blk <= bv[0], blk, -jnp.inf),
                              -jnp.inf, jnp.maximum, jnp.max)

            def fb_cond(st):
                _, cge, _ = st
                more = jnp.logical_and(unresolved, cge < kf)
                return jnp.max(jnp.where(more, 1.0, 0.0)) > 0.0

            def fb_body(st):
                v, cge, cgt = st
                more = jnp.logical_and(unresolved, cge < kf)
                v2 = reduce_tiles([v], lambda blk, bv, k0: jnp.where(blk < bv[0], blk, -jnp.inf),
                                  -jnp.inf, jnp.maximum, jnp.max)
                return (jnp.where(more, v2, v), jnp.where(more, count_ge(v2), cge),
                        jnp.where(more, count_gt(v2), cgt))

            v, cge, cgt = lax.while_loop(fb_cond, fb_body, (v0, count_ge(v0), count_gt(v0)))
            need = kf - cgt
            partial = jnp.logical_and(unresolved, (cge - cgt) > need)
            thr2 = jnp.where(unresolved, v, thr)
            thr_scr[...] = jnp.broadcast_to(thr2, (tq, 128))

            @pl.when(jnp.max(jnp.where(partial, 1.0, 0.0)) > 0.0)
            def _tie_cut():
                def cut_body(_, st):
                    lo_i, hi_i = st
                    mid = jnp.floor((lo_i + hi_i) * 0.5)
                    c = reduce_tiles(
                        [thr2, mid],
                        lambda blk, bv, k0: jnp.where(
                            blk == bv[0], jnp.where(key_index(k0) <= bv[1], 1.0, 0.0), 0.0),
                        0.0, jnp.add, jnp.sum)
                    ge = c >= need
                    return jnp.where(ge, lo_i, mid), jnp.where(ge, mid, hi_i)

                n_steps = max(1, int(lp).bit_length())
                _, hi_i = lax.fori_loop(
                    0, n_steps, cut_body,
                    (jnp.full((tq, 1), -1.0, f32), jnp.full((tq, 1), float(lp - 1), f32)))
                cut_scr[...] = jnp.broadcast_to(jnp.where(partial, hi_i, float(lp)), (tq, 128))
                flag_scr[0] = jnp.int32(1)

    n_sub = jnp.minimum(g_tiles, n_tiles - j * g_tiles)

    @pl.when(n_sub > 0)
    def _attend():
        ones_cols = jnp.ones((kt, HEAD_DIM), bf16)

        def sub_body(s, carry):
            t = j * g_tiles + s
            tie_path = flag_scr[0] > 0

            @pl.when(jnp.logical_not(tie_path))
            def _():
                thr = thr_scr[...]
                for hl in range(nh):
                    lanes = slice(hl * 128, (hl + 1) * 128)
                    bias_scr[:, lanes] = jnp.where(s_scr[t, :, lanes] >= thr, 0.0, NEG)

            @pl.when(tie_path)
            def _():
                thr = thr_scr[...]
                cut = cut_scr[...]
                for hl in range(nh):
                    lanes = slice(hl * 128, (hl + 1) * 128)
                    sc = s_scr[t, :, lanes]
                    keep_tie = jnp.where(key_index(t * kt + hl * 128) <= cut, 0.0, NEG)
                    bias_scr[:, lanes] = jnp.where(sc > thr, 0.0,
                                                   jnp.where(sc == thr, keep_tie, NEG))

            row0 = pl.multiple_of(s * kt, kt)
            for h in range(N_HEADS):
                cols = slice(h * HEAD_DIM, (h + 1) * HEAD_DIM)
                qh = q_ref[0, :, cols]
                kh = k_ref[0, pl.ds(row0, kt), cols]
                vh = v_ref[0, pl.ds(row0, kt), cols]
                lg = lax.dot_general(qh, kh, (((1,), (1,)), ((), ())),
                                     preferred_element_type=f32) + bias_scr[...]
                m_old = m_scr[h]
                m_new = jnp.maximum(m_old, jnp.max(lg, axis=1, keepdims=True))
                alpha = jnp.exp2(m_old - m_new)
                p = jnp.exp2(lg - jnp.concatenate([m_new] * nh, axis=1))
                v_ext = jnp.concatenate([vh, ones_cols], axis=1)
                pv = jnp.dot(p.astype(bf16), v_ext, preferred_element_type=f32)
                l_scr[h] = alpha * l_scr[h] + pv[:, HEAD_DIM:]
                acc_scr[:, cols] = alpha * acc_scr[:, cols] + pv[:, :HEAD_DIM]
                m_scr[h] = m_new
            return carry

        lax.fori_loop(0, n_sub, sub_body, 0)

    @pl.when(j == nkb - 1)
    def _finish():
        for h in range(N_HEADS):
            cols = slice(h * HEAD_DIM, (h + 1) * HEAD_DIM)
            o_ref[0, :, cols] = (acc_scr[:, cols] / l_scr[h]).astype(o_ref.dtype)


def _attention(qi4, wi, kit, qkv, *, seq, lp, topk):
    b = qkv.shape[0]
    tq = ROW_TILE
    nt = lp // tq
    g_tiles = max(g for g in range(1, MAX_KV_TILES_PER_STEP + 1) if nt % g == 0)
    nkb = nt // g_tiles
    nq = seq // tq

    def kv_block(iq, j):
        return jnp.minimum(j, (iq + 1) // g_tiles)

    kern = functools.partial(_attn_kernel, topk=topk, g_tiles=g_tiles, lp=lp)
    return pl.pallas_call(
        kern,
        grid=(b, nq, nkb),
        in_specs=[
            pl.BlockSpec((1, N_IDX_HEADS, tq, IDX_DIM), lambda bb, iq, j: (bb, 0, iq + 1, 0)),
            pl.BlockSpec((1, tq, N_IDX_HEADS), lambda bb, iq, j: (bb, iq + 1, 0)),
            pl.BlockSpec((1, nt, IDX_DIM, tq), lambda bb, iq, j: (bb, 0, 0, 0)),
            pl.BlockSpec((1, tq, ATTN_WIDTH), lambda bb, iq, j: (bb, iq + 1, 0)),
            pl.BlockSpec((1, g_tiles * tq, ATTN_WIDTH), lambda bb, iq, j: (bb, kv_block(iq, j), 1)),
            pl.BlockSpec((1, g_tiles * tq, ATTN_WIDTH), lambda bb, iq, j: (bb, kv_block(iq, j), 2)),
        ],
        out_specs=pl.BlockSpec((1, tq, ATTN_WIDTH), lambda bb, iq, j: (bb, iq, 0)),
        out_shape=jax.ShapeDtypeStruct((b, seq, ATTN_WIDTH), bf16),
        scratch_shapes=[
            pltpu.VMEM((nt, tq, tq), f32),
            pltpu.VMEM((tq, tq), f32),
            pltpu.VMEM((N_IDX_HEADS, tq, 128), f32),
            pltpu.VMEM((tq, 128), f32),
            pltpu.VMEM((tq, 128), f32),
            pltpu.VMEM((N_HEADS, tq, 128), f32),
            pltpu.VMEM((N_HEADS, tq, 128), f32),
            pltpu.VMEM((tq, ATTN_WIDTH), f32),
            pltpu.SMEM((1,), jnp.int32),
        ],
        compiler_params=pltpu.CompilerParams(
            dimension_semantics=("arbitrary", "arbitrary", "arbitrary"),
            vmem_limit_bytes=VMEM_LIMIT),
        name="dsa_attention",
    )(qi4, wi, kit, qkv, qkv, qkv)


def _mix_kernel(x_ref, ya_ref, cv_ref, cvp_ref, gt_ref, cw_ref, wao_ref, wco_ref, wout_ref,
                o_ref, u_scr):
    tm = x_ref.shape[1]
    c = cw_ref.shape[1]
    cu = cv_ref[0, :, 0:c]
    cb = cv_ref[0, :, c:2 * c]
    cc = cv_ref[0, :, 2 * c:3 * c]
    u_scr[0:8, :] = cvp_ref[0, :, 2 * c:3 * c] * cvp_ref[0, :, 0:c]
    u_scr[8:8 + tm, :] = cc * cu
    w = cw_ref[...]
    conv = (w[2:3] * u_scr[8:8 + tm, :] + w[1:2] * u_scr[7:7 + tm, :]
            + w[0:1] * u_scr[6:6 + tm, :])
    y_conv = jnp.dot((cb * conv).astype(bf16), wco_ref[...], preferred_element_type=f32)
    y_attn = jnp.dot(ya_ref[0], wao_ref[...], preferred_element_type=f32)
    d = wout_ref.shape[0]
    mixed = _sigmoid(gt_ref[0, :, 0:d]) * y_attn + _sigmoid(gt_ref[0, :, d:2 * d]) * y_conv
    o_ref[0] = x_ref[0] + jnp.dot(mixed.astype(bf16), wout_ref[...], preferred_element_type=f32)


def _mix(x, y_attn, conv3, gates3, conv_w, wao, wco, wout):
    b, seq, d = x.shape
    tm = ROW_TILE
    c = conv_w.shape[1]
    const = lambda bb, i: (0, 0)
    return pl.pallas_call(
        _mix_kernel,
        grid=(b, seq // tm),
        in_specs=[
            pl.BlockSpec((1, tm, d), lambda bb, i: (bb, i, 0)),
            pl.BlockSpec((1, tm, ATTN_WIDTH), lambda bb, i: (bb, i, 0)),
            pl.BlockSpec((1, tm, 3 * c), lambda bb, i: (bb, i + 1, 0)),
            pl.BlockSpec((1, 8, 3 * c), lambda bb, i: (bb, (i + 1) * (tm // 8) - 1, 0)),
            pl.BlockSpec((1, tm, 2 * d), lambda bb, i: (bb, i + 1, 0)),
            pl.BlockSpec((CONV_K, c), const),
            pl.BlockSpec(wao.shape, const),
            pl.BlockSpec(wco.shape, const),
            pl.BlockSpec(wout.shape, const),
        ],
        out_specs=pl.BlockSpec((1, tm, d), lambda bb, i: (bb, i, 0)),
        out_shape=jax.ShapeDtypeStruct((b, seq, d), f32),
        scratch_shapes=[pltpu.VMEM((tm + 8, c), f32)],
        compiler_params=pltpu.CompilerParams(
            dimension_semantics=("arbitrary", "arbitrary"), vmem_limit_bytes=VMEM_LIMIT),
        name="mix_merge",
    )(x, y_attn, conv3, conv3, gates3, conv_w, wao, wco, wout)


def _ffn_kernel(h_ref, g1_ref, wg_ref, wu_ref, wd_ref, g2_ref, o_ref):
    h = h_ref[0]
    f = _rms(h, g1_ref[...]).astype(bf16)
    gate = jnp.dot(f, wg_ref[...], preferred_element_type=f32)
    up = jnp.dot(f, wu_ref[...], preferred_element_type=f32)
    act = (gate * _sigmoid(gate)) * up
    h2 = h + jnp.dot(act.astype(bf16), wd_ref[...], preferred_element_type=f32)
    o_ref[0] = _rms(h2, g2_ref[...])


def _ffn(h1, g1, wg, wu, wd, g2):
    b, seq, d = h1.shape
    tm = ROW_TILE
    const = lambda bb, i: (0, 0)
    return pl.pallas_call(
        _ffn_kernel,
        grid=(b, seq // tm),
        in_specs=[
            pl.BlockSpec((1, tm, d), lambda bb, i: (bb, i, 0)),
            pl.BlockSpec((1, d), const),
            pl.BlockSpec(wg.shape, const),
            pl.BlockSpec(wu.shape, const),
            pl.BlockSpec(wd.shape, const),
            pl.BlockSpec((1, d), const),
        ],
        out_specs=pl.BlockSpec((1, tm, d), lambda bb, i: (bb, i, 0)),
        out_shape=jax.ShapeDtypeStruct((b, seq, d), f32),
        compiler_params=pltpu.CompilerParams(
            dimension_semantics=("arbitrary", "arbitrary"), vmem_limit_bytes=VMEM_LIMIT),
        name="ffn_final",
    )(h1, g1, wg, wu, wd, g2)


def kernel(x, meta_tokens, norm_mix_g, w_in, w_attn_out, conv_w, w_conv_out, w_out,
           norm_ffn_g, w_gate, w_up, w_down, norm_final_g):
    b, seq, d = x.shape
    assert w_in.shape[0] == 1, "single-layer block"
    assert seq % ROW_TILE == 0 and meta_tokens.shape[0] == N_META
    c = conv_w.shape[2]
    lp = ROW_TILE + seq
    l_real = N_META + seq
    topk = min(TOPK_MAX, l_real // 4)
    nt = lp // ROW_TILE

    meta = jnp.broadcast_to(meta_tokens[None].astype(x.dtype), (b, N_META, d))
    hp = jnp.concatenate([jnp.zeros((b, FRONT_PAD, d), x.dtype), meta, x], axis=1)
    h2d = hp.reshape(b * lp, d)

    w = w_in[0]
    o_idx = 3 * ATTN_WIDTH
    n_idx = N_IDX_HEADS * IDX_DIM + IDX_DIM + N_IDX_HEADS
    o_conv = o_idx + n_idx
    o_gate = o_conv + 3 * c
    n_idx_pad = -(-n_idx // 128) * 128
    w_qkv = w[:, :o_idx].astype(bf16)
    w_idx = jnp.pad(w[:, o_idx:o_conv], ((0, 0), (0, n_idx_pad - n_idx))).astype(bf16)
    w_conv = w[:, o_conv:o_gate].astype(bf16)
    w_gates = w[:, o_gate:].astype(bf16)
    g_mix = norm_mix_g[0][None]

    tm = 512 if (b * lp) % 512 == 0 else ROW_TILE
    qkv_scale = jnp.concatenate([jnp.full((1, ATTN_WIDTH), LOG2E * HEAD_DIM ** -0.5, f32),
                                 jnp.ones((1, 2 * ATTN_WIDTH), f32)], axis=1)
    qkv = _rms_proj(h2d, g_mix, w_qkv, qkv_scale, bf16, tm).reshape(b, lp, 3 * ATTN_WIDTH)
    idx = _rms_proj(h2d, g_mix, w_idx, jnp.ones((1, n_idx_pad), f32), f32, tm)
    conv3 = _rms_proj(h2d, g_mix, w_conv, jnp.ones((1, 3 * c), f32), f32, tm).reshape(b, lp, 3 * c)
    gates3 = _rms_proj(h2d, g_mix, w_gates, jnp.ones((1, 2 * d), f32), f32, tm).reshape(b, lp, 2 * d)

    nqi = N_IDX_HEADS * IDX_DIM
    qi4 = idx[:, :nqi].astype(bf16).reshape(b, lp, N_IDX_HEADS, IDX_DIM).transpose(0, 2, 1, 3)
    kit = idx[:, nqi:nqi + IDX_DIM].astype(bf16).reshape(b, nt, ROW_TILE, IDX_DIM).transpose(0, 1, 3, 2)
    wi = idx[:, nqi + IDX_DIM:n_idx].reshape(b, lp, N_IDX_HEADS)

    y_attn = _attention(qi4, wi, kit, qkv, seq=seq, lp=lp, topk=topk)

    h1 = _mix(x, y_attn, conv3, gates3, conv_w[0], w_attn_out[0].astype(bf16),
              w_conv_out[0].astype(bf16), w_out[0].astype(bf16))
    return _ffn(h1, norm_ffn_g[0][None], w_gate[0].astype(bf16), w_up[0].astype(bf16),
                w_down[0].astype(bf16), norm_final_g[None])
```

```python
import functools

import jax
import jax.numpy as jnp
from jax import lax
from jax.experimental import pallas as pl
from jax.experimental.pallas import tpu as pltpu

N_META = 16
N_HEADS = 8
HEAD_DIM = 128
ATTN_WIDTH = N_HEADS * HEAD_DIM
N_IDX_HEADS = 8
IDX_DIM = 64
TOPK_MAX = 256
CONV_K = 3
EPS = 1e-6
IDX_SCALE = (N_IDX_HEADS ** -0.5) * (IDX_DIM ** -0.5)

ROW_TILE = 256
FRONT_PAD = ROW_TILE - N_META
MAX_KV_TILES_PER_STEP = 5
BISECT_CAP = 32
NEG = -1e30
KEEP_ALL = 1e9
LOG2E = 1.4426950408889634
VMEM_LIMIT = 56 * 1024 * 1024

f32 = jnp.float32
bf16 = jnp.bfloat16


def _rms(x, g):
    return (x * lax.rsqrt(jnp.mean(x * x, axis=-1, keepdims=True) + EPS)) * g


def _sigmoid(x):
    return 1.0 / (1.0 + jnp.exp(-x))


def _rms_proj_kernel(h_ref, g_ref, w_ref, cs_ref, o_ref):
    a = _rms(h_ref[...], g_ref[...]).astype(bf16)
    y = jnp.dot(a, w_ref[...], preferred_element_type=f32)
    o_ref[...] = (y * cs_ref[...]).astype(o_ref.dtype)


def _rms_proj(h2d, g, w, col_scale, out_dtype, tm):
    rows, d = h2d.shape
    n = w.shape[1]
    return pl.pallas_call(
        _rms_proj_kernel,
        grid=(rows // tm,),
        in_specs=[
            pl.BlockSpec((tm, d), lambda i: (i, 0)),
            pl.BlockSpec((1, d), lambda i: (0, 0)),
            pl.BlockSpec((d, n), lambda i: (0, 0)),
            pl.BlockSpec((1, n), lambda i: (0, 0)),
        ],
        out_specs=pl.BlockSpec((tm, n), lambda i: (i, 0)),
        out_shape=jax.ShapeDtypeStruct((rows, n), out_dtype),
        compiler_params=pltpu.CompilerParams(
            dimension_semantics=("arbitrary",), vmem_limit_bytes=VMEM_LIMIT),
        name="rms_proj",
    )(h2d, g, w, col_scale)


def _attn_kernel(qi_ref, wi_ref, kit_ref, tri_ref, q_ref, k_ref, v_ref, o_ref,
                 s_scr, bias_scr, wb_scr, thr_scr, need_scr, carry_scr, m_scr, l_scr, acc_scr,
                 flag_scr, *, topk, g_tiles):
    tq = ROW_TILE
    kt = ROW_TILE
    rg = 128
    nh = kt // 128
    iq = pl.program_id(1)
    j = pl.program_id(2)
    nkb = pl.num_programs(2)
    i = iq + 1
    n_tiles = i + 1
    kf = float(topk)

    def lanes_x(v):
        return jnp.concatenate([v] * nh, axis=1)

    def reduce_tiles(vec, fn, init, comb, lane_red):
        outs = []
        for g in range(tq // rg):
            rows = slice(g * rg, (g + 1) * rg)
            bv = vec[rows]

            def body(t, acc, rows=rows, bv=bv):
                for hl in range(nh):
                    acc = comb(acc, fn(s_scr[t, rows, hl * 128:(hl + 1) * 128], bv))
                return acc

            acc = lax.fori_loop(0, n_tiles, body, jnp.full((rg, 128), init, f32))
            outs.append(jnp.broadcast_to(lane_red(acc, axis=1, keepdims=True), (rg, 128)))
        return jnp.concatenate(outs, axis=0)

    def count_ge(v):
        return reduce_tiles(v, lambda blk, bv: jnp.where(blk >= bv, 1.0, 0.0), 0.0, jnp.add, jnp.sum)

    def count_gt(v):
        return reduce_tiles(v, lambda blk, bv: jnp.where(blk > bv, 1.0, 0.0), 0.0, jnp.add, jnp.sum)

    def any_row(mask):
        return jnp.max(jnp.where(mask, 1.0, 0.0)) > 0.0

    @pl.when(j == 0)
    def _scores_and_threshold():
        m_scr[...] = jnp.full(m_scr.shape, NEG, f32)
        l_scr[...] = jnp.zeros(l_scr.shape, f32)
        acc_scr[...] = jnp.zeros(acc_scr.shape, f32)
        carry_scr[...] = jnp.zeros(carry_scr.shape, f32)

        w = wi_ref[0] * IDX_SCALE
        for h in range(N_IDX_HEADS):
            wb_scr[h] = jnp.broadcast_to(w[:, h:h + 1], (tq, 128))
        qpos1 = i * tq + lax.broadcasted_iota(jnp.int32, (tq, 1), 0)

        def tile_body(t, carry):
            rmax, rmin = carry
            kt_tile = kit_ref[0, t]
            acc = None
            for h in range(N_IDX_HEADS):
                x = jnp.dot(qi_ref[0, h], kt_tile, preferred_element_type=f32)
                term = jnp.maximum(x, 0.0) * lanes_x(wb_scr[h])
                acc = term if acc is None else acc + term
            kpos = t * kt + lax.broadcasted_iota(jnp.int32, (1, kt), 1)
            vis = (kpos >= FRONT_PAD) & (kpos <= qpos1)
            s_hi = jnp.where(vis, acc, -jnp.inf)
            s_lo = jnp.where(vis, acc, jnp.inf)
            s_scr[t] = s_hi
            for hl in range(nh):
                rmax = jnp.maximum(rmax, s_hi[:, hl * 128:(hl + 1) * 128])
                rmin = jnp.minimum(rmin, s_lo[:, hl * 128:(hl + 1) * 128])
            return rmax, rmin

        rmax, rmin = lax.fori_loop(
            0, n_tiles, tile_body,
            (jnp.full((tq, 128), -jnp.inf, f32), jnp.full((tq, 128), jnp.inf, f32)))
        rmax = jnp.broadcast_to(jnp.max(rmax, axis=1, keepdims=True), (tq, 128))
        rmin = jnp.broadcast_to(jnp.min(rmin, axis=1, keepdims=True), (tq, 128))

        qpos = i * tq + lax.broadcasted_iota(jnp.int32, (tq, 128), 0)
        n_vis = (qpos - (FRONT_PAD - 1)).astype(f32)
        short = n_vis <= kf

        zero = jnp.zeros((tq, 128), f32)
        cge0 = count_ge(zero)
        cgt0 = count_gt(zero)
        at_zero = jnp.logical_and(jnp.logical_not(short),
                                  jnp.logical_and(cgt0 < kf, cge0 >= kf))
        res0 = jnp.where(jnp.logical_or(short, at_zero), 1.0, 0.0)
        thr0 = jnp.where(short, rmin, zero)
        need0 = jnp.where(jnp.logical_and(at_zero, cge0 > kf), kf - cgt0, KEEP_ALL)
        lo0 = jnp.where(cgt0 >= kf, zero, rmin)
        hi0 = jnp.where(cge0 < kf, zero, rmax)

        def bis_cond(st):
            it, _, _, _, _, active = st
            return jnp.logical_and(it < BISECT_CAP, jnp.max(active) > 0.0)

        def bis_body(st):
            it, lo, hi, thr, resolved, active = st
            piv = lo + (hi - lo) * 0.5
            splits = jnp.logical_and(piv > lo, piv < hi)
            c = count_ge(piv)
            upd = jnp.logical_and(active > 0.0, splits)
            hit = jnp.logical_and(upd, c == kf)
            thr = jnp.where(hit, piv, thr)
            resolved = jnp.where(hit, 1.0, resolved)
            lo = jnp.where(jnp.logical_and(upd, c > kf), piv, lo)
            hi = jnp.where(jnp.logical_and(upd, c < kf), piv, hi)
            active = jnp.where(jnp.logical_and(upd, jnp.logical_not(hit)), 1.0, 0.0)
            return it + 1, lo, hi, thr, resolved, active

        st = (jnp.int32(0), lo0, hi0, thr0, res0, 1.0 - res0)
        _, lo, hi, thr, resolved, _ = lax.while_loop(bis_cond, bis_body, st)

        thr_scr[...] = thr
        need_scr[...] = need0
        flag_scr[0] = jnp.where(any_row(need0 < KEEP_ALL), 1, 0).astype(jnp.int32)
        unresolved = resolved < 0.5

        @pl.when(any_row(unresolved))
        def _fallback():
            v0 = reduce_tiles(hi, lambda blk, bv: jnp.where(blk <= bv, blk, -jnp.inf),
                              -jnp.inf, jnp.maximum, jnp.max)

            def fb_cond(st):
                _, cge, _ = st
                return any_row(jnp.logical_and(unresolved, cge < kf))

            def fb_body(st):
                v, cge, cgt = st
                more = jnp.logical_and(unresolved, cge < kf)
                v2 = reduce_tiles(v, lambda blk, bv: jnp.where(blk < bv, blk, -jnp.inf),
                                  -jnp.inf, jnp.maximum, jnp.max)
                return (jnp.where(more, v2, v), jnp.where(more, count_ge(v2), cge),
                        jnp.where(more, count_gt(v2), cgt))

            v, cge, cgt = lax.while_loop(fb_cond, fb_body, (v0, count_ge(v0), count_gt(v0)))
            need = kf - cgt
            partial = jnp.logical_and(unresolved, (cge - cgt) > need)
            thr_scr[...] = jnp.where(unresolved, v, thr)
            need_scr[...] = jnp.where(partial, need, need0)

            @pl.when(any_row(partial))
            def _():
                flag_scr[0] = jnp.int32(1)

    n_sub = jnp.minimum(g_tiles, n_tiles - j * g_tiles)

    @pl.when(n_sub > 0)
    def _attend():
        ones_cols = jnp.ones((kt, HEAD_DIM), bf16)

        def sub_body(s, carry):
            t = j * g_tiles + s
            tie_path = flag_scr[0] > 0

            @pl.when(jnp.logical_not(tie_path))
            def _():
                thr = thr_scr[...]
                for hl in range(nh):
                    lanes = slice(hl * 128, (hl + 1) * 128)
                    bias_scr[:, lanes] = jnp.where(s_scr[t, :, lanes] >= thr, 0.0, NEG)

            @pl.when(tie_path)
            def _():
                sc = s_scr[t]
                thr = lanes_x(thr_scr[...])
                tie = sc == thr
                counts = jnp.dot(jnp.where(tie, 1.0, 0.0).astype(bf16), tri_ref[...],
                                 preferred_element_type=f32)
                seen = carry_scr[...]
                keep = (counts[:, :kt] + lanes_x(seen)) <= lanes_x(need_scr[...])
                bias_scr[...] = jnp.where(sc > thr, 0.0,
                                          jnp.where(jnp.logical_and(tie, keep), 0.0, NEG))
                carry_scr[...] = seen + counts[:, kt:]

            row0 = pl.multiple_of(s * kt, kt)
            for h in range(N_HEADS):
                cols = slice(h * HEAD_DIM, (h + 1) * HEAD_DIM)
                qh = q_ref[0, :, cols]
                kh = k_ref[0, pl.ds(row0, kt), cols]
                vh = v_ref[0, pl.ds(row0, kt), cols]
                lg = lax.dot_general(qh, kh, (((1,), (1,)), ((), ())),
                                     preferred_element_type=f32) + bias_scr[...]
                m_old = m_scr[h]
                m_new = jnp.maximum(m_old, jnp.max(lg, axis=1, keepdims=True))
                alpha = jnp.exp2(m_old - m_new)
                p = jnp.exp2(lg - lanes_x(m_new))
                v_ext = jnp.concatenate([vh, ones_cols], axis=1)
                pv = jnp.dot(p.astype(bf16), v_ext, preferred_element_type=f32)
                l_scr[h] = alpha * l_scr[h] + pv[:, HEAD_DIM:]
                acc_scr[:, cols] = alpha * acc_scr[:, cols] + pv[:, :HEAD_DIM]
                m_scr[h] = m_new
            return carry

        lax.fori_loop(0, n_sub, sub_body, 0)

    @pl.when(j == nkb - 1)
    def _finish():
        for h in range(N_HEADS):
            cols = slice(h * HEAD_DIM, (h + 1) * HEAD_DIM)
            o_ref[0, :, cols] = (acc_scr[:, cols] / l_scr[h]).astype(o_ref.dtype)


def _attention(qi4, wi, kit, qkv, *, seq, lp, topk):
    b = qkv.shape[0]
    tq = ROW_TILE
    nt = lp // tq
    g_tiles = max(g for g in range(1, MAX_KV_TILES_PER_STEP + 1) if nt % g == 0)
    nkb = nt // g_tiles
    nq = seq // tq

    def kv_block(iq, j):
        return jnp.minimum(j, (iq + 1) // g_tiles)

    ka = lax.broadcasted_iota(jnp.int32, (tq, tq + 128), 0)
    kb = lax.broadcasted_iota(jnp.int32, (tq, tq + 128), 1)
    tri = jnp.where(jnp.logical_or(ka <= kb, kb >= tq), 1.0, 0.0).astype(bf16)

    kern = functools.partial(_attn_kernel, topk=topk, g_tiles=g_tiles)
    return pl.pallas_call(
        kern,
        grid=(b, nq, nkb),
        in_specs=[
            pl.BlockSpec((1, N_IDX_HEADS, tq, IDX_DIM), lambda bb, iq, j: (bb, 0, iq + 1, 0)),
            pl.BlockSpec((1, tq, N_IDX_HEADS), lambda bb, iq, j: (bb, iq + 1, 0)),
            pl.BlockSpec((1, nt, IDX_DIM, tq), lambda bb, iq, j: (bb, 0, 0, 0)),
            pl.BlockSpec((tq, tq + 128), lambda bb, iq, j: (0, 0)),
            pl.BlockSpec((1, tq, ATTN_WIDTH), lambda bb, iq, j: (bb, iq + 1, 0)),
            pl.BlockSpec((1, g_tiles * tq, ATTN_WIDTH), lambda bb, iq, j: (bb, kv_block(iq, j), 1)),
            pl.BlockSpec((1, g_tiles * tq, ATTN_WIDTH), lambda bb, iq, j: (bb, kv_block(iq, j), 2)),
        ],
        out_specs=pl.BlockSpec((1, tq, ATTN_WIDTH), lambda bb, iq, j: (bb, iq, 0)),
        out_shape=jax.ShapeDtypeStruct((b, seq, ATTN_WIDTH), bf16),
        scratch_shapes=[
            pltpu.VMEM((nt, tq, tq), f32),
            pltpu.VMEM((tq, tq), f32),
            pltpu.VMEM((N_IDX_HEADS, tq, 128), f32),
            pltpu.VMEM((tq, 128), f32),
            pltpu.VMEM((tq, 128), f32),
            pltpu.VMEM((tq, 128), f32),
            pltpu.VMEM((N_HEADS, tq, 128), f32),
            pltpu.VMEM((N_HEADS, tq, 128), f32),
            pltpu.VMEM((tq, ATTN_WIDTH), f32),
            pltpu.SMEM((1,), jnp.int32),
        ],
        compiler_params=pltpu.CompilerParams(
            dimension_semantics=("arbitrary", "arbitrary", "arbitrary"),
            vmem_limit_bytes=VMEM_LIMIT),
        name="dsa_attention",
    )(qi4, wi, kit, tri, qkv, qkv, qkv)


def _mix_kernel(x_ref, ya_ref, cv_ref, cvp_ref, gt_ref, cw_ref, wao_ref, wco_ref, wout_ref,
                o_ref, u_scr):
    tm = x_ref.shape[1]
    c = cw_ref.shape[1]
    cu = cv_ref[0, :, 0:c]
    cb = cv_ref[0, :, c:2 * c]
    cc = cv_ref[0, :, 2 * c:3 * c]
    u_scr[0:8, :] = cvp_ref[0, :, 2 * c:3 * c] * cvp_ref[0, :, 0:c]
    u_scr[8:8 + tm, :] = cc * cu
    w = cw_ref[...]
    conv = (w[2:3] * u_scr[8:8 + tm, :] + w[1:2] * u_scr[7:7 + tm, :]
            + w[0:1] * u_scr[6:6 + tm, :])
    y_conv = jnp.dot((cb * conv).astype(bf16), wco_ref[...], preferred_element_type=f32)
    y_attn = jnp.dot(ya_ref[0], wao_ref[...], preferred_element_type=f32)
    d = wout_ref.shape[0]
    mixed = _sigmoid(gt_ref[0, :, 0:d]) * y_attn + _sigmoid(gt_ref[0, :, d:2 * d]) * y_conv
    o_ref[0] = x_ref[0] + jnp.dot(mixed.astype(bf16), wout_ref[...], preferred_element_type=f32)


def _mix(x, y_attn, conv3, gates3, conv_w, wao, wco, wout):
    b, seq, d = x.shape
    tm = ROW_TILE
    c = conv_w.shape[1]
    const = lambda bb, i: (0, 0)
    return pl.pallas_call(
        _mix_kernel,
        grid=(b, seq // tm),
        in_specs=[
            pl.BlockSpec((1, tm, d), lambda bb, i: (bb, i, 0)),
            pl.BlockSpec((1, tm, ATTN_WIDTH), lambda bb, i: (bb, i, 0)),
            pl.BlockSpec((1, tm, 3 * c), lambda bb, i: (bb, i + 1, 0)),
            pl.BlockSpec((1, 8, 3 * c), lambda bb, i: (bb, (i + 1) * (tm // 8) - 1, 0)),
            pl.BlockSpec((1, tm, 2 * d), lambda bb, i: (bb, i + 1, 0)),
            pl.BlockSpec((CONV_K, c), const),
            pl.BlockSpec(wao.shape, const),
            pl.BlockSpec(wco.shape, const),
            pl.BlockSpec(wout.shape, const),
        ],
        out_specs=pl.BlockSpec((1, tm, d), lambda bb, i: (bb, i, 0)),
        out_shape=jax.ShapeDtypeStruct((b, seq, d), f32),
        scratch_shapes=[pltpu.VMEM((tm + 8, c), f32)],
        compiler_params=pltpu.CompilerParams(
            dimension_semantics=("arbitrary", "arbitrary"), vmem_limit_bytes=VMEM_LIMIT),
        name="mix_merge",
    )(x, y_attn, conv3, conv3, gates3, conv_w, wao, wco, wout)


def _ffn_kernel(h_ref, g1_ref, wg_ref, wu_ref, wd_ref, g2_ref, o_ref):
    h = h_ref[0]
    f = _rms(h, g1_ref[...]).astype(bf16)
    gate = jnp.dot(f, wg_ref[...], preferred_element_type=f32)
    up = jnp.dot(f, wu_ref[...], preferred_element_type=f32)
    act = (gate * _sigmoid(gate)) * up
    h2 = h + jnp.dot(act.astype(bf16), wd_ref[...], preferred_element_type=f32)
    o_ref[0] = _rms(h2, g2_ref[...])


def _ffn(h1, g1, wg, wu, wd, g2):
    b, seq, d = h1.shape
    tm = ROW_TILE
    const = lambda bb, i: (0, 0)
    return pl.pallas_call(
        _ffn_kernel,
        grid=(b, seq // tm),
        in_specs=[
            pl.BlockSpec((1, tm, d), lambda bb, i: (bb, i, 0)),
            pl.BlockSpec((1, d), const),
            pl.BlockSpec(wg.shape, const),
            pl.BlockSpec(wu.shape, const),
            pl.BlockSpec(wd.shape, const),
            pl.BlockSpec((1, d), const),
        ],
        out_specs=pl.BlockSpec((1, tm, d), lambda bb, i: (bb, i, 0)),
        out_shape=jax.ShapeDtypeStruct((b, seq, d), f32),
        compiler_params=pltpu.CompilerParams(
            dimension_semantics=("arbitrary", "arbitrary"), vmem_limit_bytes=VMEM_LIMIT),
        name="ffn_final",
    )(h1, g1, wg, wu, wd, g2)


def kernel(x, meta_tokens, norm_mix_g, w_in, w_attn_out, conv_w, w_conv_out, w_out,
           norm_ffn_g, w_gate, w_up, w_down, norm_final_g):
    b, seq, d = x.shape
    assert w_in.shape[0] == 1, "single-layer block"
    assert seq % ROW_TILE == 0 and meta_tokens.shape[0] == N_META
    c = conv_w.shape[2]
    lp = ROW_TILE + seq
    l_real = N_META + seq
    topk = min(TOPK_MAX, l_real // 4)
    nt = lp // ROW_TILE

    meta = jnp.broadcast_to(meta_tokens[None].astype(x.dtype), (b, N_META, d))
    hp = jnp.concatenate([jnp.zeros((b, FRONT_PAD, d), x.dtype), meta, x], axis=1)
    h2d = hp.reshape(b * lp, d)

    w = w_in[0]
    o_idx = 3 * ATTN_WIDTH
    n_idx = N_IDX_HEADS * IDX_DIM + IDX_DIM + N_IDX_HEADS
    o_conv = o_idx + n_idx
    o_gate = o_conv + 3 * c
    n_idx_pad = -(-n_idx // 128) * 128
    w_qkv = w[:, :o_idx].astype(bf16)
    w_idx = jnp.pad(w[:, o_idx:o_conv], ((0, 0), (0, n_idx_pad - n_idx))).astype(bf16)
    w_conv = w[:, o_conv:o_gate].astype(bf16)
    w_gates = w[:, o_gate:].astype(bf16)
    g_mix = norm_mix_g[0][None]

    tm = 512 if (b * lp) % 512 == 0 else ROW_TILE
    qkv_scale = jnp.concatenate([jnp.full((1, ATTN_WIDTH), LOG2E * HEAD_DIM ** -0.5, f32),
                                 jnp.ones((1, 2 * ATTN_WIDTH), f32)], axis=1)
    qkv = _rms_proj(h2d, g_mix, w_qkv, qkv_scale, bf16, tm).reshape(b, lp, 3 * ATTN_WIDTH)
    idx = _rms_proj(h2d, g_mix, w_idx, jnp.ones((1, n_idx_pad), f32), f32, tm)
    conv3 = _rms_proj(h2d, g_mix, w_conv, jnp.ones((1, 3 * c), f32), f32, tm).reshape(b, lp, 3 * c)
    gates3 = _rms_proj(h2d, g_mix, w_gates, jnp.ones((1, 2 * d), f32), f32, tm).reshape(b, lp, 2 * d)

    nqi = N_IDX_HEADS * IDX_DIM
    qi4 = idx[:, :nqi].astype(bf16).reshape(b, lp, N_IDX_HEADS, IDX_DIM).transpose(0, 2, 1, 3)
    kit = idx[:, nqi:nqi + IDX_DIM].astype(bf16).reshape(b, nt, ROW_TILE, IDX_DIM).transpose(0, 1, 3, 2)
    wi = idx[:, nqi + IDX_DIM:n_idx].reshape(b, lp, N_IDX_HEADS)

    y_attn = _attention(qi4, wi, kit, qkv, seq=seq, lp=lp, topk=topk)

    h1 = _mix(x, y_attn, conv3, gates3, conv_w[0], w_attn_out[0].astype(bf16),
              w_conv_out[0].astype(bf16), w_out[0].astype(bf16))
    return _ffn(h1, norm_ffn_g[0][None], w_gate[0].astype(bf16), w_up[0].astype(bf16),
                w_down[0].astype(bf16), norm_final_g[None])
```

```python
import functools

import jax
import jax.numpy as jnp
from jax import lax
from jax.experimental import pallas as pl
from jax.experimental.pallas import tpu as pltpu

N_META = 16
N_HEADS = 8
HEAD_DIM = 128
ATTN_WIDTH = N_HEADS * HEAD_DIM
N_IDX_HEADS = 8
IDX_DIM = 64
TOPK_MAX = 256
CONV_K = 3
EPS = 1e-6
IDX_SCALE = (N_IDX_HEADS ** -0.5) * (IDX_DIM ** -0.5)

ROW_TILE = 256
FRONT_PAD = ROW_TILE - N_META
MAX_KV_TILES_PER_STEP = 5
BISECT_CAP = 32
NEG = -1e30
KEEP_ALL = 1e9
LANE_LIST = 12
LIST_ROWS = 16
LOG2E = 1.4426950408889634
VMEM_LIMIT = 56 * 1024 * 1024

f32 = jnp.float32
bf16 = jnp.bfloat16


def _rms(x, g):
    return (x * lax.rsqrt(jnp.mean(x * x, axis=-1, keepdims=True) + EPS)) * g


def _sigmoid(x):
    return 1.0 / (1.0 + jnp.exp(-x))


def _rms_proj_kernel(h_ref, g_ref, w_ref, cs_ref, o_ref):
    a = _rms(h_ref[...], g_ref[...]).astype(bf16)
    y = jnp.dot(a, w_ref[...], preferred_element_type=f32)
    o_ref[...] = (y * cs_ref[...]).astype(o_ref.dtype)


def _rms_proj(h2d, g, w, col_scale, out_dtype, tm):
    rows, d = h2d.shape
    n = w.shape[1]
    return pl.pallas_call(
        _rms_proj_kernel,
        grid=(rows // tm,),
        in_specs=[
            pl.BlockSpec((tm, d), lambda i: (i, 0)),
            pl.BlockSpec((1, d), lambda i: (0, 0)),
            pl.BlockSpec((d, n), lambda i: (0, 0)),
            pl.BlockSpec((1, n), lambda i: (0, 0)),
        ],
        out_specs=pl.BlockSpec((tm, n), lambda i: (i, 0)),
        out_shape=jax.ShapeDtypeStruct((rows, n), out_dtype),
        compiler_params=pltpu.CompilerParams(
            dimension_semantics=("arbitrary",), vmem_limit_bytes=VMEM_LIMIT),
        name="rms_proj",
    )(h2d, g, w, col_scale)


def _attn_kernel(qi_ref, wi_ref, kit_ref, tri_ref, q_ref, k_ref, v_ref, o_ref,
                 s_scr, cand_scr, bias_scr, wb_scr, thr_scr, need_scr, carry_scr, m_scr, l_scr,
                 acc_scr, flag_scr, *, topk, g_tiles):
    tq = ROW_TILE
    kt = ROW_TILE
    rg = 128
    nh = kt // 128
    iq = pl.program_id(1)
    j = pl.program_id(2)
    nkb = pl.num_programs(2)
    i = iq + 1
    n_tiles = i + 1
    kf = float(topk)

    def lanes_x(v):
        return jnp.concatenate([v] * nh, axis=1)

    def reduce_scores(vec, fn, init, comb, lane_red):
        outs = []
        for g in range(tq // rg):
            rows = slice(g * rg, (g + 1) * rg)
            bv = vec[rows]

            def body(t, acc, rows=rows, bv=bv):
                for hl in range(nh):
                    acc = comb(acc, fn(s_scr[t, rows, hl * 128:(hl + 1) * 128], bv))
                return acc

            acc = lax.fori_loop(0, n_tiles, body, jnp.full((rg, 128), init, f32))
            outs.append(jnp.broadcast_to(lane_red(acc, axis=1, keepdims=True), (rg, 128)))
        return jnp.concatenate(outs, axis=0)

    def reduce_lists(vec, fn, init, comb, lane_red):
        outs = []
        for g in range(tq // rg):
            rows = slice(g * rg, (g + 1) * rg)
            bv = vec[rows]
            acc = jnp.full((rg, 128), init, f32)
            for k in range(LANE_LIST):
                acc = comb(acc, fn(cand_scr[k, rows, :], bv))
            outs.append(jnp.broadcast_to(lane_red(acc, axis=1, keepdims=True), (rg, 128)))
        return jnp.concatenate(outs, axis=0)

    def any_row(mask):
        return jnp.max(jnp.where(mask, 1.0, 0.0)) > 0.0

    @pl.when(j == 0)
    def _scores_and_threshold():
        m_scr[...] = jnp.full(m_scr.shape, NEG, f32)
        l_scr[...] = jnp.zeros(l_scr.shape, f32)
        acc_scr[...] = jnp.zeros(acc_scr.shape, f32)
        carry_scr[...] = jnp.zeros(carry_scr.shape, f32)

        w = wi_ref[0] * IDX_SCALE
        for h in range(N_IDX_HEADS):
            wb_scr[h] = jnp.broadcast_to(w[:, h:h + 1], (tq, 128))
        qpos1 = i * tq + lax.broadcasted_iota(jnp.int32, (tq, 1), 0)

        def tile_body(t, carry):
            rmax, rmin = carry
            kt_tile = kit_ref[0, t]
            acc = None
            for h in range(N_IDX_HEADS):
                x = jnp.dot(qi_ref[0, h], kt_tile, preferred_element_type=f32)
                term = jnp.maximum(x, 0.0) * lanes_x(wb_scr[h])
                acc = term if acc is None else acc + term
            kpos = t * kt + lax.broadcasted_iota(jnp.int32, (1, kt), 1)
            vis = (kpos >= FRONT_PAD) & (kpos <= qpos1)
            s_hi = jnp.where(vis, acc, -jnp.inf)
            s_lo = jnp.where(vis, acc, jnp.inf)
            s_scr[t] = s_hi
            for hl in range(nh):
                rmax = jnp.maximum(rmax, s_hi[:, hl * 128:(hl + 1) * 128])
                rmin = jnp.minimum(rmin, s_lo[:, hl * 128:(hl + 1) * 128])
            return rmax, rmin

        rmax, rmin = lax.fori_loop(
            0, n_tiles, tile_body,
            (jnp.full((tq, 128), -jnp.inf, f32), jnp.full((tq, 128), jnp.inf, f32)))
        rmax = jnp.broadcast_to(jnp.max(rmax, axis=1, keepdims=True), (tq, 128))
        rmin = jnp.broadcast_to(jnp.min(rmin, axis=1, keepdims=True), (tq, 128))

        qpos = i * tq + lax.broadcasted_iota(jnp.int32, (tq, 128), 0)
        n_vis = (qpos - (FRONT_PAD - 1)).astype(f32)
        short = n_vis <= kf

        def search(reduce):
            def count_ge(v):
                return reduce(v, lambda blk, bv: jnp.where(blk >= bv, 1.0, 0.0), 0.0, jnp.add, jnp.sum)

            def count_gt(v):
                return reduce(v, lambda blk, bv: jnp.where(blk > bv, 1.0, 0.0), 0.0, jnp.add, jnp.sum)

            zero = jnp.zeros((tq, 128), f32)
            cge0 = count_ge(zero)
            cgt0 = count_gt(zero)
            at_zero = jnp.logical_and(jnp.logical_not(short),
                                      jnp.logical_and(cgt0 < kf, cge0 >= kf))
            res0 = jnp.where(jnp.logical_or(short, at_zero), 1.0, 0.0)
            thr0 = jnp.where(short, rmin, zero)
            need0 = jnp.where(jnp.logical_and(at_zero, cge0 > kf), kf - cgt0, KEEP_ALL)
            lo0 = jnp.where(cgt0 >= kf, zero, rmin)
            hi0 = jnp.where(cge0 < kf, zero, rmax)

            def bis_cond(st):
                it, _, _, _, _, active = st
                return jnp.logical_and(it < BISECT_CAP, jnp.max(active) > 0.0)

            def bis_body(st):
                it, lo, hi, thr, resolved, active = st
                piv = lo + (hi - lo) * 0.5
                splits = jnp.logical_and(piv > lo, piv < hi)
                c = count_ge(piv)
                upd = jnp.logical_and(active > 0.0, splits)
                hit = jnp.logical_and(upd, c == kf)
                thr = jnp.where(hit, piv, thr)
                resolved = jnp.where(hit, 1.0, resolved)
                lo = jnp.where(jnp.logical_and(upd, c > kf), piv, lo)
                hi = jnp.where(jnp.logical_and(upd, c < kf), piv, hi)
                active = jnp.where(jnp.logical_and(upd, jnp.logical_not(hit)), 1.0, 0.0)
                return it + 1, lo, hi, thr, resolved, active

            st = (jnp.int32(0), lo0, hi0, thr0, res0, 1.0 - res0)
            _, lo, hi, thr, resolved, _ = lax.while_loop(bis_cond, bis_body, st)

            thr_scr[...] = thr
            need_scr[...] = need0
            flag_scr[0] = jnp.where(any_row(need0 < KEEP_ALL), 1, 0).astype(jnp.int32)
            unresolved = resolved < 0.5

            @pl.when(any_row(unresolved))
            def _fallback():
                v0 = reduce(hi, lambda blk, bv: jnp.where(blk <= bv, blk, -jnp.inf),
                            -jnp.inf, jnp.maximum, jnp.max)

                def fb_cond(st):
                    _, cge, _ = st
                    return any_row(jnp.logical_and(unresolved, cge < kf))

                def fb_body(st):
                    v, cge, cgt = st
                    more = jnp.logical_and(unresolved, cge < kf)
                    v2 = reduce(v, lambda blk, bv: jnp.where(blk < bv, blk, -jnp.inf),
                                -jnp.inf, jnp.maximum, jnp.max)
                    return (jnp.where(more, v2, v), jnp.where(more, count_ge(v2), cge),
                            jnp.where(more, count_gt(v2), cgt))

                v, cge, cgt = lax.while_loop(fb_cond, fb_body, (v0, count_ge(v0), count_gt(v0)))
                need = kf - cgt
                partial = jnp.logical_and(unresolved, (cge - cgt) > need)
                thr_scr[...] = jnp.where(unresolved, v, thr)
                need_scr[...] = jnp.where(partial, need, need0)

                @pl.when(any_row(partial))
                def _():
                    flag_scr[0] = jnp.int32(1)

        def list_group(g, carry):
            r0 = pl.multiple_of(g * LIST_ROWS, LIST_ROWS)

            def insert_tile(t, lists):
                lists = list(lists)
                for hl in range(nh):
                    x = s_scr[t, pl.ds(r0, LIST_ROWS), hl * 128:(hl + 1) * 128]
                    for k in range(LANE_LIST):
                        top = jnp.maximum(lists[k], x)
                        x = jnp.minimum(lists[k], x)
                        lists[k] = top
                return tuple(lists)

            lists = lax.fori_loop(
                0, n_tiles, insert_tile,
                tuple(jnp.full((LIST_ROWS, 128), -jnp.inf, f32) for _ in range(LANE_LIST)))
            for k in range(LANE_LIST):
                cand_scr[k, pl.ds(r0, LIST_ROWS), :] = lists[k]
            return carry

        lax.fori_loop(0, tq // LIST_ROWS, list_group, 0)

        search(reduce_lists)

        @pl.when(any_row(cand_scr[LANE_LIST - 1] >= thr_scr[...]))
        def _():
            search(reduce_scores)

    n_sub = jnp.minimum(g_tiles, n_tiles - j * g_tiles)

    @pl.when(n_sub > 0)
    def _attend():
        ones_cols = jnp.ones((kt, HEAD_DIM), bf16)

        def sub_body(s, carry):
            t = j * g_tiles + s
            tie_path = flag_scr[0] > 0

            @pl.when(jnp.logical_not(tie_path))
            def _():
                thr = thr_scr[...]
                for hl in range(nh):
                    lanes = slice(hl * 128, (hl + 1) * 128)
                    bias_scr[:, lanes] = jnp.where(s_scr[t, :, lanes] >= thr, 0.0, NEG)

            @pl.when(tie_path)
            def _():
                sc = s_scr[t]
                thr = lanes_x(thr_scr[...])
                tie = sc == thr
                counts = jnp.dot(jnp.where(tie, 1.0, 0.0).astype(bf16), tri_ref[...],
                                 preferred_element_type=f32)
                seen = carry_scr[...]
                keep = (counts[:, :kt] + lanes_x(seen)) <= lanes_x(need_scr[...])
                bias_scr[...] = jnp.where(sc > thr, 0.0,
                                          jnp.where(jnp.logical_and(tie, keep), 0.0, NEG))
                carry_scr[...] = seen + counts[:, kt:]

            row0 = pl.multiple_of(s * kt, kt)
            for h in range(N_HEADS):
                cols = slice(h * HEAD_DIM, (h + 1) * HEAD_DIM)
                qh = q_ref[0, :, cols]
                kh = k_ref[0, pl.ds(row0, kt), cols]
                vh = v_ref[0, pl.ds(row0, kt), cols]
                lg = lax.dot_general(qh, kh, (((1,), (1,)), ((), ())),
                                     preferred_element_type=f32) + bias_scr[...]
                m_old = m_scr[h]
                m_new = jnp.maximum(m_old, jnp.max(lg, axis=1, keepdims=True))
                alpha = jnp.exp2(m_old - m_new)
                p = jnp.exp2(lg - lanes_x(m_new))
                v_ext = jnp.concatenate([vh, ones_cols], axis=1)
                pv = jnp.dot(p.astype(bf16), v_ext, preferred_element_type=f32)
                l_scr[h] = alpha * l_scr[h] + pv[:, HEAD_DIM:]
                acc_scr[:, cols] = alpha * acc_scr[:, cols] + pv[:, :HEAD_DIM]
                m_scr[h] = m_new
            return carry

        lax.fori_loop(0, n_sub, sub_body, 0)

    @pl.when(j == nkb - 1)
    def _finish():
        for h in range(N_HEADS):
            cols = slice(h * HEAD_DIM, (h + 1) * HEAD_DIM)
            o_ref[0, :, cols] = (acc_scr[:, cols] / l_scr[h]).astype(o_ref.dtype)


def _attention(qi4, wi, kit, qkv, *, seq, lp, topk):
    b = qkv.shape[0]
    tq = ROW_TILE
    nt = lp // tq
    g_tiles = max(g for g in range(1, MAX_KV_TILES_PER_STEP + 1) if nt % g == 0)
    nkb = nt // g_tiles
    nq = seq // tq

    def kv_block(iq, j):
        return jnp.minimum(j, (iq + 1) // g_tiles)

    ka = lax.broadcasted_iota(jnp.int32, (tq, tq + 128), 0)
    kb = lax.broadcasted_iota(jnp.int32, (tq, tq + 128), 1)
    tri = jnp.where(jnp.logical_or(ka <= kb, kb >= tq), 1.0, 0.0).astype(bf16)

    kern = functools.partial(_attn_kernel, topk=topk, g_tiles=g_tiles)
    return pl.pallas_call(
        kern,
        grid=(b, nq, nkb),
        in_specs=[
            pl.BlockSpec((1, N_IDX_HEADS, tq, IDX_DIM), lambda bb, iq, j: (bb, 0, iq + 1, 0)),
            pl.BlockSpec((1, tq, N_IDX_HEADS), lambda bb, iq, j: (bb, iq + 1, 0)),
            pl.BlockSpec((1, nt, IDX_DIM, tq), lambda bb, iq, j: (bb, 0, 0, 0)),
            pl.BlockSpec((tq, tq + 128), lambda bb, iq, j: (0, 0)),
            pl.BlockSpec((1, tq, ATTN_WIDTH), lambda bb, iq, j: (bb, iq + 1, 0)),
            pl.BlockSpec((1, g_tiles * tq, ATTN_WIDTH), lambda bb, iq, j: (bb, kv_block(iq, j), 1)),
            pl.BlockSpec((1, g_tiles * tq, ATTN_WIDTH), lambda bb, iq, j: (bb, kv_block(iq, j), 2)),
        ],
        out_specs=pl.BlockSpec((1, tq, ATTN_WIDTH), lambda bb, iq, j: (bb, iq, 0)),
        out_shape=jax.ShapeDtypeStruct((b, seq, ATTN_WIDTH), bf16),
        scratch_shapes=[
            pltpu.VMEM((nt, tq, tq), f32),
            pltpu.VMEM((LANE_LIST, tq, 128), f32),
            pltpu.VMEM((tq, tq), f32),
            pltpu.VMEM((N_IDX_HEADS, tq, 128), f32),
            pltpu.VMEM((tq, 128), f32),
            pltpu.VMEM((tq, 128), f32),
            pltpu.VMEM((tq, 128), f32),
            pltpu.VMEM((N_HEADS, tq, 128), f32),
            pltpu.VMEM((N_HEADS, tq, 128), f32),
            pltpu.VMEM((tq, ATTN_WIDTH), f32),
            pltpu.SMEM((1,), jnp.int32),
        ],
        compiler_params=pltpu.CompilerParams(
            dimension_semantics=("arbitrary", "arbitrary", "arbitrary"),
            vmem_limit_bytes=VMEM_LIMIT),
        name="dsa_attention",
    )(qi4, wi, kit, tri, qkv, qkv, qkv)


def _mix_kernel(x_ref, ya_ref, cv_ref, cvp_ref, gt_ref, cw_ref, wao_ref, wco_ref, wout_ref,
                o_ref, u_scr):
    tm = x_ref.shape[1]
    c = cw_ref.shape[1]
    cu = cv_ref[0, :, 0:c]
    cb = cv_ref[0, :, c:2 * c]
    cc = cv_ref[0, :, 2 * c:3 * c]
    u_scr[0:8, :] = cvp_ref[0, :, 2 * c:3 * c] * cvp_ref[0, :, 0:c]
    u_scr[8:8 + tm, :] = cc * cu
    w = cw_ref[...]
    conv = (w[2:3] * u_scr[8:8 + tm, :] + w[1:2] * u_scr[7:7 + tm, :]
            + w[0:1] * u_scr[6:6 + tm, :])
    y_conv = jnp.dot((cb * conv).astype(bf16), wco_ref[...], preferred_element_type=f32)
    y_attn = jnp.dot(ya_ref[0], wao_ref[...], preferred_element_type=f32)
    d = wout_ref.shape[0]
    mixed = _sigmoid(gt_ref[0, :, 0:d]) * y_attn + _sigmoid(gt_ref[0, :, d:2 * d]) * y_conv
    o_ref[0] = x_ref[0] + jnp.dot(mixed.astype(bf16), wout_ref[...], preferred_element_type=f32)


def _mix(x, y_attn, conv3, gates3, conv_w, wao, wco, wout):
    b, seq, d = x.shape
    tm = ROW_TILE
    c = conv_w.shape[1]
    const = lambda bb, i: (0, 0)
    return pl.pallas_call(
        _mix_kernel,
        grid=(b, seq // tm),
        in_specs=[
            pl.BlockSpec((1, tm, d), lambda bb, i: (bb, i, 0)),
            pl.BlockSpec((1, tm, ATTN_WIDTH), lambda bb, i: (bb, i, 0)),
            pl.BlockSpec((1, tm, 3 * c), lambda bb, i: (bb, i + 1, 0)),
            pl.BlockSpec((1, 8, 3 * c), lambda bb, i: (bb, (i + 1) * (tm // 8) - 1, 0)),
            pl.BlockSpec((1, tm, 2 * d), lambda bb, i: (bb, i + 1, 0)),
            pl.BlockSpec((CONV_K, c), const),
            pl.BlockSpec(wao.shape, const),
            pl.BlockSpec(wco.shape, const),
            pl.BlockSpec(wout.shape, const),
        ],
        out_specs=pl.BlockSpec((1, tm, d), lambda bb, i: (bb, i, 0)),
        out_shape=jax.ShapeDtypeStruct((b, seq, d), f32),
        scratch_shapes=[pltpu.VMEM((tm + 8, c), f32)],
        compiler_params=pltpu.CompilerParams(
            dimension_semantics=("arbitrary", "arbitrary"), vmem_limit_bytes=VMEM_LIMIT),
        name="mix_merge",
    )(x, y_attn, conv3, conv3, gates3, conv_w, wao, wco, wout)


def _ffn_kernel(h_ref, g1_ref, wg_ref, wu_ref, wd_ref, g2_ref, o_ref):
    h = h_ref[0]
    f = _rms(h, g1_ref[...]).astype(bf16)
    gate = jnp.dot(f, wg_ref[...], preferred_element_type=f32)
    up = jnp.dot(f, wu_ref[...], preferred_element_type=f32)
    act = (gate * _sigmoid(gate)) * up
    h2 = h + jnp.dot(act.astype(bf16), wd_ref[...], preferred_element_type=f32)
    o_ref[0] = _rms(h2, g2_ref[...])


def _ffn(h1, g1, wg, wu, wd, g2):
    b, seq, d = h1.shape
    tm = ROW_TILE
    const = lambda bb, i: (0, 0)
    return pl.pallas_call(
        _ffn_kernel,
        grid=(b, seq // tm),
        in_specs=[
            pl.BlockSpec((1, tm, d), lambda bb, i: (bb, i, 0)),
            pl.BlockSpec((1, d), const),
            pl.BlockSpec(wg.shape, const),
            pl.BlockSpec(wu.shape, const),
            pl.BlockSpec(wd.shape, const),
            pl.BlockSpec((1, d), const),
        ],
        out_specs=pl.BlockSpec((1, tm, d), lambda bb, i: (bb, i, 0)),
        out_shape=jax.ShapeDtypeStruct((b, seq, d), f32),
        compiler_params=pltpu.CompilerParams(
            dimension_semantics=("arbitrary", "arbitrary"), vmem_limit_bytes=VMEM_LIMIT),
        name="ffn_final",
    )(h1, g1, wg, wu, wd, g2)


def kernel(x, meta_tokens, norm_mix_g, w_in, w_attn_out, conv_w, w_conv_out, w_out,
           norm_ffn_g, w_gate, w_up, w_down, norm_final_g):
    b, seq, d = x.shape
    assert w_in.shape[0] == 1, "single-layer block"
    assert seq % ROW_TILE == 0 and meta_tokens.shape[0] == N_META
    c = conv_w.shape[2]
    lp = ROW_TILE + seq
    l_real = N_META + seq
    topk = min(TOPK_MAX, l_real // 4)
    nt = lp // ROW_TILE

    meta = jnp.broadcast_to(meta_tokens[None].astype(x.dtype), (b, N_META, d))
    hp = jnp.concatenate([jnp.zeros((b, FRONT_PAD, d), x.dtype), meta, x], axis=1)
    h2d = hp.reshape(b * lp, d)

    w = w_in[0]
    o_idx = 3 * ATTN_WIDTH
    n_idx = N_IDX_HEADS * IDX_DIM + IDX_DIM + N_IDX_HEADS
    o_conv = o_idx + n_idx
    o_gate = o_conv + 3 * c
    n_idx_pad = -(-n_idx // 128) * 128
    w_qkv = w[:, :o_idx].astype(bf16)
    w_idx = jnp.pad(w[:, o_idx:o_conv], ((0, 0), (0, n_idx_pad - n_idx))).astype(bf16)
    w_conv = w[:, o_conv:o_gate].astype(bf16)
    w_gates = w[:, o_gate:].astype(bf16)
    g_mix = norm_mix_g[0][None]

    tm = 512 if (b * lp) % 512 == 0 else ROW_TILE
    qkv_scale = jnp.concatenate([jnp.full((1, ATTN_WIDTH), LOG2E * HEAD_DIM ** -0.5, f32),
                                 jnp.ones((1, 2 * ATTN_WIDTH), f32)], axis=1)
    qkv = _rms_proj(h2d, g_mix, w_qkv, qkv_scale, bf16, tm).reshape(b, lp, 3 * ATTN_WIDTH)
    idx = _rms_proj(h2d, g_mix, w_idx, jnp.ones((1, n_idx_pad), f32), f32, tm)
    conv3 = _rms_proj(h2d, g_mix, w_conv, jnp.ones((1, 3 * c), f32), f32, tm).reshape(b, lp, 3 * c)
    gates3 = _rms_proj(h2d, g_mix, w_gates, jnp.ones((1, 2 * d), f32), f32, tm).reshape(b, lp, 2 * d)

    nqi = N_IDX_HEADS * IDX_DIM
    qi4 = idx[:, :nqi].astype(bf16).reshape(b, lp, N_IDX_HEADS, IDX_DIM).transpose(0, 2, 1, 3)
    kit = idx[:, nqi:nqi + IDX_DIM].astype(bf16).reshape(b, nt, ROW_TILE, IDX_DIM).transpose(0, 1, 3, 2)
    wi = idx[:, nqi + IDX_DIM:n_idx].reshape(b, lp, N_IDX_HEADS)

    y_attn = _attention(qi4, wi, kit, qkv, seq=seq, lp=lp, topk=topk)

    h1 = _mix(x, y_attn, conv3, gates3, conv_w[0], w_attn_out[0].astype(bf16),
              w_conv_out[0].astype(bf16), w_out[0].astype(bf16))
    return _ffn(h1, norm_ffn_g[0][None], w_gate[0].astype(bf16), w_up[0].astype(bf16),
                w_down[0].astype(bf16), norm_final_g[None])
```

```python
import functools

import jax
import jax.numpy as jnp
from jax import lax
from jax.experimental import pallas as pl
from jax.experimental.pallas import tpu as pltpu

N_META = 16
N_HEADS = 8
HEAD_DIM = 128
ATTN_WIDTH = N_HEADS * HEAD_DIM
N_IDX_HEADS = 8
IDX_DIM = 64
TOPK_MAX = 256
CONV_K = 3
EPS = 1e-6
IDX_SCALE = (N_IDX_HEADS ** -0.5) * (IDX_DIM ** -0.5)

ROW_TILE = 256
FRONT_PAD = ROW_TILE - N_META
MAX_KV_TILES_PER_STEP = 5
BISECT_CAP = 32
NEG = -1e30
KEEP_ALL = 1e9
LANE_LIST = 12
LIST_ROWS = 16
LOG2E = 1.4426950408889634
VMEM_LIMIT = 56 * 1024 * 1024

f32 = jnp.float32
bf16 = jnp.bfloat16


def _rms(x, g):
    return (x * lax.rsqrt(jnp.mean(x * x, axis=-1, keepdims=True) + EPS)) * g


def _sigmoid(x):
    return 1.0 / (1.0 + jnp.exp(-x))


def _rms_proj_kernel(h_ref, g_ref, w_ref, cs_ref, o_ref):
    a = _rms(h_ref[...], g_ref[...]).astype(bf16)
    y = jnp.dot(a, w_ref[...], preferred_element_type=f32)
    o_ref[...] = (y * cs_ref[...]).astype(o_ref.dtype)


def _rms_proj(h2d, g, w, col_scale, out_dtype, tm):
    rows, d = h2d.shape
    n = w.shape[1]
    return pl.pallas_call(
        _rms_proj_kernel,
        grid=(rows // tm,),
        in_specs=[
            pl.BlockSpec((tm, d), lambda i: (i, 0)),
            pl.BlockSpec((1, d), lambda i: (0, 0)),
            pl.BlockSpec((d, n), lambda i: (0, 0)),
            pl.BlockSpec((1, n), lambda i: (0, 0)),
        ],
        out_specs=pl.BlockSpec((tm, n), lambda i: (i, 0)),
        out_shape=jax.ShapeDtypeStruct((rows, n), out_dtype),
        compiler_params=pltpu.CompilerParams(
            dimension_semantics=("arbitrary",), vmem_limit_bytes=VMEM_LIMIT),
        name="rms_proj",
    )(h2d, g, w, col_scale)


def _attn_kernel(qi_ref, wi_ref, kit_ref, tri_ref, q_ref, k_ref, v_ref, o_ref,
                 s_scr, cand_scr, cand_t_scr, bias_scr, wb_scr, thr_scr, need_scr, carry_scr,
                 m_scr, l_scr, acc_scr, flag_scr, *, topk, g_tiles):
    tq = ROW_TILE
    kt = ROW_TILE
    rg = 128
    nh = kt // 128
    iq = pl.program_id(1)
    j = pl.program_id(2)
    nkb = pl.num_programs(2)
    i = iq + 1
    n_tiles = i + 1
    kf = float(topk)

    def lanes_x(v):
        return jnp.concatenate([v] * nh, axis=1)

    def to_dense(rep):
        return jnp.concatenate(
            [rep[g * 128:(g + 1) * 128, :].T[0:8, :] for g in range(tq // 128)], axis=1)

    def to_rows(dense):
        return jnp.concatenate(
            [jnp.broadcast_to(dense[0:1, g * 128:(g + 1) * 128], (128, 128)).T
             for g in range(tq // 128)], axis=0)

    def reduce_scores(vec, fn, init, comb, red):
        vec_rows = to_rows(vec)
        outs = []
        for g in range(tq // rg):
            rows = slice(g * rg, (g + 1) * rg)
            bv = vec_rows[rows]

            def body(t, acc, rows=rows, bv=bv):
                for hl in range(nh):
                    acc = comb(acc, fn(s_scr[t, rows, hl * 128:(hl + 1) * 128], bv))
                return acc

            acc = lax.fori_loop(0, n_tiles, body, jnp.full((rg, 128), init, f32))
            outs.append(jnp.broadcast_to(red(acc, axis=1, keepdims=True), (rg, 128)))
        return to_dense(jnp.concatenate(outs, axis=0))

    def reduce_lists(vec, fn, init, comb, red):
        accs = [jnp.full((8, tq), init, f32) for _ in range(4)]
        for c in range(LANE_LIST * 128 // 8):
            accs[c % 4] = comb(accs[c % 4], fn(cand_t_scr[c * 8:(c + 1) * 8, :], vec))
        acc = comb(comb(accs[0], accs[1]), comb(accs[2], accs[3]))
        return jnp.broadcast_to(red(acc, axis=0, keepdims=True), (8, tq))

    def any_row(mask):
        return jnp.max(jnp.where(mask, 1.0, 0.0)) > 0.0

    @pl.when(j == 0)
    def _scores_and_threshold():
        m_scr[...] = jnp.full(m_scr.shape, NEG, f32)
        l_scr[...] = jnp.zeros(l_scr.shape, f32)
        acc_scr[...] = jnp.zeros(acc_scr.shape, f32)
        carry_scr[...] = jnp.zeros(carry_scr.shape, f32)

        w = wi_ref[0] * IDX_SCALE
        for h in range(N_IDX_HEADS):
            wb_scr[h] = jnp.broadcast_to(w[:, h:h + 1], (tq, 128))
        qpos1 = i * tq + lax.broadcasted_iota(jnp.int32, (tq, 1), 0)

        def tile_body(t, carry):
            rmax, rmin = carry
            kt_tile = kit_ref[0, t]
            acc = None
            for h in range(N_IDX_HEADS):
                x = jnp.dot(qi_ref[0, h], kt_tile, preferred_element_type=f32)
                term = jnp.maximum(x, 0.0) * lanes_x(wb_scr[h])
                acc = term if acc is None else acc + term
            kpos = t * kt + lax.broadcasted_iota(jnp.int32, (1, kt), 1)
            vis = (kpos >= FRONT_PAD) & (kpos <= qpos1)
            s_hi = jnp.where(vis, acc, -jnp.inf)
            s_lo = jnp.where(vis, acc, jnp.inf)
            s_scr[t] = s_hi
            for hl in range(nh):
                rmax = jnp.maximum(rmax, s_hi[:, hl * 128:(hl + 1) * 128])
                rmin = jnp.minimum(rmin, s_lo[:, hl * 128:(hl + 1) * 128])
            return rmax, rmin

        rmax, rmin = lax.fori_loop(
            0, n_tiles, tile_body,
            (jnp.full((tq, 128), -jnp.inf, f32), jnp.full((tq, 128), jnp.inf, f32)))
        s_scr[n_tiles] = jnp.full((tq, kt), -jnp.inf, f32)
        rmax = to_dense(jnp.broadcast_to(jnp.max(rmax, axis=1, keepdims=True), (tq, 128)))
        rmin = to_dense(jnp.broadcast_to(jnp.min(rmin, axis=1, keepdims=True), (tq, 128)))

        qpos = i * tq + lax.broadcasted_iota(jnp.int32, (8, tq), 1)
        n_vis = (qpos - (FRONT_PAD - 1)).astype(f32)
        short = n_vis <= kf

        def search(reduce):
            def count_ge(v):
                return reduce(v, lambda blk, bv: jnp.where(blk >= bv, 1.0, 0.0), 0.0, jnp.add, jnp.sum)

            def count_gt(v):
                return reduce(v, lambda blk, bv: jnp.where(blk > bv, 1.0, 0.0), 0.0, jnp.add, jnp.sum)

            zero = jnp.zeros((8, tq), f32)
            cge0 = count_ge(zero)
            cgt0 = count_gt(zero)
            at_zero = jnp.logical_and(jnp.logical_not(short),
                                      jnp.logical_and(cgt0 < kf, cge0 >= kf))
            res0 = jnp.where(jnp.logical_or(short, at_zero), 1.0, 0.0)
            thr0 = jnp.where(short, rmin, zero)
            need0 = jnp.where(jnp.logical_and(at_zero, cge0 > kf), kf - cgt0, KEEP_ALL)
            lo0 = jnp.where(cgt0 >= kf, zero, rmin)
            hi0 = jnp.where(cge0 < kf, zero, rmax)

            def bis_cond(st):
                it, _, _, _, _, active = st
                return jnp.logical_and(it < BISECT_CAP, jnp.max(active) > 0.0)

            def bis_body(st):
                it, lo, hi, thr, resolved, active = st
                piv = lo + (hi - lo) * 0.5
                splits = jnp.logical_and(piv > lo, piv < hi)
                c = count_ge(piv)
                upd = jnp.logical_and(active > 0.0, splits)
                hit = jnp.logical_and(upd, c == kf)
                thr = jnp.where(hit, piv, thr)
                resolved = jnp.where(hit, 1.0, resolved)
                lo = jnp.where(jnp.logical_and(upd, c > kf), piv, lo)
                hi = jnp.where(jnp.logical_and(upd, c < kf), piv, hi)
                active = jnp.where(jnp.logical_and(upd, jnp.logical_not(hit)), 1.0, 0.0)
                return it + 1, lo, hi, thr, resolved, active

            st = (jnp.int32(0), lo0, hi0, thr0, res0, 1.0 - res0)
            _, lo, hi, thr, resolved, _ = lax.while_loop(bis_cond, bis_body, st)

            thr_scr[...] = to_rows(thr)
            need_scr[...] = to_rows(need0)
            flag_scr[0] = jnp.where(any_row(need0 < KEEP_ALL), 1, 0).astype(jnp.int32)
            unresolved = resolved < 0.5

            @pl.when(any_row(unresolved))
            def _fallback():
                v0 = reduce(hi, lambda blk, bv: jnp.where(blk <= bv, blk, -jnp.inf),
                            -jnp.inf, jnp.maximum, jnp.max)

                def fb_cond(st):
                    _, cge, _ = st
                    return any_row(jnp.logical_and(unresolved, cge < kf))

                def fb_body(st):
                    v, cge, cgt = st
                    more = jnp.logical_and(unresolved, cge < kf)
                    v2 = reduce(v, lambda blk, bv: jnp.where(blk < bv, blk, -jnp.inf),
                                -jnp.inf, jnp.maximum, jnp.max)
                    return (jnp.where(more, v2, v), jnp.where(more, count_ge(v2), cge),
                            jnp.where(more, count_gt(v2), cgt))

                v, cge, cgt = lax.while_loop(fb_cond, fb_body, (v0, count_ge(v0), count_gt(v0)))
                need = kf - cgt
                partial = jnp.logical_and(unresolved, (cge - cgt) > need)
                thr_scr[...] = to_rows(jnp.where(unresolved, v, thr))
                need_scr[...] = to_rows(jnp.where(partial, need, need0))

                @pl.when(any_row(partial))
                def _():
                    flag_scr[0] = jnp.int32(1)

        def list_group(g, carry):
            r0 = pl.multiple_of(g * LIST_ROWS, LIST_ROWS)

            def insert_pair(p, lists):
                lists = list(lists)
                for u in range(2):
                    for hl in range(nh):
                        x = s_scr[2 * p + u, pl.ds(r0, LIST_ROWS), hl * 128:(hl + 1) * 128]
                        for k in range(LANE_LIST):
                            top = jnp.maximum(lists[k], x)
                            x = jnp.minimum(lists[k], x)
                            lists[k] = top
                return tuple(lists)

            lists = lax.fori_loop(
                0, (n_tiles + 1) // 2, insert_pair,
                tuple(jnp.full((LIST_ROWS, 128), -jnp.inf, f32) for _ in range(LANE_LIST)))
            for k in range(LANE_LIST):
                cand_scr[k, pl.ds(r0, LIST_ROWS), :] = lists[k]
            return carry

        lax.fori_loop(0, tq // LIST_ROWS, list_group, 0)
        for k in range(LANE_LIST):
            for g in range(tq // 128):
                cand_t_scr[k * 128:(k + 1) * 128, g * 128:(g + 1) * 128] = (
                    cand_scr[k, g * 128:(g + 1) * 128, :].T)

        search(reduce_lists)

        @pl.when(any_row(cand_scr[LANE_LIST - 1] >= thr_scr[...]))
        def _():
            search(reduce_scores)

    n_sub = jnp.minimum(g_tiles, n_tiles - j * g_tiles)

    @pl.when(n_sub > 0)
    def _attend():
        ones_cols = jnp.ones((kt, HEAD_DIM), bf16)

        def sub_body(s, carry):
            t = j * g_tiles + s
            tie_path = flag_scr[0] > 0

            @pl.when(jnp.logical_not(tie_path))
            def _():
                thr = thr_scr[...]
                for hl in range(nh):
                    lanes = slice(hl * 128, (hl + 1) * 128)
                    bias_scr[:, lanes] = jnp.where(s_scr[t, :, lanes] >= thr, 0.0, NEG)

            @pl.when(tie_path)
            def _():
                sc = s_scr[t]
                thr = lanes_x(thr_scr[...])
                tie = sc == thr
                counts = jnp.dot(jnp.where(tie, 1.0, 0.0).astype(bf16), tri_ref[...],
                                 preferred_element_type=f32)
                seen = carry_scr[...]
                keep = (counts[:, :kt] + lanes_x(seen)) <= lanes_x(need_scr[...])
                bias_scr[...] = jnp.where(sc > thr, 0.0,
                                          jnp.where(jnp.logical_and(tie, keep), 0.0, NEG))
                carry_scr[...] = seen + counts[:, kt:]

            row0 = pl.multiple_of(s * kt, kt)
            for h in range(N_HEADS):
                cols = slice(h * HEAD_DIM, (h + 1) * HEAD_DIM)
                qh = q_ref[0, :, cols]
                kh = k_ref[0, pl.ds(row0, kt), cols]
                vh = v_ref[0, pl.ds(row0, kt), cols]
                lg = lax.dot_general(qh, kh, (((1,), (1,)), ((), ())),
                                     preferred_element_type=f32) + bias_scr[...]
                m_old = m_scr[h]
                m_new = jnp.maximum(m_old, jnp.max(lg, axis=1, keepdims=True))
                alpha = jnp.exp2(m_old - m_new)
                p = jnp.exp2(lg - lanes_x(m_new))
                v_ext = jnp.concatenate([vh, ones_cols], axis=1)
                pv = jnp.dot(p.astype(bf16), v_ext, preferred_element_type=f32)
                l_scr[h] = alpha * l_scr[h] + pv[:, HEAD_DIM:]
                acc_scr[:, cols] = alpha * acc_scr[:, cols] + pv[:, :HEAD_DIM]
                m_scr[h] = m_new
            return carry

        lax.fori_loop(0, n_sub, sub_body, 0)

    @pl.when(j == nkb - 1)
    def _finish():
        for h in range(N_HEADS):
            cols = slice(h * HEAD_DIM, (h + 1) * HEAD_DIM)
            o_ref[0, :, cols] = (acc_scr[:, cols] / l_scr[h]).astype(o_ref.dtype)


def _attention(qi4, wi, kit, qkv, *, seq, lp, topk):
    b = qkv.shape[0]
    tq = ROW_TILE
    nt = lp // tq
    g_tiles = max(g for g in range(1, MAX_KV_TILES_PER_STEP + 1) if nt % g == 0)
    nkb = nt // g_tiles
    nq = seq // tq

    def kv_block(iq, j):
        return jnp.minimum(j, (iq + 1) // g_tiles)

    ka = lax.broadcasted_iota(jnp.int32, (tq, tq + 128), 0)
    kb = lax.broadcasted_iota(jnp.int32, (tq, tq + 128), 1)
    tri = jnp.where(jnp.logical_or(ka <= kb, kb >= tq), 1.0, 0.0).astype(bf16)

    kern = functools.partial(_attn_kernel, topk=topk, g_tiles=g_tiles)
    return pl.pallas_call(
        kern,
        grid=(b, nq, nkb),
        in_specs=[
            pl.BlockSpec((1, N_IDX_HEADS, tq, IDX_DIM), lambda bb, iq, j: (bb, 0, iq + 1, 0)),
            pl.BlockSpec((1, tq, N_IDX_HEADS), lambda bb, iq, j: (bb, iq + 1, 0)),
            pl.BlockSpec((1, nt, IDX_DIM, tq), lambda bb, iq, j: (bb, 0, 0, 0)),
            pl.BlockSpec((tq, tq + 128), lambda bb, iq, j: (0, 0)),
            pl.BlockSpec((1, tq, ATTN_WIDTH), lambda bb, iq, j: (bb, iq + 1, 0)),
            pl.BlockSpec((1, g_tiles * tq, ATTN_WIDTH), lambda bb, iq, j: (bb, kv_block(iq, j), 1)),
            pl.BlockSpec((1, g_tiles * tq, ATTN_WIDTH), lambda bb, iq, j: (bb, kv_block(iq, j), 2)),
        ],
        out_specs=pl.BlockSpec((1, tq, ATTN_WIDTH), lambda bb, iq, j: (bb, iq, 0)),
        out_shape=jax.ShapeDtypeStruct((b, seq, ATTN_WIDTH), bf16),
        scratch_shapes=[
            pltpu.VMEM((nt + 1, tq, tq), f32),
            pltpu.VMEM((LANE_LIST, tq, 128), f32),
            pltpu.VMEM((LANE_LIST * 128, tq), f32),
            pltpu.VMEM((tq, tq), f32),
            pltpu.VMEM((N_IDX_HEADS, tq, 128), f32),
            pltpu.VMEM((tq, 128), f32),
            pltpu.VMEM((tq, 128), f32),
            pltpu.VMEM((tq, 128), f32),
            pltpu.VMEM((N_HEADS, tq, 128), f32),
            pltpu.VMEM((N_HEADS, tq, 128), f32),
            pltpu.VMEM((tq, ATTN_WIDTH), f32),
            pltpu.SMEM((1,), jnp.int32),
        ],
        compiler_params=pltpu.CompilerParams(
            dimension_semantics=("arbitrary", "arbitrary", "arbitrary"),
            vmem_limit_bytes=VMEM_LIMIT),
        name="dsa_attention",
    )(qi4, wi, kit, tri, qkv, qkv, qkv)


def _mix_kernel(x_ref, ya_ref, cv_ref, cvp_ref, gt_ref, cw_ref, wao_ref, wco_ref, wout_ref,
                o_ref, u_scr):
    tm = x_ref.shape[1]
    c = cw_ref.shape[1]
    cu = cv_ref[0, :, 0:c]
    cb = cv_ref[0, :, c:2 * c]
    cc = cv_ref[0, :, 2 * c:3 * c]
    u_scr[0:8, :] = cvp_ref[0, :, 2 * c:3 * c] * cvp_ref[0, :, 0:c]
    u_scr[8:8 + tm, :] = cc * cu
    w = cw_ref[...]
    conv = (w[2:3] * u_scr[8:8 + tm, :] + w[1:2] * u_scr[7:7 + tm, :]
            + w[0:1] * u_scr[6:6 + tm, :])
    y_conv = jnp.dot((cb * conv).astype(bf16), wco_ref[...], preferred_element_type=f32)
    y_attn = jnp.dot(ya_ref[0], wao_ref[...], preferred_element_type=f32)
    d = wout_ref.shape[0]
    mixed = _sigmoid(gt_ref[0, :, 0:d]) * y_attn + _sigmoid(gt_ref[0, :, d:2 * d]) * y_conv
    o_ref[0] = x_ref[0] + jnp.dot(mixed.astype(bf16), wout_ref[...], preferred_element_type=f32)


def _mix(x, y_attn, conv3, gates3, conv_w, wao, wco, wout):
    b, seq, d = x.shape
    tm = ROW_TILE
    c = conv_w.shape[1]
    const = lambda bb, i: (0, 0)
    return pl.pallas_call(
        _mix_kernel,
        grid=(b, seq // tm),
        in_specs=[
            pl.BlockSpec((1, tm, d), lambda bb, i: (bb, i, 0)),
            pl.BlockSpec((1, tm, ATTN_WIDTH), lambda bb, i: (bb, i, 0)),
            pl.BlockSpec((1, tm, 3 * c), lambda bb, i: (bb, i + 1, 0)),
            pl.BlockSpec((1, 8, 3 * c), lambda bb, i: (bb, (i + 1) * (tm // 8) - 1, 0)),
            pl.BlockSpec((1, tm, 2 * d), lambda bb, i: (bb, i + 1, 0)),
            pl.BlockSpec((CONV_K, c), const),
            pl.BlockSpec(wao.shape, const),
            pl.BlockSpec(wco.shape, const),
            pl.BlockSpec(wout.shape, const),
        ],
        out_specs=pl.BlockSpec((1, tm, d), lambda bb, i: (bb, i, 0)),
        out_shape=jax.ShapeDtypeStruct((b, seq, d), f32),
        scratch_shapes=[pltpu.VMEM((tm + 8, c), f32)],
        compiler_params=pltpu.CompilerParams(
            dimension_semantics=("arbitrary", "arbitrary"), vmem_limit_bytes=VMEM_LIMIT),
        name="mix_merge",
    )(x, y_attn, conv3, conv3, gates3, conv_w, wao, wco, wout)


def _ffn_kernel(h_ref, g1_ref, wg_ref, wu_ref, wd_ref, g2_ref, o_ref):
    h = h_ref[0]
    f = _rms(h, g1_ref[...]).astype(bf16)
    gate = jnp.dot(f, wg_ref[...], preferred_element_type=f32)
    up = jnp.dot(f, wu_ref[...], preferred_element_type=f32)
    act = (gate * _sigmoid(gate)) * up
    h2 = h + jnp.dot(act.astype(bf16), wd_ref[...], preferred_element_type=f32)
    o_ref[0] = _rms(h2, g2_ref[...])


def _ffn(h1, g1, wg, wu, wd, g2):
    b, seq, d = h1.shape
    tm = ROW_TILE
    const = lambda bb, i: (0, 0)
    return pl.pallas_call(
        _ffn_kernel,
        grid=(b, seq // tm),
        in_specs=[
            pl.BlockSpec((1, tm, d), lambda bb, i: (bb, i, 0)),
            pl.BlockSpec((1, d), const),
            pl.BlockSpec(wg.shape, const),
            pl.BlockSpec(wu.shape, const),
            pl.BlockSpec(wd.shape, const),
            pl.BlockSpec((1, d), const),
        ],
        out_specs=pl.BlockSpec((1, tm, d), lambda bb, i: (bb, i, 0)),
        out_shape=jax.ShapeDtypeStruct((b, seq, d), f32),
        compiler_params=pltpu.CompilerParams(
            dimension_semantics=("arbitrary", "arbitrary"), vmem_limit_bytes=VMEM_LIMIT),
        name="ffn_final",
    )(h1, g1, wg, wu, wd, g2)


def kernel(x, meta_tokens, norm_mix_g, w_in, w_attn_out, conv_w, w_conv_out, w_out,
           norm_ffn_g, w_gate, w_up, w_down, norm_final_g):
    b, seq, d = x.shape
    assert w_in.shape[0] == 1, "single-layer block"
    assert seq % ROW_TILE == 0 and meta_tokens.shape[0] == N_META
    c = conv_w.shape[2]
    lp = ROW_TILE + seq
    l_real = N_META + seq
    topk = min(TOPK_MAX, l_real // 4)
    nt = lp // ROW_TILE

    meta = jnp.broadcast_to(meta_tokens[None].astype(x.dtype), (b, N_META, d))
    hp = jnp.concatenate([jnp.zeros((b, FRONT_PAD, d), x.dtype), meta, x], axis=1)
    h2d = hp.reshape(b * lp, d)

    w = w_in[0]
    o_idx = 3 * ATTN_WIDTH
    n_idx = N_IDX_HEADS * IDX_DIM + IDX_DIM + N_IDX_HEADS
    o_conv = o_idx + n_idx
    o_gate = o_conv + 3 * c
    n_idx_pad = -(-n_idx // 128) * 128
    w_qkv = w[:, :o_idx].astype(bf16)
    w_idx = jnp.pad(w[:, o_idx:o_conv], ((0, 0), (0, n_idx_pad - n_idx))).astype(bf16)
    w_conv = w[:, o_conv:o_gate].astype(bf16)
    w_gates = w[:, o_gate:].astype(bf16)
    g_mix = norm_mix_g[0][None]

    tm = 512 if (b * lp) % 512 == 0 else ROW_TILE
    qkv_scale = jnp.concatenate([jnp.full((1, ATTN_WIDTH), LOG2E * HEAD_DIM ** -0.5, f32),
                                 jnp.ones((1, 2 * ATTN_WIDTH), f32)], axis=1)
    qkv = _rms_proj(h2d, g_mix, w_qkv, qkv_scale, bf16, tm).reshape(b, lp, 3 * ATTN_WIDTH)
    idx = _rms_proj(h2d, g_mix, w_idx, jnp.ones((1, n_idx_pad), f32), f32, tm)
    conv3 = _rms_proj(h2d, g_mix, w_conv, jnp.ones((1, 3 * c), f32), f32, tm).reshape(b, lp, 3 * c)
    gates3 = _rms_proj(h2d, g_mix, w_gates, jnp.ones((1, 2 * d), f32), f32, tm).reshape(b, lp, 2 * d)

    nqi = N_IDX_HEADS * IDX_DIM
    qi4 = idx[:, :nqi].astype(bf16).reshape(b, lp, N_IDX_HEADS, IDX_DIM).transpose(0, 2, 1, 3)
    kit = idx[:, nqi:nqi + IDX_DIM].astype(bf16).reshape(b, nt, ROW_TILE, IDX_DIM).transpose(0, 1, 3, 2)
    wi = idx[:, nqi + IDX_DIM:n_idx].reshape(b, lp, N_IDX_HEADS)

    y_attn = _attention(qi4, wi, kit, qkv, seq=seq, lp=lp, topk=topk)

    h1 = _mix(x, y_attn, conv3, gates3, conv_w[0], w_attn_out[0].astype(bf16),
              w_conv_out[0].astype(bf16), w_out[0].astype(bf16))
    return _ffn(h1, norm_ffn_g[0][None], w_gate[0].astype(bf16), w_up[0].astype(bf16),
                w_down[0].astype(bf16), norm_final_g[None])
```

```python
import functools

import jax
import jax.numpy as jnp
from jax import lax
from jax.experimental import pallas as pl
from jax.experimental.pallas import tpu as pltpu

N_META = 16
N_HEADS = 8
HEAD_DIM = 128
ATTN_WIDTH = N_HEADS * HEAD_DIM
N_IDX_HEADS = 8
IDX_DIM = 64
TOPK_MAX = 256
CONV_K = 3
EPS = 1e-6
IDX_SCALE = (N_IDX_HEADS ** -0.5) * (IDX_DIM ** -0.5)

ROW_TILE = 256
FRONT_PAD = ROW_TILE - N_META
MAX_KV_TILES_PER_STEP = 5
BISECT_CAP = 32
NEG = -1e30
KEEP_ALL = 1e9
LANE_LIST = 12
LIST_ROWS = 16
LOG2E = 1.4426950408889634
VMEM_LIMIT = 56 * 1024 * 1024

f32 = jnp.float32
bf16 = jnp.bfloat16


def _rms(x, g):
    return (x * lax.rsqrt(jnp.mean(x * x, axis=-1, keepdims=True) + EPS)) * g


def _sigmoid(x):
    return 1.0 / (1.0 + jnp.exp(-x))


def _rms_proj_kernel(h_ref, g_ref, w_ref, cs_ref, o_ref):
    a = _rms(h_ref[...], g_ref[...]).astype(bf16)
    y = jnp.dot(a, w_ref[...], preferred_element_type=f32)
    o_ref[...] = (y * cs_ref[...]).astype(o_ref.dtype)


def _rms_proj(h2d, g, w, col_scale, out_dtype, tm):
    rows, d = h2d.shape
    n = w.shape[1]
    return pl.pallas_call(
        _rms_proj_kernel,
        grid=(rows // tm,),
        in_specs=[
            pl.BlockSpec((tm, d), lambda i: (i, 0)),
            pl.BlockSpec((1, d), lambda i: (0, 0)),
            pl.BlockSpec((d, n), lambda i: (0, 0)),
            pl.BlockSpec((1, n), lambda i: (0, 0)),
        ],
        out_specs=pl.BlockSpec((tm, n), lambda i: (i, 0)),
        out_shape=jax.ShapeDtypeStruct((rows, n), out_dtype),
        compiler_params=pltpu.CompilerParams(
            dimension_semantics=("arbitrary",), vmem_limit_bytes=VMEM_LIMIT),
        name="rms_proj",
    )(h2d, g, w, col_scale)


def _attn_kernel(qi_ref, wi_ref, kit_ref, tri_ref, q_ref, k_ref, v_ref, o_ref,
                 s_scr, cand_scr, cand_t_scr, bias_scr, wb_scr, rmax_scr, rmin_scr, thr_scr,
                 need_scr, carry_scr, m_scr, l_scr, acc_scr, flag_scr, *, topk, g_tiles):
    tq = ROW_TILE
    kt = ROW_TILE
    rg = 128
    nh = kt // 128
    iq = pl.program_id(1)
    j = pl.program_id(2)
    nq = pl.num_programs(1) - 1
    nkb = pl.num_programs(2)
    i = iq + 1
    n_tiles = i + 1
    has_scores = iq < nq
    has_attn = iq >= 1
    t0 = j * g_tiles
    n_sc = jnp.where(has_scores, jnp.clip(n_tiles - t0, 0, g_tiles), 0)
    n_at = jnp.where(has_attn, jnp.clip(iq + 1 - t0, 0, g_tiles), 0)
    n_both = jnp.minimum(n_sc, n_at)
    kf = float(topk)

    def lanes_x(v):
        return jnp.concatenate([v] * nh, axis=1)

    def to_dense(rep):
        return jnp.concatenate(
            [rep[g * 128:(g + 1) * 128, :].T[0:8, :] for g in range(tq // 128)], axis=1)

    def to_rows(dense):
        return jnp.concatenate(
            [jnp.broadcast_to(dense[0:1, g * 128:(g + 1) * 128], (128, 128)).T
             for g in range(tq // 128)], axis=0)

    def reduce_scores(vec, fn, init, comb, red):
        vec_rows = to_rows(vec)
        outs = []
        for g in range(tq // rg):
            rows = slice(g * rg, (g + 1) * rg)
            bv = vec_rows[rows]

            def body(t, acc, rows=rows, bv=bv):
                for hl in range(nh):
                    acc = comb(acc, fn(s_scr[t, rows, hl * 128:(hl + 1) * 128], bv))
                return acc

            acc = lax.fori_loop(0, n_tiles, body, jnp.full((rg, 128), init, f32))
            outs.append(jnp.broadcast_to(red(acc, axis=1, keepdims=True), (rg, 128)))
        return to_dense(jnp.concatenate(outs, axis=0))

    def reduce_lists(vec, fn, init, comb, red):
        accs = [jnp.full((8, tq), init, f32) for _ in range(4)]
        for c in range(LANE_LIST * 128 // 8):
            accs[c % 4] = comb(accs[c % 4], fn(cand_t_scr[c * 8:(c + 1) * 8, :], vec))
        acc = comb(comb(accs[0], accs[1]), comb(accs[2], accs[3]))
        return jnp.broadcast_to(red(acc, axis=0, keepdims=True), (8, tq))

    def any_row(mask):
        return jnp.max(jnp.where(mask, 1.0, 0.0)) > 0.0

    @pl.when(jnp.logical_and(j == 0, has_attn))
    def _init_attention():
        m_scr[...] = jnp.full(m_scr.shape, NEG, f32)
        l_scr[...] = jnp.zeros(l_scr.shape, f32)
        acc_scr[...] = jnp.zeros(acc_scr.shape, f32)
        carry_scr[...] = jnp.zeros(carry_scr.shape, f32)

    @pl.when(jnp.logical_and(j == 0, has_scores))
    def _init_scores():
        w = wi_ref[0] * IDX_SCALE
        for h in range(N_IDX_HEADS):
            wb_scr[h] = jnp.broadcast_to(w[:, h:h + 1], (tq, 128))
        rmax_scr[...] = jnp.full((tq, 128), -jnp.inf, f32)
        rmin_scr[...] = jnp.full((tq, 128), jnp.inf, f32)

    def score_tile(s):
        t = t0 + s
        kt_tile = kit_ref[0, t]
        acc = None
        for h in range(N_IDX_HEADS):
            x = jnp.dot(qi_ref[0, h], kt_tile, preferred_element_type=f32)
            term = jnp.maximum(x, 0.0) * lanes_x(wb_scr[h])
            acc = term if acc is None else acc + term
        qpos1 = i * tq + lax.broadcasted_iota(jnp.int32, (tq, 1), 0)
        kpos = t * kt + lax.broadcasted_iota(jnp.int32, (1, kt), 1)
        vis = (kpos >= FRONT_PAD) & (kpos <= qpos1)
        s_hi = jnp.where(vis, acc, -jnp.inf)
        s_lo = jnp.where(vis, acc, jnp.inf)
        s_scr[t] = s_hi
        rmax = rmax_scr[...]
        rmin = rmin_scr[...]
        for hl in range(nh):
            rmax = jnp.maximum(rmax, s_hi[:, hl * 128:(hl + 1) * 128])
            rmin = jnp.minimum(rmin, s_lo[:, hl * 128:(hl + 1) * 128])
        rmax_scr[...] = rmax
        rmin_scr[...] = rmin

    def attend_tile(s):
        t = t0 + s
        tie_path = flag_scr[0] > 0

        @pl.when(jnp.logical_not(tie_path))
        def _():
            thr = thr_scr[...]
            for hl in range(nh):
                lanes = slice(hl * 128, (hl + 1) * 128)
                bias_scr[:, lanes] = jnp.where(s_scr[t, :, lanes] >= thr, 0.0, NEG)

        @pl.when(tie_path)
        def _():
            sc = s_scr[t]
            thr = lanes_x(thr_scr[...])
            tie = sc == thr
            counts = jnp.dot(jnp.where(tie, 1.0, 0.0).astype(bf16), tri_ref[...],
                             preferred_element_type=f32)
            seen = carry_scr[...]
            keep = (counts[:, :kt] + lanes_x(seen)) <= lanes_x(need_scr[...])
            bias_scr[...] = jnp.where(sc > thr, 0.0,
                                      jnp.where(jnp.logical_and(tie, keep), 0.0, NEG))
            carry_scr[...] = seen + counts[:, kt:]

        ones_cols = jnp.ones((kt, HEAD_DIM), bf16)
        row0 = pl.multiple_of(s * kt, kt)
        for h in range(N_HEADS):
            cols = slice(h * HEAD_DIM, (h + 1) * HEAD_DIM)
            qh = q_ref[0, :, cols]
            kh = k_ref[0, pl.ds(row0, kt), cols]
            vh = v_ref[0, pl.ds(row0, kt), cols]
            lg = lax.dot_general(qh, kh, (((1,), (1,)), ((), ())),
                                 preferred_element_type=f32) + bias_scr[...]
            m_old = m_scr[h]
            m_new = jnp.maximum(m_old, jnp.max(lg, axis=1, keepdims=True))
            alpha = jnp.exp2(m_old - m_new)
            p = jnp.exp2(lg - lanes_x(m_new))
            v_ext = jnp.concatenate([vh, ones_cols], axis=1)
            pv = jnp.dot(p.astype(bf16), v_ext, preferred_element_type=f32)
            l_scr[h] = alpha * l_scr[h] + pv[:, HEAD_DIM:]
            acc_scr[:, cols] = alpha * acc_scr[:, cols] + pv[:, :HEAD_DIM]
            m_scr[h] = m_new

    def both_body(s, carry):
        attend_tile(s)
        score_tile(s)
        return carry

    def attend_body(s, carry):
        attend_tile(s)
        return carry

    def score_body(s, carry):
        score_tile(s)
        return carry

    lax.fori_loop(0, n_both, both_body, 0)
    lax.fori_loop(n_both, n_at, attend_body, 0)
    lax.fori_loop(n_both, n_sc, score_body, 0)

    @pl.when(jnp.logical_and(has_scores, j == (n_tiles - 1) // g_tiles))
    def _threshold():
        s_scr[n_tiles] = jnp.full((tq, kt), -jnp.inf, f32)
        rmax = to_dense(jnp.broadcast_to(jnp.max(rmax_scr[...], axis=1, keepdims=True), (tq, 128)))
        rmin = to_dense(jnp.broadcast_to(jnp.min(rmin_scr[...], axis=1, keepdims=True), (tq, 128)))

        qpos = i * tq + lax.broadcasted_iota(jnp.int32, (8, tq), 1)
        n_vis = (qpos - (FRONT_PAD - 1)).astype(f32)
        short = n_vis <= kf

        def search(reduce):
            def count_ge(v):
                return reduce(v, lambda blk, bv: jnp.where(blk >= bv, 1.0, 0.0), 0.0, jnp.add, jnp.sum)

            def count_gt(v):
                return reduce(v, lambda blk, bv: jnp.where(blk > bv, 1.0, 0.0), 0.0, jnp.add, jnp.sum)

            zero = jnp.zeros((8, tq), f32)
            cge0 = count_ge(zero)
            cgt0 = count_gt(zero)
            at_zero = jnp.logical_and(jnp.logical_not(short),
                                      jnp.logical_and(cgt0 < kf, cge0 >= kf))
            res0 = jnp.where(jnp.logical_or(short, at_zero), 1.0, 0.0)
            thr0 = jnp.where(short, rmin, zero)
            need0 = jnp.where(jnp.logical_and(at_zero, cge0 > kf), kf - cgt0, KEEP_ALL)
            lo0 = jnp.where(cgt0 >= kf, zero, rmin)
            hi0 = jnp.where(cge0 < kf, zero, rmax)

            def bis_cond(st):
                it, _, _, _, _, active = st
                return jnp.logical_and(it < BISECT_CAP, jnp.max(active) > 0.0)

            def bis_body(st):
                it, lo, hi, thr, resolved, active = st
                piv = lo + (hi - lo) * 0.5
                splits = jnp.logical_and(piv > lo, piv < hi)
                c = count_ge(piv)
                upd = jnp.logical_and(active > 0.0, splits)
                hit = jnp.logical_and(upd, c == kf)
                thr = jnp.where(hit, piv, thr)
                resolved = jnp.where(hit, 1.0, resolved)
                lo = jnp.where(jnp.logical_and(upd, c > kf), piv, lo)
                hi = jnp.where(jnp.logical_and(upd, c < kf), piv, hi)
                active = jnp.where(jnp.logical_and(upd, jnp.logical_not(hit)), 1.0, 0.0)
                return it + 1, lo, hi, thr, resolved, active

            st = (jnp.int32(0), lo0, hi0, thr0, res0, 1.0 - res0)
            _, lo, hi, thr, resolved, _ = lax.while_loop(bis_cond, bis_body, st)

            thr_scr[...] = to_rows(thr)
            need_scr[...] = to_rows(need0)
            flag_scr[0] = jnp.where(any_row(need0 < KEEP_ALL), 1, 0).astype(jnp.int32)
            unresolved = resolved < 0.5

            @pl.when(any_row(unresolved))
            def _fallback():
                v0 = reduce(hi, lambda blk, bv: jnp.where(blk <= bv, blk, -jnp.inf),
                            -jnp.inf, jnp.maximum, jnp.max)

                def fb_cond(st):
                    _, cge, _ = st
                    return any_row(jnp.logical_and(unresolved, cge < kf))

                def fb_body(st):
                    v, cge, cgt = st
                    more = jnp.logical_and(unresolved, cge < kf)
                    v2 = reduce(v, lambda blk, bv: jnp.where(blk < bv, blk, -jnp.inf),
                                -jnp.inf, jnp.maximum, jnp.max)
                    return (jnp.where(more, v2, v), jnp.where(more, count_ge(v2), cge),
                            jnp.where(more, count_gt(v2), cgt))

                v, cge, cgt = lax.while_loop(fb_cond, fb_body, (v0, count_ge(v0), count_gt(v0)))
                need = kf - cgt
                partial = jnp.logical_and(unresolved, (cge - cgt) > need)
                thr_scr[...] = to_rows(jnp.where(unresolved, v, thr))
                need_scr[...] = to_rows(jnp.where(partial, need, need0))

                @pl.when(any_row(partial))
                def _():
                    flag_scr[0] = jnp.int32(1)

        def list_group(g, carry):
            r0 = pl.multiple_of(g * LIST_ROWS, LIST_ROWS)

            def insert_pair(p, lists):
                lists = list(lists)
                for u in range(2):
                    for hl in range(nh):
                        x = s_scr[2 * p + u, pl.ds(r0, LIST_ROWS), hl * 128:(hl + 1) * 128]
                        for k in range(LANE_LIST):
                            top = jnp.maximum(lists[k], x)
                            x = jnp.minimum(lists[k], x)
                            lists[k] = top
                return tuple(lists)

            lists = lax.fori_loop(
                0, (n_tiles + 1) // 2, insert_pair,
                tuple(jnp.full((LIST_ROWS, 128), -jnp.inf, f32) for _ in range(LANE_LIST)))
            for k in range(LANE_LIST):
                cand_scr[k, pl.ds(r0, LIST_ROWS), :] = lists[k]
            return carry

        lax.fori_loop(0, tq // LIST_ROWS, list_group, 0)
        for k in range(LANE_LIST):
            for g in range(tq // 128):
                cand_t_scr[k * 128:(k + 1) * 128, g * 128:(g + 1) * 128] = (
                    cand_scr[k, g * 128:(g + 1) * 128, :].T)

        search(reduce_lists)

        @pl.when(any_row(cand_scr[LANE_LIST - 1] >= thr_scr[...]))
        def _():
            search(reduce_scores)

    @pl.when(jnp.logical_and(has_attn, j == nkb - 1))
    def _finish():
        for h in range(N_HEADS):
            cols = slice(h * HEAD_DIM, (h + 1) * HEAD_DIM)
            o_ref[0, :, cols] = (acc_scr[:, cols] / l_scr[h]).astype(o_ref.dtype)


def _attention(qi4, wi, kit, qkv, *, seq, lp, topk):
    b = qkv.shape[0]
    tq = ROW_TILE
    nt = lp // tq
    g_tiles = max(g for g in range(1, MAX_KV_TILES_PER_STEP + 1) if nt % g == 0)
    nkb = nt // g_tiles
    nq = seq // tq

    def kv_block(iq, j):
        return jnp.minimum(j, iq // g_tiles)

    def score_tile_idx(iq):
        return jnp.minimum(iq + 1, nq)

    ka = lax.broadcasted_iota(jnp.int32, (tq, tq + 128), 0)
    kb = lax.broadcasted_iota(jnp.int32, (tq, tq + 128), 1)
    tri = jnp.where(jnp.logical_or(ka <= kb, kb >= tq), 1.0, 0.0).astype(bf16)

    kern = functools.partial(_attn_kernel, topk=topk, g_tiles=g_tiles)
    return pl.pallas_call(
        kern,
        grid=(b, nq + 1, nkb),
        in_specs=[
            pl.BlockSpec((1, N_IDX_HEADS, tq, IDX_DIM),
                         lambda bb, iq, j: (bb, 0, score_tile_idx(iq), 0)),
            pl.BlockSpec((1, tq, N_IDX_HEADS), lambda bb, iq, j: (bb, score_tile_idx(iq), 0)),
            pl.BlockSpec((1, nt, IDX_DIM, tq), lambda bb, iq, j: (bb, 0, 0, 0)),
            pl.BlockSpec((tq, tq + 128), lambda bb, iq, j: (0, 0)),
            pl.BlockSpec((1, tq, ATTN_WIDTH), lambda bb, iq, j: (bb, iq, 0)),
            pl.BlockSpec((1, g_tiles * tq, ATTN_WIDTH), lambda bb, iq, j: (bb, kv_block(iq, j), 1)),
            pl.BlockSpec((1, g_tiles * tq, ATTN_WIDTH), lambda bb, iq, j: (bb, kv_block(iq, j), 2)),
        ],
        out_specs=pl.BlockSpec((1, tq, ATTN_WIDTH), lambda bb, iq, j: (bb, jnp.maximum(iq - 1, 0), 0)),
        out_shape=jax.ShapeDtypeStruct((b, seq, ATTN_WIDTH), bf16),
        scratch_shapes=[
            pltpu.VMEM((nt + 1, tq, tq), f32),
            pltpu.VMEM((LANE_LIST, tq, 128), f32),
            pltpu.VMEM((LANE_LIST * 128, tq), f32),
            pltpu.VMEM((tq, tq), f32),
            pltpu.VMEM((N_IDX_HEADS, tq, 128), f32),
            pltpu.VMEM((tq, 128), f32),
            pltpu.VMEM((tq, 128), f32),
            pltpu.VMEM((tq, 128), f32),
            pltpu.VMEM((tq, 128), f32),
            pltpu.VMEM((tq, 128), f32),
            pltpu.VMEM((N_HEADS, tq, 128), f32),
            pltpu.VMEM((N_HEADS, tq, 128), f32),
            pltpu.VMEM((tq, ATTN_WIDTH), f32),
            pltpu.SMEM((1,), jnp.int32),
        ],
        compiler_params=pltpu.CompilerParams(
            dimension_semantics=("arbitrary", "arbitrary", "arbitrary"),
            vmem_limit_bytes=VMEM_LIMIT),
        name="dsa_attention",
    )(qi4, wi, kit, tri, qkv, qkv, qkv)


def _mix_kernel(x_ref, ya_ref, cv_ref, cvp_ref, gt_ref, cw_ref, wao_ref, wco_ref, wout_ref,
                o_ref, u_scr):
    tm = x_ref.shape[1]
    c = cw_ref.shape[1]
    cu = cv_ref[0, :, 0:c]
    cb = cv_ref[0, :, c:2 * c]
    cc = cv_ref[0, :, 2 * c:3 * c]
    u_scr[0:8, :] = cvp_ref[0, :, 2 * c:3 * c] * cvp_ref[0, :, 0:c]
    u_scr[8:8 + tm, :] = cc * cu
    w = cw_ref[...]
    conv = (w[2:3] * u_scr[8:8 + tm, :] + w[1:2] * u_scr[7:7 + tm, :]
            + w[0:1] * u_scr[6:6 + tm, :])
    y_conv = jnp.dot((cb * conv).astype(bf16), wco_ref[...], preferred_element_type=f32)
    y_attn = jnp.dot(ya_ref[0], wao_ref[...], preferred_element_type=f32)
    d = wout_ref.shape[0]
    mixed = _sigmoid(gt_ref[0, :, 0:d]) * y_attn + _sigmoid(gt_ref[0, :, d:2 * d]) * y_conv
    o_ref[0] = x_ref[0] + jnp.dot(mixed.astype(bf16), wout_ref[...], preferred_element_type=f32)


def _mix(x, y_attn, conv3, gates3, conv_w, wao, wco, wout):
    b, seq, d = x.shape
    tm = ROW_TILE
    c = conv_w.shape[1]
    const = lambda bb, i: (0, 0)
    return pl.pallas_call(
        _mix_kernel,
        grid=(b, seq // tm),
        in_specs=[
            pl.BlockSpec((1, tm, d), lambda bb, i: (bb, i, 0)),
            pl.BlockSpec((1, tm, ATTN_WIDTH), lambda bb, i: (bb, i, 0)),
            pl.BlockSpec((1, tm, 3 * c), lambda bb, i: (bb, i + 1, 0)),
            pl.BlockSpec((1, 8, 3 * c), lambda bb, i: (bb, (i + 1) * (tm // 8) - 1, 0)),
            pl.BlockSpec((1, tm, 2 * d), lambda bb, i: (bb, i + 1, 0)),
            pl.BlockSpec((CONV_K, c), const),
            pl.BlockSpec(wao.shape, const),
            pl.BlockSpec(wco.shape, const),
            pl.BlockSpec(wout.shape, const),
        ],
        out_specs=pl.BlockSpec((1, tm, d), lambda bb, i: (bb, i, 0)),
        out_shape=jax.ShapeDtypeStruct((b, seq, d), f32),
        scratch_shapes=[pltpu.VMEM((tm + 8, c), f32)],
        compiler_params=pltpu.CompilerParams(
            dimension_semantics=("arbitrary", "arbitrary"), vmem_limit_bytes=VMEM_LIMIT),
        name="mix_merge",
    )(x, y_attn, conv3, conv3, gates3, conv_w, wao, wco, wout)


def _ffn_kernel(h_ref, g1_ref, wg_ref, wu_ref, wd_ref, g2_ref, o_ref):
    h = h_ref[0]
    f = _rms(h, g1_ref[...]).astype(bf16)
    gate = jnp.dot(f, wg_ref[...], preferred_element_type=f32)
    up = jnp.dot(f, wu_ref[...], preferred_element_type=f32)
    act = (gate * _sigmoid(gate)) * up
    h2 = h + jnp.dot(act.astype(bf16), wd_ref[...], preferred_element_type=f32)
    o_ref[0] = _rms(h2, g2_ref[...])


def _ffn(h1, g1, wg, wu, wd, g2):
    b, seq, d = h1.shape
    tm = ROW_TILE
    const = lambda bb, i: (0, 0)
    return pl.pallas_call(
        _ffn_kernel,
        grid=(b, seq // tm),
        in_specs=[
            pl.BlockSpec((1, tm, d), lambda bb, i: (bb, i, 0)),
            pl.BlockSpec((1, d), const),
            pl.BlockSpec(wg.shape, const),
            pl.BlockSpec(wu.shape, const),
            pl.BlockSpec(wd.shape, const),
            pl.BlockSpec((1, d), const),
        ],
        out_specs=pl.BlockSpec((1, tm, d), lambda bb, i: (bb, i, 0)),
        out_shape=jax.ShapeDtypeStruct((b, seq, d), f32),
        compiler_params=pltpu.CompilerParams(
            dimension_semantics=("arbitrary", "arbitrary"), vmem_limit_bytes=VMEM_LIMIT),
        name="ffn_final",
    )(h1, g1, wg, wu, wd, g2)


def kernel(x, meta_tokens, norm_mix_g, w_in, w_attn_out, conv_w, w_conv_out, w_out,
           norm_ffn_g, w_gate, w_up, w_down, norm_final_g):
    b, seq, d = x.shape
    assert w_in.shape[0] == 1, "single-layer block"
    assert seq % ROW_TILE == 0 and meta_tokens.shape[0] == N_META
    c = conv_w.shape[2]
    lp = ROW_TILE + seq
    l_real = N_META + seq
    topk = min(TOPK_MAX, l_real // 4)
    nt = lp // ROW_TILE

    meta = jnp.broadcast_to(meta_tokens[None].astype(x.dtype), (b, N_META, d))
    hp = jnp.concatenate([jnp.zeros((b, FRONT_PAD, d), x.dtype), meta, x], axis=1)
    h2d = hp.reshape(b * lp, d)

    w = w_in[0]
    o_idx = 3 * ATTN_WIDTH
    n_idx = N_IDX_HEADS * IDX_DIM + IDX_DIM + N_IDX_HEADS
    o_conv = o_idx + n_idx
    o_gate = o_conv + 3 * c
    n_idx_pad = -(-n_idx // 128) * 128
    w_qkv = w[:, :o_idx].astype(bf16)
    w_idx = jnp.pad(w[:, o_idx:o_conv], ((0, 0), (0, n_idx_pad - n_idx))).astype(bf16)
    w_conv = w[:, o_conv:o_gate].astype(bf16)
    w_gates = w[:, o_gate:].astype(bf16)
    g_mix = norm_mix_g[0][None]

    tm = 512 if (b * lp) % 512 == 0 else ROW_TILE
    qkv_scale = jnp.concatenate([jnp.full((1, ATTN_WIDTH), LOG2E * HEAD_DIM ** -0.5, f32),
                                 jnp.ones((1, 2 * ATTN_WIDTH), f32)], axis=1)
    qkv = _rms_proj(h2d, g_mix, w_qkv, qkv_scale, bf16, tm).reshape(b, lp, 3 * ATTN_WIDTH)
    idx = _rms_proj(h2d, g_mix, w_idx, jnp.ones((1, n_idx_pad), f32), f32, tm)
    conv3 = _rms_proj(h2d, g_mix, w_conv, jnp.ones((1, 3 * c), f32), f32, tm).reshape(b, lp, 3 * c)
    gates3 = _rms_proj(h2d, g_mix, w_gates, jnp.ones((1, 2 * d), f32), f32, tm).reshape(b, lp, 2 * d)

    nqi = N_IDX_HEADS * IDX_DIM
    qi4 = idx[:, :nqi].astype(bf16).reshape(b, lp, N_IDX_HEADS, IDX_DIM).transpose(0, 2, 1, 3)
    kit = idx[:, nqi:nqi + IDX_DIM].astype(bf16).reshape(b, nt, ROW_TILE, IDX_DIM).transpose(0, 1, 3, 2)
    wi = idx[:, nqi + IDX_DIM:n_idx].reshape(b, lp, N_IDX_HEADS)

    y_attn = _attention(qi4, wi, kit, qkv, seq=seq, lp=lp, topk=topk)

    h1 = _mix(x, y_attn, conv3, gates3, conv_w[0], w_attn_out[0].astype(bf16),
              w_conv_out[0].astype(bf16), w_out[0].astype(bf16))
    return _ffn(h1, norm_ffn_g[0][None], w_gate[0].astype(bf16), w_up[0].astype(bf16),
                w_down[0].astype(bf16), norm_final_g[None])
```

```python
import functools

import jax
import jax.numpy as jnp
from jax import lax
from jax.experimental import pallas as pl
from jax.experimental.pallas import tpu as pltpu

N_META = 16
N_HEADS = 8
HEAD_DIM = 128
ATTN_WIDTH = N_HEADS * HEAD_DIM
N_IDX_HEADS = 8
IDX_DIM = 64
TOPK_MAX = 256
CONV_K = 3
EPS = 1e-6
IDX_SCALE = (N_IDX_HEADS ** -0.5) * (IDX_DIM ** -0.5)

ROW_TILE = 256
FRONT_PAD = ROW_TILE - N_META
MAX_KV_TILES_PER_STEP = 5
BISECT_CAP = 32
NEG = -1e30
KEEP_ALL = 1e9
BELOW_ALL = -3e38
HALO_ROWS = 16
LANE_LIST = 12
LIST_ROWS = 16
LOG2E = 1.4426950408889634
VMEM_LIMIT = 56 * 1024 * 1024

f32 = jnp.float32
bf16 = jnp.bfloat16


def _rms(x, g):
    return (x * lax.rsqrt(jnp.mean(x * x, axis=-1, keepdims=True) + EPS)) * g


def _sigmoid(x):
    return 1.0 / (1.0 + jnp.exp(-x))


def _rms_proj_kernel(h_ref, g_ref, w_ref, cs_ref, o_ref):
    a = _rms(h_ref[...], g_ref[...]).astype(bf16)
    y = jnp.dot(a, w_ref[...], preferred_element_type=f32)
    o_ref[...] = (y * cs_ref[...]).astype(o_ref.dtype)


def _rms_proj(h2d, g, w, col_scale, out_dtype, tm):
    rows, d = h2d.shape
    n = w.shape[1]
    return pl.pallas_call(
        _rms_proj_kernel,
        grid=(rows // tm,),
        in_specs=[
            pl.BlockSpec((tm, d), lambda i: (i, 0)),
            pl.BlockSpec((1, d), lambda i: (0, 0)),
            pl.BlockSpec((d, n), lambda i: (0, 0)),
            pl.BlockSpec((1, n), lambda i: (0, 0)),
        ],
        out_specs=pl.BlockSpec((tm, n), lambda i: (i, 0)),
        out_shape=jax.ShapeDtypeStruct((rows, n), out_dtype),
        compiler_params=pltpu.CompilerParams(
            dimension_semantics=("arbitrary",), vmem_limit_bytes=VMEM_LIMIT),
        name="rms_proj",
    )(h2d, g, w, col_scale)


def _attn_kernel(qi_ref, wi_ref, kit_ref, tri_ref, q_ref, k_ref, v_ref, o_ref,
                 s_scr, cand_scr, cand_t_scr, bias_scr, wb_scr, thr_scr,
                 need_scr, carry_scr, m_scr, l_scr, acc_scr, flag_scr, *, topk, g_tiles):
    tq = ROW_TILE
    kt = ROW_TILE
    rg = 128
    nh = kt // 128
    iq = pl.program_id(1)
    j = pl.program_id(2)
    nq = pl.num_programs(1) - 1
    nkb = pl.num_programs(2)
    i = iq + 1
    n_tiles = i + 1
    has_scores = iq < nq
    has_attn = iq >= 1
    t0 = j * g_tiles
    n_sc = jnp.where(has_scores, jnp.clip(n_tiles - t0, 0, g_tiles), 0)
    n_at = jnp.where(has_attn, jnp.clip(iq + 1 - t0, 0, g_tiles), 0)
    n_both = jnp.minimum(n_sc, n_at)
    kf = float(topk)

    def lanes_x(v):
        return jnp.concatenate([v] * nh, axis=1)

    def to_dense(rep):
        return jnp.concatenate(
            [rep[g * 128:(g + 1) * 128, :].T[0:8, :] for g in range(tq // 128)], axis=1)

    def to_rows(dense):
        return jnp.concatenate(
            [jnp.broadcast_to(dense[0:1, g * 128:(g + 1) * 128], (128, 128)).T
             for g in range(tq // 128)], axis=0)

    def reduce_scores(vec, fn, init, comb, red):
        vec_rows = to_rows(vec)
        outs = []
        for g in range(tq // rg):
            rows = slice(g * rg, (g + 1) * rg)
            bv = vec_rows[rows]

            def body(t, acc, rows=rows, bv=bv):
                for hl in range(nh):
                    acc = comb(acc, fn(s_scr[t, rows, hl * 128:(hl + 1) * 128], bv))
                return acc

            acc = lax.fori_loop(0, n_tiles, body, jnp.full((rg, 128), init, f32))
            outs.append(jnp.broadcast_to(red(acc, axis=1, keepdims=True), (rg, 128)))
        return to_dense(jnp.concatenate(outs, axis=0))

    def reduce_lists(vec, fn, init, comb, red):
        accs = [jnp.full((8, tq), init, f32) for _ in range(4)]
        for c in range(LANE_LIST * 128 // 8):
            accs[c % 4] = comb(accs[c % 4], fn(cand_t_scr[c * 8:(c + 1) * 8, :], vec))
        acc = comb(comb(accs[0], accs[1]), comb(accs[2], accs[3]))
        return jnp.broadcast_to(red(acc, axis=0, keepdims=True), (8, tq))

    def any_row(mask):
        return jnp.max(jnp.where(mask, 1.0, 0.0)) > 0.0

    @pl.when(jnp.logical_and(j == 0, has_attn))
    def _init_attention():
        m_scr[...] = jnp.full(m_scr.shape, NEG, f32)
        l_scr[...] = jnp.zeros(l_scr.shape, f32)
        acc_scr[...] = jnp.zeros(acc_scr.shape, f32)
        carry_scr[...] = jnp.zeros(carry_scr.shape, f32)

    @pl.when(jnp.logical_and(j == 0, has_scores))
    def _init_scores():
        w = wi_ref[0] * IDX_SCALE
        for h in range(N_IDX_HEADS):
            wb_scr[h] = jnp.broadcast_to(w[:, h:h + 1], (tq, 128))

    def score_tile(s, causal):
        t = t0 + s
        kt_tile = kit_ref[0, t]
        acc = None
        for h in range(N_IDX_HEADS):
            x = jnp.dot(qi_ref[0, h], kt_tile, preferred_element_type=f32)
            term = jnp.maximum(x, 0.0) * lanes_x(wb_scr[h])
            acc = term if acc is None else acc + term
        kpos = t * kt + lax.broadcasted_iota(jnp.int32, (1, kt), 1)
        acc = acc + jnp.where(kpos >= FRONT_PAD, 0.0, -jnp.inf)
        if causal:
            qpos1 = i * tq + lax.broadcasted_iota(jnp.int32, (tq, 1), 0)
            acc = jnp.where(kpos <= qpos1, acc, -jnp.inf)
        s_scr[t] = acc

    def attend_tile(s):
        t = t0 + s
        tie_path = flag_scr[0] > 0

        @pl.when(jnp.logical_not(tie_path))
        def _():
            thr = thr_scr[...]
            for hl in range(nh):
                lanes = slice(hl * 128, (hl + 1) * 128)
                bias_scr[:, lanes] = jnp.where(s_scr[t, :, lanes] >= thr, 0.0, NEG)

        @pl.when(tie_path)
        def _():
            sc = s_scr[t]
            thr = lanes_x(thr_scr[...])
            tie = sc == thr
            counts = jnp.dot(jnp.where(tie, 1.0, 0.0).astype(bf16), tri_ref[...],
                             preferred_element_type=f32)
            seen = carry_scr[...]
            keep = (counts[:, :kt] + lanes_x(seen)) <= lanes_x(need_scr[...])
            bias_scr[...] = jnp.where(sc > thr, 0.0,
                                      jnp.where(jnp.logical_and(tie, keep), 0.0, NEG))
            carry_scr[...] = seen + counts[:, kt:]

        ones_cols = jnp.ones((kt, HEAD_DIM), bf16)
        row0 = pl.multiple_of(s * kt, kt)
        for h in range(N_HEADS):
            cols = slice(h * HEAD_DIM, (h + 1) * HEAD_DIM)
            qh = q_ref[0, :, cols]
            kh = k_ref[0, pl.ds(row0, kt), cols]
            vh = v_ref[0, pl.ds(row0, kt), cols]
            lg = lax.dot_general(qh, kh, (((1,), (1,)), ((), ())),
                                 preferred_element_type=f32) + bias_scr[...]
            m_old = m_scr[h]
            m_new = jnp.maximum(m_old, jnp.max(lg, axis=1, keepdims=True))
            alpha = jnp.exp2(m_old - m_new)
            p = jnp.exp2(lg - lanes_x(m_new))
            v_ext = jnp.concatenate([vh, ones_cols], axis=1)
            pv = jnp.dot(p.astype(bf16), v_ext, preferred_element_type=f32)
            l_scr[h] = alpha * l_scr[h] + pv[:, HEAD_DIM:]
            acc_scr[:, cols] = alpha * acc_scr[:, cols] + pv[:, :HEAD_DIM]
            m_scr[h] = m_new

    def both_body(s, carry):
        attend_tile(s)
        score_tile(s, causal=False)
        return carry

    def attend_body(s, carry):
        attend_tile(s)
        return carry

    def score_body(s, carry):
        score_tile(s, causal=True)
        return carry

    lax.fori_loop(0, n_both, both_body, 0)
    lax.fori_loop(n_both, n_at, attend_body, 0)
    lax.fori_loop(n_both, n_sc, score_body, 0)

    @pl.when(jnp.logical_and(has_scores, j == (n_tiles - 1) // g_tiles))
    def _threshold():
        s_scr[n_tiles] = jnp.full((tq, kt), -jnp.inf, f32)
        qpos = i * tq + lax.broadcasted_iota(jnp.int32, (8, tq), 1)
        n_vis = (qpos - (FRONT_PAD - 1)).astype(f32)
        short = n_vis <= kf

        def search(reduce):
            def count_ge(v):
                return reduce(v, lambda blk, bv: jnp.where(blk >= bv, 1.0, 0.0), 0.0, jnp.add, jnp.sum)

            def count_gt(v):
                return reduce(v, lambda blk, bv: jnp.where(blk > bv, 1.0, 0.0), 0.0, jnp.add, jnp.sum)

            zero = jnp.zeros((8, tq), f32)
            cge0 = count_ge(zero)
            cgt0 = count_gt(zero)
            at_zero = jnp.logical_and(jnp.logical_not(short),
                                      jnp.logical_and(cgt0 < kf, cge0 >= kf))
            res0 = jnp.where(jnp.logical_or(short, at_zero), 1.0, 0.0)
            thr0 = jnp.where(short, BELOW_ALL, zero)
            need0 = jnp.where(jnp.logical_and(at_zero, cge0 > kf), kf - cgt0, KEEP_ALL)
            lo0 = jnp.where(cgt0 >= kf, jnp.maximum(lo_lists, zero), lo_lists)
            hi0 = jnp.where(cge0 < kf, jnp.minimum(hi_lists, zero), hi_lists)

            def bis_cond(st):
                it, _, _, _, _, active = st
                return jnp.logical_and(it < BISECT_CAP, jnp.max(active) > 0.0)

            def bis_body(st):
                it, lo, hi, thr, resolved, active = st
                piv = lo + (hi - lo) * 0.5
                splits = jnp.logical_and(piv > lo, piv < hi)
                c = count_ge(piv)
                upd = jnp.logical_and(active > 0.0, splits)
                hit = jnp.logical_and(upd, c == kf)
                thr = jnp.where(hit, piv, thr)
                resolved = jnp.where(hit, 1.0, resolved)
                lo = jnp.where(jnp.logical_and(upd, c > kf), piv, lo)
                hi = jnp.where(jnp.logical_and(upd, c < kf), piv, hi)
                active = jnp.where(jnp.logical_and(upd, jnp.logical_not(hit)), 1.0, 0.0)
                return it + 1, lo, hi, thr, resolved, active

            st = (jnp.int32(0), lo0, hi0, thr0, res0, 1.0 - res0)
            _, lo, hi, thr, resolved, _ = lax.while_loop(bis_cond, bis_body, st)

            thr_scr[...] = to_rows(thr)
            need_scr[...] = to_rows(need0)
            flag_scr[0] = jnp.where(any_row(need0 < KEEP_ALL), 1, 0).astype(jnp.int32)
            unresolved = resolved < 0.5

            @pl.when(any_row(unresolved))
            def _fallback():
                v0 = reduce(hi, lambda blk, bv: jnp.where(blk <= bv, blk, -jnp.inf),
                            -jnp.inf, jnp.maximum, jnp.max)

                def fb_cond(st):
                    _, cge, _ = st
                    return any_row(jnp.logical_and(unresolved, cge < kf))

                def fb_body(st):
                    v, cge, cgt = st
                    more = jnp.logical_and(unresolved, cge < kf)
                    v2 = reduce(v, lambda blk, bv: jnp.where(blk < bv, blk, -jnp.inf),
                                -jnp.inf, jnp.maximum, jnp.max)
                    return (jnp.where(more, v2, v), jnp.where(more, count_ge(v2), cge),
                            jnp.where(more, count_gt(v2), cgt))

                v, cge, cgt = lax.while_loop(fb_cond, fb_body, (v0, count_ge(v0), count_gt(v0)))
                need = kf - cgt
                partial = jnp.logical_and(unresolved, (cge - cgt) > need)
                thr_scr[...] = to_rows(jnp.where(unresolved, v, thr))
                need_scr[...] = to_rows(jnp.where(partial, need, need0))

                @pl.when(any_row(partial))
                def _():
                    flag_scr[0] = jnp.int32(1)

        def list_group(g, carry):
            r0 = pl.multiple_of(g * LIST_ROWS, LIST_ROWS)

            def insert_pair(p, lists):
                lists = list(lists)
                for u in range(2):
                    for hl in range(nh):
                        x = s_scr[2 * p + u, pl.ds(r0, LIST_ROWS), hl * 128:(hl + 1) * 128]
                        for k in range(LANE_LIST):
                            top = jnp.maximum(lists[k], x)
                            x = jnp.minimum(lists[k], x)
                            lists[k] = top
                return tuple(lists)

            lists = lax.fori_loop(
                0, (n_tiles + 1) // 2, insert_pair,
                tuple(jnp.full((LIST_ROWS, 128), -jnp.inf, f32) for _ in range(LANE_LIST)))
            for k in range(LANE_LIST):
                cand_scr[k, pl.ds(r0, LIST_ROWS), :] = lists[k]
            return carry

        lax.fori_loop(0, tq // LIST_ROWS, list_group, 0)
        for k in range(LANE_LIST):
            for g in range(tq // 128):
                cand_t_scr[k * 128:(k + 1) * 128, g * 128:(g + 1) * 128] = (
                    cand_scr[k, g * 128:(g + 1) * 128, :].T)

        second = cand_t_scr[128:256, :]
        lo_lists = jnp.broadcast_to(jnp.min(second, axis=0, keepdims=True), (8, tq))
        hi_lists = jnp.broadcast_to(jnp.max(second, axis=0, keepdims=True), (8, tq))

        search(reduce_lists)

        @pl.when(any_row(cand_scr[LANE_LIST - 1] >= thr_scr[...]))
        def _():
            search(reduce_scores)

    @pl.when(jnp.logical_and(has_attn, j == nkb - 1))
    def _finish():
        for h in range(N_HEADS):
            cols = slice(h * HEAD_DIM, (h + 1) * HEAD_DIM)
            o_ref[0, :, cols] = (acc_scr[:, cols] / l_scr[h]).astype(o_ref.dtype)


def _attention(qi4, wi, kit, qkv, *, seq, lp, topk):
    b = qkv.shape[0]
    tq = ROW_TILE
    nt = lp // tq
    g_tiles = max(g for g in range(1, MAX_KV_TILES_PER_STEP + 1) if nt % g == 0)
    nkb = nt // g_tiles
    nq = seq // tq

    def kv_block(iq, j):
        return jnp.minimum(j, iq // g_tiles)

    def score_tile_idx(iq):
        return jnp.minimum(iq + 1, nq)

    ka = lax.broadcasted_iota(jnp.int32, (tq, tq + 128), 0)
    kb = lax.broadcasted_iota(jnp.int32, (tq, tq + 128), 1)
    tri = jnp.where(jnp.logical_or(ka <= kb, kb >= tq), 1.0, 0.0).astype(bf16)

    kern = functools.partial(_attn_kernel, topk=topk, g_tiles=g_tiles)
    return pl.pallas_call(
        kern,
        grid=(b, nq + 1, nkb),
        in_specs=[
            pl.BlockSpec((1, N_IDX_HEADS, tq, IDX_DIM),
                         lambda bb, iq, j: (bb, 0, score_tile_idx(iq), 0)),
            pl.BlockSpec((1, tq, N_IDX_HEADS), lambda bb, iq, j: (bb, score_tile_idx(iq), 0)),
            pl.BlockSpec((1, nt, IDX_DIM, tq), lambda bb, iq, j: (bb, 0, 0, 0)),
            pl.BlockSpec((tq, tq + 128), lambda bb, iq, j: (0, 0)),
            pl.BlockSpec((1, tq, ATTN_WIDTH), lambda bb, iq, j: (bb, iq, 0)),
            pl.BlockSpec((1, g_tiles * tq, ATTN_WIDTH), lambda bb, iq, j: (bb, kv_block(iq, j), 1)),
            pl.BlockSpec((1, g_tiles * tq, ATTN_WIDTH), lambda bb, iq, j: (bb, kv_block(iq, j), 2)),
        ],
        out_specs=pl.BlockSpec((1, tq, ATTN_WIDTH), lambda bb, iq, j: (bb, jnp.maximum(iq - 1, 0), 0)),
        out_shape=jax.ShapeDtypeStruct((b, seq, ATTN_WIDTH), bf16),
        scratch_shapes=[
            pltpu.VMEM((nt + 1, tq, tq), f32),
            pltpu.VMEM((LANE_LIST, tq, 128), f32),
            pltpu.VMEM((LANE_LIST * 128, tq), f32),
            pltpu.VMEM((tq, tq), f32),
            pltpu.VMEM((N_IDX_HEADS, tq, 128), f32),
            pltpu.VMEM((tq, 128), f32),
            pltpu.VMEM((tq, 128), f32),
            pltpu.VMEM((tq, 128), f32),
            pltpu.VMEM((N_HEADS, tq, 128), f32),
            pltpu.VMEM((N_HEADS, tq, 128), f32),
            pltpu.VMEM((tq, ATTN_WIDTH), f32),
            pltpu.SMEM((1,), jnp.int32),
        ],
        compiler_params=pltpu.CompilerParams(
            dimension_semantics=("arbitrary", "arbitrary", "arbitrary"),
            vmem_limit_bytes=VMEM_LIMIT),
        name="dsa_attention",
    )(qi4, wi, kit, tri, qkv, qkv, qkv)


def _mix_kernel(x_ref, ya_ref, cv_ref, cvp_ref, gt_ref, cw_ref, wao_ref, wco_ref, wout_ref,
                o_ref, u_scr):
    tm = x_ref.shape[1]
    c = cw_ref.shape[1]
    hr = cvp_ref.shape[1]
    cu = cv_ref[0, :, 0:c].astype(f32)
    cb = cv_ref[0, :, c:2 * c].astype(f32)
    cc = cv_ref[0, :, 2 * c:3 * c].astype(f32)
    u_scr[0:hr, :] = cvp_ref[0, :, 2 * c:3 * c].astype(f32) * cvp_ref[0, :, 0:c].astype(f32)
    u_scr[hr:hr + tm, :] = cc * cu
    w = cw_ref[...]
    conv = (w[2:3] * u_scr[hr:hr + tm, :] + w[1:2] * u_scr[hr - 1:hr - 1 + tm, :]
            + w[0:1] * u_scr[hr - 2:hr - 2 + tm, :])
    y_conv = jnp.dot((cb * conv).astype(bf16), wco_ref[...], preferred_element_type=f32)
    y_attn = jnp.dot(ya_ref[0], wao_ref[...], preferred_element_type=f32)
    d = wout_ref.shape[0]
    mixed = (_sigmoid(gt_ref[0, :, 0:d].astype(f32)) * y_attn
             + _sigmoid(gt_ref[0, :, d:2 * d].astype(f32)) * y_conv)
    o_ref[0] = x_ref[0] + jnp.dot(mixed.astype(bf16), wout_ref[...], preferred_element_type=f32)


def _mix(x, y_attn, conv3, gates3, conv_w, wao, wco, wout):
    b, seq, d = x.shape
    tm = ROW_TILE
    c = conv_w.shape[1]
    const = lambda bb, i: (0, 0)
    return pl.pallas_call(
        _mix_kernel,
        grid=(b, seq // tm),
        in_specs=[
            pl.BlockSpec((1, tm, d), lambda bb, i: (bb, i, 0)),
            pl.BlockSpec((1, tm, ATTN_WIDTH), lambda bb, i: (bb, i, 0)),
            pl.BlockSpec((1, tm, 3 * c), lambda bb, i: (bb, i + 1, 0)),
            pl.BlockSpec((1, HALO_ROWS, 3 * c),
                         lambda bb, i: (bb, (i + 1) * (tm // HALO_ROWS) - 1, 0)),
            pl.BlockSpec((1, tm, 2 * d), lambda bb, i: (bb, i + 1, 0)),
            pl.BlockSpec((CONV_K, c), const),
            pl.BlockSpec(wao.shape, const),
            pl.BlockSpec(wco.shape, const),
            pl.BlockSpec(wout.shape, const),
        ],
        out_specs=pl.BlockSpec((1, tm, d), lambda bb, i: (bb, i, 0)),
        out_shape=jax.ShapeDtypeStruct((b, seq, d), f32),
        scratch_shapes=[pltpu.VMEM((tm + HALO_ROWS, c), f32)],
        compiler_params=pltpu.CompilerParams(
            dimension_semantics=("arbitrary", "arbitrary"), vmem_limit_bytes=VMEM_LIMIT),
        name="mix_merge",
    )(x, y_attn, conv3, conv3, gates3, conv_w, wao, wco, wout)


def _ffn_kernel(h_ref, g1_ref, wg_ref, wu_ref, wd_ref, g2_ref, o_ref):
    h = h_ref[0]
    f = _rms(h, g1_ref[...]).astype(bf16)
    gate = jnp.dot(f, wg_ref[...], preferred_element_type=f32)
    up = jnp.dot(f, wu_ref[...], preferred_element_type=f32)
    act = (gate * _sigmoid(gate)) * up
    h2 = h + jnp.dot(act.astype(bf16), wd_ref[...], preferred_element_type=f32)
    o_ref[0] = _rms(h2, g2_ref[...])


def _ffn(h1, g1, wg, wu, wd, g2):
    b, seq, d = h1.shape
    tm = ROW_TILE
    const = lambda bb, i: (0, 0)
    return pl.pallas_call(
        _ffn_kernel,
        grid=(b, seq // tm),
        in_specs=[
            pl.BlockSpec((1, tm, d), lambda bb, i: (bb, i, 0)),
            pl.BlockSpec((1, d), const),
            pl.BlockSpec(wg.shape, const),
            pl.BlockSpec(wu.shape, const),
            pl.BlockSpec(wd.shape, const),
            pl.BlockSpec((1, d), const),
        ],
        out_specs=pl.BlockSpec((1, tm, d), lambda bb, i: (bb, i, 0)),
        out_shape=jax.ShapeDtypeStruct((b, seq, d), f32),
        compiler_params=pltpu.CompilerParams(
            dimension_semantics=("arbitrary", "arbitrary"), vmem_limit_bytes=VMEM_LIMIT),
        name="ffn_final",
    )(h1, g1, wg, wu, wd, g2)


def kernel(x, meta_tokens, norm_mix_g, w_in, w_attn_out, conv_w, w_conv_out, w_out,
           norm_ffn_g, w_gate, w_up, w_down, norm_final_g):
    b, seq, d = x.shape
    assert w_in.shape[0] == 1, "single-layer block"
    assert seq % ROW_TILE == 0 and meta_tokens.shape[0] == N_META
    c = conv_w.shape[2]
    lp = ROW_TILE + seq
    l_real = N_META + seq
    topk = min(TOPK_MAX, l_real // 4)
    nt = lp // ROW_TILE

    meta = jnp.broadcast_to(meta_tokens[None].astype(x.dtype), (b, N_META, d))
    hp = jnp.concatenate([jnp.zeros((b, FRONT_PAD, d), x.dtype), meta, x], axis=1)
    h2d = hp.reshape(b * lp, d)

    w = w_in[0]
    o_idx = 3 * ATTN_WIDTH
    n_idx = N_IDX_HEADS * IDX_DIM + IDX_DIM + N_IDX_HEADS
    o_conv = o_idx + n_idx
    o_gate = o_conv + 3 * c
    n_idx_pad = -(-n_idx // 128) * 128
    w_qkv = w[:, :o_idx].astype(bf16)
    w_idx = jnp.pad(w[:, o_idx:o_conv], ((0, 0), (0, n_idx_pad - n_idx))).astype(bf16)
    w_conv = w[:, o_conv:o_gate].astype(bf16)
    w_gates = w[:, o_gate:].astype(bf16)
    g_mix = norm_mix_g[0][None]

    tm = 512 if (b * lp) % 512 == 0 else ROW_TILE
    qkv_scale = jnp.concatenate([jnp.full((1, ATTN_WIDTH), LOG2E * HEAD_DIM ** -0.5, f32),
                                 jnp.ones((1, 2 * ATTN_WIDTH), f32)], axis=1)
    qkv = _rms_proj(h2d, g_mix, w_qkv, qkv_scale, bf16, tm).reshape(b, lp, 3 * ATTN_WIDTH)
    idx = _rms_proj(h2d, g_mix, w_idx, jnp.ones((1, n_idx_pad), f32), f32, tm)
    conv3 = _rms_proj(h2d, g_mix, w_conv, jnp.ones((1, 3 * c), f32), bf16, tm).reshape(b, lp, 3 * c)
    gates3 = _rms_proj(h2d, g_mix, w_gates, jnp.ones((1, 2 * d), f32), bf16, tm).reshape(b, lp, 2 * d)

    nqi = N_IDX_HEADS * IDX_DIM
    qi4 = idx[:, :nqi].astype(bf16).reshape(b, lp, N_IDX_HEADS, IDX_DIM).transpose(0, 2, 1, 3)
    kit = idx[:, nqi:nqi + IDX_DIM].astype(bf16).reshape(b, nt, ROW_TILE, IDX_DIM).transpose(0, 1, 3, 2)
    wi = idx[:, nqi + IDX_DIM:n_idx].reshape(b, lp, N_IDX_HEADS)

    y_attn = _attention(qi4, wi, kit, qkv, seq=seq, lp=lp, topk=topk)

    h1 = _mix(x, y_attn, conv3, gates3, conv_w[0], w_attn_out[0].astype(bf16),
              w_conv_out[0].astype(bf16), w_out[0].astype(bf16))
    return _ffn(h1, norm_ffn_g[0][None], w_gate[0].astype(bf16), w_up[0].astype(bf16),
                w_down[0].astype(bf16), norm_final_g[None])
```

```python
import functools

import jax
import jax.numpy as jnp
from jax import lax
from jax.experimental import pallas as pl
from jax.experimental.pallas import tpu as pltpu

N_META = 16
N_HEADS = 8
HEAD_DIM = 128
ATTN_WIDTH = N_HEADS * HEAD_DIM
N_IDX_HEADS = 8
IDX_DIM = 64
TOPK_MAX = 256
CONV_K = 3
EPS = 1e-6
IDX_SCALE = (N_IDX_HEADS ** -0.5) * (IDX_DIM ** -0.5)

ROW_TILE = 256
FRONT_PAD = ROW_TILE - N_META
MAX_KV_TILES_PER_STEP = 5
BISECT_CAP = 32
NEG = -1e30
KEEP_ALL = 1e9
BELOW_ALL = -3e38
HALO_ROWS = 16
LANE_LIST = 12
LIST_ROWS = 16
LOG2E = 1.4426950408889634
VMEM_LIMIT = 56 * 1024 * 1024

f32 = jnp.float32
bf16 = jnp.bfloat16


def _rms(x, g):
    return (x * lax.rsqrt(jnp.mean(x * x, axis=-1, keepdims=True) + EPS)) * g


def _sigmoid(x):
    return 1.0 / (1.0 + jnp.exp(-x))


def _rms_proj_kernel(h_ref, g_ref, w_ref, cs_ref, o_ref):
    a = _rms(h_ref[...], g_ref[...]).astype(bf16)
    y = jnp.dot(a, w_ref[...], preferred_element_type=f32)
    o_ref[...] = (y * cs_ref[...]).astype(o_ref.dtype)


def _rms_proj(h2d, g, w, col_scale, out_dtype, tm):
    rows, d = h2d.shape
    n = w.shape[1]
    return pl.pallas_call(
        _rms_proj_kernel,
        grid=(rows // tm,),
        in_specs=[
            pl.BlockSpec((tm, d), lambda i: (i, 0)),
            pl.BlockSpec((1, d), lambda i: (0, 0)),
            pl.BlockSpec((d, n), lambda i: (0, 0)),
            pl.BlockSpec((1, n), lambda i: (0, 0)),
        ],
        out_specs=pl.BlockSpec((tm, n), lambda i: (i, 0)),
        out_shape=jax.ShapeDtypeStruct((rows, n), out_dtype),
        compiler_params=pltpu.CompilerParams(
            dimension_semantics=("arbitrary",), vmem_limit_bytes=VMEM_LIMIT),
        name="rms_proj",
    )(h2d, g, w, col_scale)


def _attn_kernel(qi_ref, wi_ref, kit_ref, tri_ref, q_ref, k_ref, v_ref, o_ref,
                 s_scr, cand_scr, cand_t_scr, bias_scr, wb_scr, thr_scr,
                 need_scr, carry_scr, m_scr, l_scr, acc_scr, flag_scr, *, topk, g_tiles):
    tq = ROW_TILE
    kt = ROW_TILE
    rg = 128
    nh = kt // 128
    iq = pl.program_id(1)
    j = pl.program_id(2)
    nq = pl.num_programs(1) - 1
    nkb = pl.num_programs(2)
    i = iq + 1
    n_tiles = i + 1
    has_scores = iq < nq
    has_attn = iq >= 1
    t0 = j * g_tiles
    n_sc = jnp.where(has_scores, jnp.clip(n_tiles - t0, 0, g_tiles), 0)
    n_at = jnp.where(has_attn, jnp.clip(iq + 1 - t0, 0, g_tiles), 0)
    n_both = jnp.minimum(n_sc, n_at)
    kf = float(topk)

    def lanes_x(v):
        return jnp.concatenate([v] * nh, axis=1)

    def to_dense(rep):
        return jnp.concatenate(
            [rep[g * 128:(g + 1) * 128, :].T[0:8, :] for g in range(tq // 128)], axis=1)

    def to_rows(dense):
        return jnp.concatenate(
            [jnp.broadcast_to(dense[0:1, g * 128:(g + 1) * 128], (128, 128)).T
             for g in range(tq // 128)], axis=0)

    def reduce_scores(vec, fn, init, comb, red):
        vec_rows = to_rows(vec)
        outs = []
        for g in range(tq // rg):
            rows = slice(g * rg, (g + 1) * rg)
            bv = vec_rows[rows]

            def body(t, acc, rows=rows, bv=bv):
                for hl in range(nh):
                    acc = comb(acc, fn(s_scr[t, rows, hl * 128:(hl + 1) * 128], bv))
                return acc

            acc = lax.fori_loop(0, n_tiles, body, jnp.full((rg, 128), init, f32))
            outs.append(jnp.broadcast_to(red(acc, axis=1, keepdims=True), (rg, 128)))
        return to_dense(jnp.concatenate(outs, axis=0))

    def reduce_lists(vec, fn, init, comb, red):
        accs = [jnp.full((8, tq), init, f32) for _ in range(4)]
        for c in range(LANE_LIST * 128 // 8):
            accs[c % 4] = comb(accs[c % 4], fn(cand_t_scr[c * 8:(c + 1) * 8, :], vec))
        acc = comb(comb(accs[0], accs[1]), comb(accs[2], accs[3]))
        return jnp.broadcast_to(red(acc, axis=0, keepdims=True), (8, tq))

    def any_row(mask):
        return jnp.max(jnp.where(mask, 1.0, 0.0)) > 0.0

    @pl.when(jnp.logical_and(j == 0, has_attn))
    def _init_attention():
        m_scr[...] = jnp.full(m_scr.shape, NEG, f32)
        l_scr[...] = jnp.zeros(l_scr.shape, f32)
        acc_scr[...] = jnp.zeros(acc_scr.shape, f32)
        carry_scr[...] = jnp.zeros(carry_scr.shape, f32)

    @pl.when(jnp.logical_and(j == 0, has_scores))
    def _init_scores():
        w = wi_ref[0] * IDX_SCALE
        for h in range(N_IDX_HEADS):
            wb_scr[h] = jnp.broadcast_to(w[:, h:h + 1], (tq, 128))

    def score_tile(s, causal):
        t = t0 + s
        kt_tile = kit_ref[0, t]
        acc = None
        for h in range(N_IDX_HEADS):
            x = jnp.dot(qi_ref[0, h], kt_tile, preferred_element_type=f32)
            term = jnp.maximum(x, 0.0) * lanes_x(wb_scr[h])
            acc = term if acc is None else acc + term
        kpos = t * kt + lax.broadcasted_iota(jnp.int32, (1, kt), 1)
        acc = acc + jnp.where(kpos >= FRONT_PAD, 0.0, -jnp.inf)
        if causal:
            qpos1 = i * tq + lax.broadcasted_iota(jnp.int32, (tq, 1), 0)
            acc = jnp.where(kpos <= qpos1, acc, -jnp.inf)
        s_scr[t] = acc

    def attend_mask(s, buf):
        t = t0 + s
        tie_path = flag_scr[0] > 0

        @pl.when(jnp.logical_not(tie_path))
        def _():
            thr = thr_scr[...]
            for hl in range(nh):
                lanes = slice(hl * 128, (hl + 1) * 128)
                bias_scr[buf, :, lanes] = jnp.where(s_scr[t, :, lanes] >= thr, 0.0, NEG)

        @pl.when(tie_path)
        def _():
            sc = s_scr[t]
            thr = lanes_x(thr_scr[...])
            tie = sc == thr
            counts = jnp.dot(jnp.where(tie, 1.0, 0.0).astype(bf16), tri_ref[...],
                             preferred_element_type=f32)
            seen = carry_scr[...]
            keep = (counts[:, :kt] + lanes_x(seen)) <= lanes_x(need_scr[...])
            bias_scr[buf] = jnp.where(sc > thr, 0.0,
                                      jnp.where(jnp.logical_and(tie, keep), 0.0, NEG))
            carry_scr[...] = seen + counts[:, kt:]

    def attend_heads(s, buf):
        ones_cols = jnp.ones((kt, HEAD_DIM), bf16)
        row0 = pl.multiple_of(s * kt, kt)
        for h in range(N_HEADS):
            cols = slice(h * HEAD_DIM, (h + 1) * HEAD_DIM)
            qh = q_ref[0, :, cols]
            kh = k_ref[0, pl.ds(row0, kt), cols]
            vh = v_ref[0, pl.ds(row0, kt), cols]
            lg = lax.dot_general(qh, kh, (((1,), (1,)), ((), ())),
                                 preferred_element_type=f32) + bias_scr[buf]
            m_old = m_scr[h]
            m_new = jnp.maximum(m_old, jnp.max(lg, axis=1, keepdims=True))
            alpha = jnp.exp2(m_old - m_new)
            p = jnp.exp2(lg - lanes_x(m_new))
            v_ext = jnp.concatenate([vh, ones_cols], axis=1)
            pv = jnp.dot(p.astype(bf16), v_ext, preferred_element_type=f32)
            l_scr[h] = alpha * l_scr[h] + pv[:, HEAD_DIM:]
            acc_scr[:, cols] = alpha * acc_scr[:, cols] + pv[:, :HEAD_DIM]
            m_scr[h] = m_new

    def pair_body(p, carry):
        attend_mask(2 * p, 0)
        attend_mask(2 * p + 1, 1)
        attend_heads(2 * p, 0)
        score_tile(2 * p, causal=False)
        attend_heads(2 * p + 1, 1)
        score_tile(2 * p + 1, causal=False)
        return carry

    def both_body(s, carry):
        attend_mask(s, 0)
        attend_heads(s, 0)
        score_tile(s, causal=False)
        return carry

    def attend_body(s, carry):
        attend_mask(s, 0)
        attend_heads(s, 0)
        return carry

    def score_body(s, carry):
        score_tile(s, causal=True)
        return carry

    n_pairs = n_both // 2
    lax.fori_loop(0, n_pairs, pair_body, 0)
    lax.fori_loop(2 * n_pairs, n_both, both_body, 0)
    lax.fori_loop(n_both, n_at, attend_body, 0)
    lax.fori_loop(n_both, n_sc, score_body, 0)

    @pl.when(jnp.logical_and(has_scores, j == (n_tiles - 1) // g_tiles))
    def _threshold():
        s_scr[n_tiles] = jnp.full((tq, kt), -jnp.inf, f32)
        qpos = i * tq + lax.broadcasted_iota(jnp.int32, (8, tq), 1)
        n_vis = (qpos - (FRONT_PAD - 1)).astype(f32)
        short = n_vis <= kf

        def search(reduce):
            def count_ge(v):
                return reduce(v, lambda blk, bv: jnp.where(blk >= bv, 1.0, 0.0), 0.0, jnp.add, jnp.sum)

            def count_gt(v):
                return reduce(v, lambda blk, bv: jnp.where(blk > bv, 1.0, 0.0), 0.0, jnp.add, jnp.sum)

            zero = jnp.zeros((8, tq), f32)
            cge0 = count_ge(zero)
            cgt0 = count_gt(zero)
            at_zero = jnp.logical_and(jnp.logical_not(short),
                                      jnp.logical_and(cgt0 < kf, cge0 >= kf))
            res0 = jnp.where(jnp.logical_or(short, at_zero), 1.0, 0.0)
            thr0 = jnp.where(short, BELOW_ALL, zero)
            need0 = jnp.where(jnp.logical_and(at_zero, cge0 > kf), kf - cgt0, KEEP_ALL)
            lo0 = jnp.where(cgt0 >= kf, jnp.maximum(lo_lists, zero), lo_lists)
            hi0 = jnp.where(cge0 < kf, jnp.minimum(hi_lists, zero), hi_lists)

            def bis_cond(st):
                it, _, _, _, _, active = st
                return jnp.logical_and(it < BISECT_CAP, jnp.max(active) > 0.0)

            def bis_body(st):
                it, lo, hi, thr, resolved, active = st
                piv = lo + (hi - lo) * 0.5
                splits = jnp.logical_and(piv > lo, piv < hi)
                c = count_ge(piv)
                upd = jnp.logical_and(active > 0.0, splits)
                hit = jnp.logical_and(upd, c == kf)
                thr = jnp.where(hit, piv, thr)
                resolved = jnp.where(hit, 1.0, resolved)
                lo = jnp.where(jnp.logical_and(upd, c > kf), piv, lo)
                hi = jnp.where(jnp.logical_and(upd, c < kf), piv, hi)
                active = jnp.where(jnp.logical_and(upd, jnp.logical_not(hit)), 1.0, 0.0)
                return it + 1, lo, hi, thr, resolved, active

            st = (jnp.int32(0), lo0, hi0, thr0, res0, 1.0 - res0)
            _, lo, hi, thr, resolved, _ = lax.while_loop(bis_cond, bis_body, st)

            thr_scr[...] = to_rows(thr)
            need_scr[...] = to_rows(need0)
            flag_scr[0] = jnp.where(any_row(need0 < KEEP_ALL), 1, 0).astype(jnp.int32)
            unresolved = resolved < 0.5

            @pl.when(any_row(unresolved))
            def _fallback():
                v0 = reduce(hi, lambda blk, bv: jnp.where(blk <= bv, blk, -jnp.inf),
                            -jnp.inf, jnp.maximum, jnp.max)

                def fb_cond(st):
                    _, cge, _ = st
                    return any_row(jnp.logical_and(unresolved, cge < kf))

                def fb_body(st):
                    v, cge, cgt = st
                    more = jnp.logical_and(unresolved, cge < kf)
                    v2 = reduce(v, lambda blk, bv: jnp.where(blk < bv, blk, -jnp.inf),
                                -jnp.inf, jnp.maximum, jnp.max)
                    return (jnp.where(more, v2, v), jnp.where(more, count_ge(v2), cge),
                            jnp.where(more, count_gt(v2), cgt))

                v, cge, cgt = lax.while_loop(fb_cond, fb_body, (v0, count_ge(v0), count_gt(v0)))
                need = kf - cgt
                partial = jnp.logical_and(unresolved, (cge - cgt) > need)
                thr_scr[...] = to_rows(jnp.where(unresolved, v, thr))
                need_scr[...] = to_rows(jnp.where(partial, need, need0))

                @pl.when(any_row(partial))
                def _():
                    flag_scr[0] = jnp.int32(1)

        def list_group(g, carry):
            r0 = pl.multiple_of(g * LIST_ROWS, LIST_ROWS)

            def insert_pair(p, lists):
                lists = list(lists)
                for u in range(2):
                    for hl in range(nh):
                        x = s_scr[2 * p + u, pl.ds(r0, LIST_ROWS), hl * 128:(hl + 1) * 128]
                        for k in range(LANE_LIST):
                            top = jnp.maximum(lists[k], x)
                            x = jnp.minimum(lists[k], x)
                            lists[k] = top
                return tuple(lists)

            lists = lax.fori_loop(
                0, (n_tiles + 1) // 2, insert_pair,
                tuple(jnp.full((LIST_ROWS, 128), -jnp.inf, f32) for _ in range(LANE_LIST)))
            for k in range(LANE_LIST):
                cand_scr[k, pl.ds(r0, LIST_ROWS), :] = lists[k]
            return carry

        lax.fori_loop(0, tq // LIST_ROWS, list_group, 0)
        for k in range(LANE_LIST):
            for g in range(tq // 128):
                cand_t_scr[k * 128:(k + 1) * 128, g * 128:(g + 1) * 128] = (
                    cand_scr[k, g * 128:(g + 1) * 128, :].T)

        second = cand_t_scr[128:256, :]
        lo_lists = jnp.broadcast_to(jnp.min(second, axis=0, keepdims=True), (8, tq))
        hi_lists = jnp.broadcast_to(jnp.max(second, axis=0, keepdims=True), (8, tq))

        search(reduce_lists)

        @pl.when(any_row(cand_scr[LANE_LIST - 1] >= thr_scr[...]))
        def _():
            search(reduce_scores)

    @pl.when(jnp.logical_and(has_attn, j == nkb - 1))
    def _finish():
        for h in range(N_HEADS):
            cols = slice(h * HEAD_DIM, (h + 1) * HEAD_DIM)
            o_ref[0, :, cols] = (acc_scr[:, cols] / l_scr[h]).astype(o_ref.dtype)


def _attention(qi4, wi, kit, qkv, *, seq, lp, topk):
    b = qkv.shape[0]
    tq = ROW_TILE
    nt = lp // tq
    g_tiles = max(g for g in range(1, MAX_KV_TILES_PER_STEP + 1) if nt % g == 0)
    nkb = nt // g_tiles
    nq = seq // tq

    def kv_block(iq, j):
        return jnp.minimum(j, iq // g_tiles)

    def score_tile_idx(iq):
        return jnp.minimum(iq + 1, nq)

    ka = lax.broadcasted_iota(jnp.int32, (tq, tq + 128), 0)
    kb = lax.broadcasted_iota(jnp.int32, (tq, tq + 128), 1)
    tri = jnp.where(jnp.logical_or(ka <= kb, kb >= tq), 1.0, 0.0).astype(bf16)

    kern = functools.partial(_attn_kernel, topk=topk, g_tiles=g_tiles)
    return pl.pallas_call(
        kern,
        grid=(b, nq + 1, nkb),
        in_specs=[
            pl.BlockSpec((1, N_IDX_HEADS, tq, IDX_DIM),
                         lambda bb, iq, j: (bb, 0, score_tile_idx(iq), 0)),
            pl.BlockSpec((1, tq, N_IDX_HEADS), lambda bb, iq, j: (bb, score_tile_idx(iq), 0)),
            pl.BlockSpec((1, nt, IDX_DIM, tq), lambda bb, iq, j: (bb, 0, 0, 0)),
            pl.BlockSpec((tq, tq + 128), lambda bb, iq, j: (0, 0)),
            pl.BlockSpec((1, tq, ATTN_WIDTH), lambda bb, iq, j: (bb, iq, 0)),
            pl.BlockSpec((1, g_tiles * tq, ATTN_WIDTH), lambda bb, iq, j: (bb, kv_block(iq, j), 1)),
            pl.BlockSpec((1, g_tiles * tq, ATTN_WIDTH), lambda bb, iq, j: (bb, kv_block(iq, j), 2)),
        ],
        out_specs=pl.BlockSpec((1, tq, ATTN_WIDTH), lambda bb, iq, j: (bb, jnp.maximum(iq - 1, 0), 0)),
        out_shape=jax.ShapeDtypeStruct((b, seq, ATTN_WIDTH), bf16),
        scratch_shapes=[
            pltpu.VMEM((nt + 1, tq, tq), f32),
            pltpu.VMEM((LANE_LIST, tq, 128), f32),
            pltpu.VMEM((LANE_LIST * 128, tq), f32),
            pltpu.VMEM((2, tq, tq), f32),
            pltpu.VMEM((N_IDX_HEADS, tq, 128), f32),
            pltpu.VMEM((tq, 128), f32),
            pltpu.VMEM((tq, 128), f32),
            pltpu.VMEM((tq, 128), f32),
            pltpu.VMEM((N_HEADS, tq, 128), f32),
            pltpu.VMEM((N_HEADS, tq, 128), f32),
            pltpu.VMEM((tq, ATTN_WIDTH), f32),
            pltpu.SMEM((1,), jnp.int32),
        ],
        compiler_params=pltpu.CompilerParams(
            dimension_semantics=("arbitrary", "arbitrary", "arbitrary"),
            vmem_limit_bytes=VMEM_LIMIT),
        name="dsa_attention",
    )(qi4, wi, kit, tri, qkv, qkv, qkv)


def _mix_kernel(x_ref, ya_ref, cv_ref, cvp_ref, gt_ref, cw_ref, wao_ref, wco_ref, wout_ref,
                o_ref, u_scr):
    tm = x_ref.shape[1]
    c = cw_ref.shape[1]
    hr = cvp_ref.shape[1]
    cu = cv_ref[0, :, 0:c].astype(f32)
    cb = cv_ref[0, :, c:2 * c].astype(f32)
    cc = cv_ref[0, :, 2 * c:3 * c].astype(f32)
    u_scr[0:hr, :] = cvp_ref[0, :, 2 * c:3 * c].astype(f32) * cvp_ref[0, :, 0:c].astype(f32)
    u_scr[hr:hr + tm, :] = cc * cu
    w = cw_ref[...]
    conv = (w[2:3] * u_scr[hr:hr + tm, :] + w[1:2] * u_scr[hr - 1:hr - 1 + tm, :]
            + w[0:1] * u_scr[hr - 2:hr - 2 + tm, :])
    y_conv = jnp.dot((cb * conv).astype(bf16), wco_ref[...], preferred_element_type=f32)
    y_attn = jnp.dot(ya_ref[0], wao_ref[...], preferred_element_type=f32)
    d = wout_ref.shape[0]
    mixed = (_sigmoid(gt_ref[0, :, 0:d].astype(f32)) * y_attn
             + _sigmoid(gt_ref[0, :, d:2 * d].astype(f32)) * y_conv)
    o_ref[0] = x_ref[0] + jnp.dot(mixed.astype(bf16), wout_ref[...], preferred_element_type=f32)


def _mix(x, y_attn, conv3, gates3, conv_w, wao, wco, wout):
    b, seq, d = x.shape
    tm = ROW_TILE
    c = conv_w.shape[1]
    const = lambda bb, i: (0, 0)
    return pl.pallas_call(
        _mix_kernel,
        grid=(b, seq // tm),
        in_specs=[
            pl.BlockSpec((1, tm, d), lambda bb, i: (bb, i, 0)),
            pl.BlockSpec((1, tm, ATTN_WIDTH), lambda bb, i: (bb, i, 0)),
            pl.BlockSpec((1, tm, 3 * c), lambda bb, i: (bb, i + 1, 0)),
            pl.BlockSpec((1, HALO_ROWS, 3 * c),
                         lambda bb, i: (bb, (i + 1) * (tm // HALO_ROWS) - 1, 0)),
            pl.BlockSpec((1, tm, 2 * d), lambda bb, i: (bb, i + 1, 0)),
            pl.BlockSpec((CONV_K, c), const),
            pl.BlockSpec(wao.shape, const),
            pl.BlockSpec(wco.shape, const),
            pl.BlockSpec(wout.shape, const),
        ],
        out_specs=pl.BlockSpec((1, tm, d), lambda bb, i: (bb, i, 0)),
        out_shape=jax.ShapeDtypeStruct((b, seq, d), f32),
        scratch_shapes=[pltpu.VMEM((tm + HALO_ROWS, c), f32)],
        compiler_params=pltpu.CompilerParams(
            dimension_semantics=("arbitrary", "arbitrary"), vmem_limit_bytes=VMEM_LIMIT),
        name="mix_merge",
    )(x, y_attn, conv3, conv3, gates3, conv_w, wao, wco, wout)


def _ffn_kernel(h_ref, g1_ref, wg_ref, wu_ref, wd_ref, g2_ref, o_ref):
    h = h_ref[0]
    f = _rms(h, g1_ref[...]).astype(bf16)
    gate = jnp.dot(f, wg_ref[...], preferred_element_type=f32)
    up = jnp.dot(f, wu_ref[...], preferred_element_type=f32)
    act = (gate * _sigmoid(gate)) * up
    h2 = h + jnp.dot(act.astype(bf16), wd_ref[...], preferred_element_type=f32)
    o_ref[0] = _rms(h2, g2_ref[...])


def _ffn(h1, g1, wg, wu, wd, g2):
    b, seq, d = h1.shape
    tm = ROW_TILE
    const = lambda bb, i: (0, 0)
    return pl.pallas_call(
        _ffn_kernel,
        grid=(b, seq // tm),
        in_specs=[
            pl.BlockSpec((1, tm, d), lambda bb, i: (bb, i, 0)),
            pl.BlockSpec((1, d), const),
            pl.BlockSpec(wg.shape, const),
            pl.BlockSpec(wu.shape, const),
            pl.BlockSpec(wd.shape, const),
            pl.BlockSpec((1, d), const),
        ],
        out_specs=pl.BlockSpec((1, tm, d), lambda bb, i: (bb, i, 0)),
        out_shape=jax.ShapeDtypeStruct((b, seq, d), f32),
        compiler_params=pltpu.CompilerParams(
            dimension_semantics=("arbitrary", "arbitrary"), vmem_limit_bytes=VMEM_LIMIT),
        name="ffn_final",
    )(h1, g1, wg, wu, wd, g2)


def kernel(x, meta_tokens, norm_mix_g, w_in, w_attn_out, conv_w, w_conv_out, w_out,
           norm_ffn_g, w_gate, w_up, w_down, norm_final_g):
    b, seq, d = x.shape
    assert w_in.shape[0] == 1, "single-layer block"
    assert seq % ROW_TILE == 0 and meta_tokens.shape[0] == N_META
    c = conv_w.shape[2]
    lp = ROW_TILE + seq
    l_real = N_META + seq
    topk = min(TOPK_MAX, l_real // 4)
    nt = lp // ROW_TILE

    meta = jnp.broadcast_to(meta_tokens[None].astype(x.dtype), (b, N_META, d))
    hp = jnp.concatenate([jnp.zeros((b, FRONT_PAD, d), x.dtype), meta, x], axis=1)
    h2d = hp.reshape(b * lp, d)

    w = w_in[0]
    o_idx = 3 * ATTN_WIDTH
    n_idx = N_IDX_HEADS * IDX_DIM + IDX_DIM + N_IDX_HEADS
    o_conv = o_idx + n_idx
    o_gate = o_conv + 3 * c
    n_idx_pad = -(-n_idx // 128) * 128
    w_qkv = w[:, :o_idx].astype(bf16)
    w_idx = jnp.pad(w[:, o_idx:o_conv], ((0, 0), (0, n_idx_pad - n_idx))).astype(bf16)
    w_conv = w[:, o_conv:o_gate].astype(bf16)
    w_gates = w[:, o_gate:].astype(bf16)
    g_mix = norm_mix_g[0][None]

    tm = 512 if (b * lp) % 512 == 0 else ROW_TILE
    qkv_scale = jnp.concatenate([jnp.full((1, ATTN_WIDTH), LOG2E * HEAD_DIM ** -0.5, f32),
                                 jnp.ones((1, 2 * ATTN_WIDTH), f32)], axis=1)
    qkv = _rms_proj(h2d, g_mix, w_qkv, qkv_scale, bf16, tm).reshape(b, lp, 3 * ATTN_WIDTH)
    idx = _rms_proj(h2d, g_mix, w_idx, jnp.ones((1, n_idx_pad), f32), f32, tm)
    conv3 = _rms_proj(h2d, g_mix, w_conv, jnp.ones((1, 3 * c), f32), bf16, tm).reshape(b, lp, 3 * c)
    gates3 = _rms_proj(h2d, g_mix, w_gates, jnp.ones((1, 2 * d), f32), bf16, tm).reshape(b, lp, 2 * d)

    nqi = N_IDX_HEADS * IDX_DIM
    qi4 = idx[:, :nqi].astype(bf16).reshape(b, lp, N_IDX_HEADS, IDX_DIM).transpose(0, 2, 1, 3)
    kit = idx[:, nqi:nqi + IDX_DIM].astype(bf16).reshape(b, nt, ROW_TILE, IDX_DIM).transpose(0, 1, 3, 2)
    wi = idx[:, nqi + IDX_DIM:n_idx].reshape(b, lp, N_IDX_HEADS)

    y_attn = _attention(qi4, wi, kit, qkv, seq=seq, lp=lp, topk=topk)

    h1 = _mix(x, y_attn, conv3, gates3, conv_w[0], w_attn_out[0].astype(bf16),
              w_conv_out[0].astype(bf16), w_out[0].astype(bf16))
    return _ffn(h1, norm_ffn_g[0][None], w_gate[0].astype(bf16), w_up[0].astype(bf16),
                w_down[0].astype(bf16), norm_final_g[None])
```

```python
import functools

import jax
import jax.numpy as jnp
from jax import lax
from jax.experimental import pallas as pl
from jax.experimental.pallas import tpu as pltpu

N_META = 16
N_HEADS = 8
HEAD_DIM = 128
ATTN_WIDTH = N_HEADS * HEAD_DIM
N_IDX_HEADS = 8
IDX_DIM = 64
TOPK_MAX = 256
CONV_K = 3
EPS = 1e-6
IDX_SCALE = (N_IDX_HEADS ** -0.5) * (IDX_DIM ** -0.5)

ROW_TILE = 256
FRONT_PAD = ROW_TILE - N_META
MAX_KV_TILES_PER_STEP = 13
BISECT_CAP = 32
NEG = -1e30
KEEP_ALL = 1e9
BELOW_ALL = -3e38
HALO_ROWS = 16
LANE_LIST = 12
LIST_ROWS = 16
LOG2E = 1.4426950408889634
VMEM_LIMIT = 58 * 1024 * 1024

f32 = jnp.float32
bf16 = jnp.bfloat16


def _rms(x, g):
    return (x * lax.rsqrt(jnp.mean(x * x, axis=-1, keepdims=True) + EPS)) * g


def _sigmoid(x):
    return 1.0 / (1.0 + jnp.exp(-x))


def _rms_proj_kernel(h_ref, g_ref, w_ref, cs_ref, o_ref):
    a = _rms(h_ref[...], g_ref[...]).astype(bf16)
    y = jnp.dot(a, w_ref[...], preferred_element_type=f32)
    o_ref[...] = (y * cs_ref[...]).astype(o_ref.dtype)


def _rms_proj(h2d, g, w, col_scale, out_dtype, tm):
    rows, d = h2d.shape
    n = w.shape[1]
    return pl.pallas_call(
        _rms_proj_kernel,
        grid=(rows // tm,),
        in_specs=[
            pl.BlockSpec((tm, d), lambda i: (i, 0)),
            pl.BlockSpec((1, d), lambda i: (0, 0)),
            pl.BlockSpec((d, n), lambda i: (0, 0)),
            pl.BlockSpec((1, n), lambda i: (0, 0)),
        ],
        out_specs=pl.BlockSpec((tm, n), lambda i: (i, 0)),
        out_shape=jax.ShapeDtypeStruct((rows, n), out_dtype),
        compiler_params=pltpu.CompilerParams(
            dimension_semantics=("arbitrary",), vmem_limit_bytes=VMEM_LIMIT),
        name="rms_proj",
    )(h2d, g, w, col_scale)


def _attn_kernel(qi_ref, wi_ref, kit_ref, tri_ref, q_ref, k_ref, v_ref, o_ref,
                 s_scr, cand_scr, cand_t_scr, bias_scr, wb_scr, thr_scr,
                 need_scr, carry_scr, m_scr, l_scr, acc_scr, flag_scr, *, topk, g_tiles):
    tq = ROW_TILE
    kt = ROW_TILE
    rg = 128
    nh = kt // 128
    iq = pl.program_id(1)
    j = pl.program_id(2)
    nq = pl.num_programs(1) - 1
    nkb = pl.num_programs(2)
    i = iq + 1
    n_tiles = i + 1
    has_scores = iq < nq
    has_attn = iq >= 1
    t0 = j * g_tiles
    n_sc = jnp.where(has_scores, jnp.clip(n_tiles - t0, 0, g_tiles), 0)
    n_at = jnp.where(has_attn, jnp.clip(iq + 1 - t0, 0, g_tiles), 0)
    n_both = jnp.minimum(n_sc, n_at)
    kf = float(topk)

    def lanes_x(v):
        return jnp.concatenate([v] * nh, axis=1)

    def to_dense(rep):
        return jnp.concatenate(
            [rep[g * 128:(g + 1) * 128, :].T[0:8, :] for g in range(tq // 128)], axis=1)

    def to_rows(dense):
        return jnp.concatenate(
            [jnp.broadcast_to(dense[0:1, g * 128:(g + 1) * 128], (128, 128)).T
             for g in range(tq // 128)], axis=0)

    def reduce_scores(vec, fn, init, comb, red):
        vec_rows = to_rows(vec)
        outs = []
        for g in range(tq // rg):
            rows = slice(g * rg, (g + 1) * rg)
            bv = vec_rows[rows]

            def body(t, acc, rows=rows, bv=bv):
                for hl in range(nh):
                    acc = comb(acc, fn(s_scr[t, rows, hl * 128:(hl + 1) * 128], bv))
                return acc

            acc = lax.fori_loop(0, n_tiles, body, jnp.full((rg, 128), init, f32))
            outs.append(jnp.broadcast_to(red(acc, axis=1, keepdims=True), (rg, 128)))
        return to_dense(jnp.concatenate(outs, axis=0))

    def reduce_lists(vec, fn, init, comb, red):
        accs = [jnp.full((8, tq), init, f32) for _ in range(4)]
        for c in range(LANE_LIST * 128 // 8):
            accs[c % 4] = comb(accs[c % 4], fn(cand_t_scr[c * 8:(c + 1) * 8, :], vec))
        acc = comb(comb(accs[0], accs[1]), comb(accs[2], accs[3]))
        return jnp.broadcast_to(red(acc, axis=0, keepdims=True), (8, tq))

    def any_row(mask):
        return jnp.max(jnp.where(mask, 1.0, 0.0)) > 0.0

    @pl.when(jnp.logical_and(j == 0, has_attn))
    def _init_attention():
        m_scr[...] = jnp.full(m_scr.shape, NEG, f32)
        l_scr[...] = jnp.zeros(l_scr.shape, f32)
        acc_scr[...] = jnp.zeros(acc_scr.shape, f32)
        carry_scr[...] = jnp.zeros(carry_scr.shape, f32)

    @pl.when(jnp.logical_and(j == 0, has_scores))
    def _init_scores():
        w = wi_ref[0] * IDX_SCALE
        for h in range(N_IDX_HEADS):
            wb_scr[h] = jnp.broadcast_to(w[:, h:h + 1], (tq, 128))

    def score_tile(s, causal):
        t = t0 + s
        kt_tile = kit_ref[0, t]
        acc = None
        for h in range(N_IDX_HEADS):
            x = jnp.dot(qi_ref[0, h], kt_tile, preferred_element_type=f32)
            term = jnp.maximum(x, 0.0) * lanes_x(wb_scr[h])
            acc = term if acc is None else acc + term
        kpos = t * kt + lax.broadcasted_iota(jnp.int32, (1, kt), 1)
        acc = acc + jnp.where(kpos >= FRONT_PAD, 0.0, -jnp.inf)
        if causal:
            qpos1 = i * tq + lax.broadcasted_iota(jnp.int32, (tq, 1), 0)
            acc = jnp.where(kpos <= qpos1, acc, -jnp.inf)
        s_scr[t] = acc

    def attend_mask(s, buf):
        t = t0 + s
        tie_path = flag_scr[0] > 0

        @pl.when(jnp.logical_not(tie_path))
        def _():
            thr = thr_scr[...]
            for hl in range(nh):
                lanes = slice(hl * 128, (hl + 1) * 128)
                bias_scr[buf, :, lanes] = jnp.where(s_scr[t, :, lanes] >= thr, 0.0, NEG)

        @pl.when(tie_path)
        def _():
            sc = s_scr[t]
            thr = lanes_x(thr_scr[...])
            tie = sc == thr
            counts = jnp.dot(jnp.where(tie, 1.0, 0.0).astype(bf16), tri_ref[...],
                             preferred_element_type=f32)
            seen = carry_scr[...]
            keep = (counts[:, :kt] + lanes_x(seen)) <= lanes_x(need_scr[...])
            bias_scr[buf] = jnp.where(sc > thr, 0.0,
                                      jnp.where(jnp.logical_and(tie, keep), 0.0, NEG))
            carry_scr[...] = seen + counts[:, kt:]

    def attend_heads(s, buf):
        ones_cols = jnp.ones((kt, HEAD_DIM), bf16)
        row0 = pl.multiple_of(s * kt, kt)
        for h in range(N_HEADS):
            cols = slice(h * HEAD_DIM, (h + 1) * HEAD_DIM)
            qh = q_ref[0, :, cols]
            kh = k_ref[0, pl.ds(row0, kt), cols]
            vh = v_ref[0, pl.ds(row0, kt), cols]
            lg = lax.dot_general(qh, kh, (((1,), (1,)), ((), ())),
                                 preferred_element_type=f32) + bias_scr[buf]
            m_old = m_scr[h]
            m_new = jnp.maximum(m_old, jnp.max(lg, axis=1, keepdims=True))
            alpha = jnp.exp2(m_old - m_new)
            p = jnp.exp2(lg - lanes_x(m_new))
            v_ext = jnp.concatenate([vh, ones_cols], axis=1)
            pv = jnp.dot(p.astype(bf16), v_ext, preferred_element_type=f32)
            l_scr[h] = alpha * l_scr[h] + pv[:, HEAD_DIM:]
            acc_scr[:, cols] = alpha * acc_scr[:, cols] + pv[:, :HEAD_DIM]
            m_scr[h] = m_new

    def pair_body(p, carry):
        attend_mask(2 * p, 0)
        attend_mask(2 * p + 1, 1)
        attend_heads(2 * p, 0)
        score_tile(2 * p, causal=False)
        attend_heads(2 * p + 1, 1)
        score_tile(2 * p + 1, causal=False)
        return carry

    def both_body(s, carry):
        attend_mask(s, 0)
        attend_heads(s, 0)
        score_tile(s, causal=False)
        return carry

    def attend_body(s, carry):
        attend_mask(s, 0)
        attend_heads(s, 0)
        return carry

    def score_body(s, carry):
        score_tile(s, causal=True)
        return carry

    n_pairs = n_both // 2
    lax.fori_loop(0, n_pairs, pair_body, 0)
    lax.fori_loop(2 * n_pairs, n_both, both_body, 0)
    lax.fori_loop(n_both, n_at, attend_body, 0)
    lax.fori_loop(n_both, n_sc, score_body, 0)

    @pl.when(jnp.logical_and(has_scores, j == (n_tiles - 1) // g_tiles))
    def _threshold():
        s_scr[n_tiles] = jnp.full((tq, kt), -jnp.inf, f32)
        qpos = i * tq + lax.broadcasted_iota(jnp.int32, (8, tq), 1)
        n_vis = (qpos - (FRONT_PAD - 1)).astype(f32)
        short = n_vis <= kf

        def search(reduce):
            def count_ge(v):
                return reduce(v, lambda blk, bv: jnp.where(blk >= bv, 1.0, 0.0), 0.0, jnp.add, jnp.sum)

            def count_gt(v):
                return reduce(v, lambda blk, bv: jnp.where(blk > bv, 1.0, 0.0), 0.0, jnp.add, jnp.sum)

            zero = jnp.zeros((8, tq), f32)
            cge0 = count_ge(zero)
            cgt0 = count_gt(zero)
            at_zero = jnp.logical_and(jnp.logical_not(short),
                                      jnp.logical_and(cgt0 < kf, cge0 >= kf))
            res0 = jnp.where(jnp.logical_or(short, at_zero), 1.0, 0.0)
            thr0 = jnp.where(short, BELOW_ALL, zero)
            need0 = jnp.where(jnp.logical_and(at_zero, cge0 > kf), kf - cgt0, KEEP_ALL)
            lo0 = jnp.where(cgt0 >= kf, jnp.maximum(lo_lists, zero), lo_lists)
            hi0 = jnp.where(cge0 < kf, jnp.minimum(hi_lists, zero), hi_lists)

            def bis_cond(st):
                it, _, _, _, _, active = st
                return jnp.logical_and(it < BISECT_CAP, jnp.max(active) > 0.0)

            def bis_body(st):
                it, lo, hi, thr, resolved, active = st
                piv = lo + (hi - lo) * 0.5
                splits = jnp.logical_and(piv > lo, piv < hi)
                c = count_ge(piv)
                upd = jnp.logical_and(active > 0.0, splits)
                hit = jnp.logical_and(upd, c == kf)
                thr = jnp.where(hit, piv, thr)
                resolved = jnp.where(hit, 1.0, resolved)
                lo = jnp.where(jnp.logical_and(upd, c > kf), piv, lo)
                hi = jnp.where(jnp.logical_and(upd, c < kf), piv, hi)
                active = jnp.where(jnp.logical_and(upd, jnp.logical_not(hit)), 1.0, 0.0)
                return it + 1, lo, hi, thr, resolved, active

            st = (jnp.int32(0), lo0, hi0, thr0, res0, 1.0 - res0)
            _, lo, hi, thr, resolved, _ = lax.while_loop(bis_cond, bis_body, st)

            thr_scr[...] = to_rows(thr)
            need_scr[...] = to_rows(need0)
            flag_scr[0] = jnp.where(any_row(need0 < KEEP_ALL), 1, 0).astype(jnp.int32)
            unresolved = resolved < 0.5

            @pl.when(any_row(unresolved))
            def _fallback():
                v0 = reduce(hi, lambda blk, bv: jnp.where(blk <= bv, blk, -jnp.inf),
                            -jnp.inf, jnp.maximum, jnp.max)

                def fb_cond(st):
                    _, cge, _ = st
                    return any_row(jnp.logical_and(unresolved, cge < kf))

                def fb_body(st):
                    v, cge, cgt = st
                    more = jnp.logical_and(unresolved, cge < kf)
                    v2 = reduce(v, lambda blk, bv: jnp.where(blk < bv, blk, -jnp.inf),
                                -jnp.inf, jnp.maximum, jnp.max)
                    return (jnp.where(more, v2, v), jnp.where(more, count_ge(v2), cge),
                            jnp.where(more, count_gt(v2), cgt))

                v, cge, cgt = lax.while_loop(fb_cond, fb_body, (v0, count_ge(v0), count_gt(v0)))
                need = kf - cgt
                partial = jnp.logical_and(unresolved, (cge - cgt) > need)
                thr_scr[...] = to_rows(jnp.where(unresolved, v, thr))
                need_scr[...] = to_rows(jnp.where(partial, need, need0))

                @pl.when(any_row(partial))
                def _():
                    flag_scr[0] = jnp.int32(1)

        def list_group(g, carry):
            r0 = pl.multiple_of(g * LIST_ROWS, LIST_ROWS)

            def insert_pair(p, lists):
                lists = list(lists)
                for u in range(2):
                    for hl in range(nh):
                        x = s_scr[2 * p + u, pl.ds(r0, LIST_ROWS), hl * 128:(hl + 1) * 128]
                        for k in range(LANE_LIST):
                            top = jnp.maximum(lists[k], x)
                            x = jnp.minimum(lists[k], x)
                            lists[k] = top
                return tuple(lists)

            lists = lax.fori_loop(
                0, (n_tiles + 1) // 2, insert_pair,
                tuple(jnp.full((LIST_ROWS, 128), -jnp.inf, f32) for _ in range(LANE_LIST)))
            for k in range(LANE_LIST):
                cand_scr[k, pl.ds(r0, LIST_ROWS), :] = lists[k]
            return carry

        lax.fori_loop(0, tq // LIST_ROWS, list_group, 0)
        for k in range(LANE_LIST):
            for g in range(tq // 128):
                cand_t_scr[k * 128:(k + 1) * 128, g * 128:(g + 1) * 128] = (
                    cand_scr[k, g * 128:(g + 1) * 128, :].T)

        second = cand_t_scr[128:256, :]
        lo_lists = jnp.broadcast_to(jnp.min(second, axis=0, keepdims=True), (8, tq))
        hi_lists = jnp.broadcast_to(jnp.max(second, axis=0, keepdims=True), (8, tq))

        search(reduce_lists)

        @pl.when(any_row(cand_scr[LANE_LIST - 1] >= thr_scr[...]))
        def _():
            search(reduce_scores)

    @pl.when(jnp.logical_and(has_attn, j == nkb - 1))
    def _finish():
        for h in range(N_HEADS):
            cols = slice(h * HEAD_DIM, (h + 1) * HEAD_DIM)
            o_ref[0, :, cols] = (acc_scr[:, cols] / l_scr[h]).astype(o_ref.dtype)


def _attention(qi4, wi, kit, qkv, *, seq, lp, topk):
    b = qkv.shape[0]
    tq = ROW_TILE
    nt = lp // tq
    g_tiles = max(g for g in range(1, MAX_KV_TILES_PER_STEP + 1) if nt % g == 0)
    nkb = nt // g_tiles
    nq = seq // tq

    def kv_block(iq, j):
        return jnp.minimum(j, iq // g_tiles)

    def score_tile_idx(iq):
        return jnp.minimum(iq + 1, nq)

    ka = lax.broadcasted_iota(jnp.int32, (tq, tq + 128), 0)
    kb = lax.broadcasted_iota(jnp.int32, (tq, tq + 128), 1)
    tri = jnp.where(jnp.logical_or(ka <= kb, kb >= tq), 1.0, 0.0).astype(bf16)

    kern = functools.partial(_attn_kernel, topk=topk, g_tiles=g_tiles)
    return pl.pallas_call(
        kern,
        grid=(b, nq + 1, nkb),
        in_specs=[
            pl.BlockSpec((1, N_IDX_HEADS, tq, IDX_DIM),
                         lambda bb, iq, j: (bb, 0, score_tile_idx(iq), 0)),
            pl.BlockSpec((1, tq, N_IDX_HEADS), lambda bb, iq, j: (bb, score_tile_idx(iq), 0)),
            pl.BlockSpec((1, nt, IDX_DIM, tq), lambda bb, iq, j: (bb, 0, 0, 0),
                         pipeline_mode=pl.Buffered(1)),
            pl.BlockSpec((tq, tq + 128), lambda bb, iq, j: (0, 0)),
            pl.BlockSpec((1, tq, ATTN_WIDTH), lambda bb, iq, j: (bb, iq, 0)),
            pl.BlockSpec((1, g_tiles * tq, ATTN_WIDTH), lambda bb, iq, j: (bb, kv_block(iq, j), 1)),
            pl.BlockSpec((1, g_tiles * tq, ATTN_WIDTH), lambda bb, iq, j: (bb, kv_block(iq, j), 2)),
        ],
        out_specs=pl.BlockSpec((1, tq, ATTN_WIDTH), lambda bb, iq, j: (bb, jnp.maximum(iq - 1, 0), 0)),
        out_shape=jax.ShapeDtypeStruct((b, seq, ATTN_WIDTH), bf16),
        scratch_shapes=[
            pltpu.VMEM((nt + 1, tq, tq), f32),
            pltpu.VMEM((LANE_LIST, tq, 128), f32),
            pltpu.VMEM((LANE_LIST * 128, tq), f32),
            pltpu.VMEM((2, tq, tq), f32),
            pltpu.VMEM((N_IDX_HEADS, tq, 128), f32),
            pltpu.VMEM((tq, 128), f32),
            pltpu.VMEM((tq, 128), f32),
            pltpu.VMEM((tq, 128), f32),
            pltpu.VMEM((N_HEADS, tq, 128), f32),
            pltpu.VMEM((N_HEADS, tq, 128), f32),
            pltpu.VMEM((tq, ATTN_WIDTH), f32),
            pltpu.SMEM((1,), jnp.int32),
        ],
        compiler_params=pltpu.CompilerParams(
            dimension_semantics=("arbitrary", "arbitrary", "arbitrary"),
            vmem_limit_bytes=VMEM_LIMIT),
        name="dsa_attention",
    )(qi4, wi, kit, tri, qkv, qkv, qkv)


def _mix_kernel(x_ref, ya_ref, cv_ref, cvp_ref, gt_ref, cw_ref, wao_ref, wco_ref, wout_ref,
                o_ref, u_scr):
    tm = x_ref.shape[1]
    c = cw_ref.shape[1]
    hr = cvp_ref.shape[1]
    cu = cv_ref[0, :, 0:c].astype(f32)
    cb = cv_ref[0, :, c:2 * c].astype(f32)
    cc = cv_ref[0, :, 2 * c:3 * c].astype(f32)
    u_scr[0:hr, :] = cvp_ref[0, :, 2 * c:3 * c].astype(f32) * cvp_ref[0, :, 0:c].astype(f32)
    u_scr[hr:hr + tm, :] = cc * cu
    w = cw_ref[...]
    conv = (w[2:3] * u_scr[hr:hr + tm, :] + w[1:2] * u_scr[hr - 1:hr - 1 + tm, :]
            + w[0:1] * u_scr[hr - 2:hr - 2 + tm, :])
    y_conv = jnp.dot((cb * conv).astype(bf16), wco_ref[...], preferred_element_type=f32)
    y_attn = jnp.dot(ya_ref[0], wao_ref[...], preferred_element_type=f32)
    d = wout_ref.shape[0]
    mixed = (_sigmoid(gt_ref[0, :, 0:d].astype(f32)) * y_attn
             + _sigmoid(gt_ref[0, :, d:2 * d].astype(f32)) * y_conv)
    o_ref[0] = x_ref[0] + jnp.dot(mixed.astype(bf16), wout_ref[...], preferred_element_type=f32)


def _mix(x, y_attn, conv3, gates3, conv_w, wao, wco, wout):
    b, seq, d = x.shape
    tm = ROW_TILE
    c = conv_w.shape[1]
    const = lambda bb, i: (0, 0)
    return pl.pallas_call(
        _mix_kernel,
        grid=(b, seq // tm),
        in_specs=[
            pl.BlockSpec((1, tm, d), lambda bb, i: (bb, i, 0)),
            pl.BlockSpec((1, tm, ATTN_WIDTH), lambda bb, i: (bb, i, 0)),
            pl.BlockSpec((1, tm, 3 * c), lambda bb, i: (bb, i + 1, 0)),
            pl.BlockSpec((1, HALO_ROWS, 3 * c),
                         lambda bb, i: (bb, (i + 1) * (tm // HALO_ROWS) - 1, 0)),
            pl.BlockSpec((1, tm, 2 * d), lambda bb, i: (bb, i + 1, 0)),
            pl.BlockSpec((CONV_K, c), const),
            pl.BlockSpec(wao.shape, const),
            pl.BlockSpec(wco.shape, const),
            pl.BlockSpec(wout.shape, const),
        ],
        out_specs=pl.BlockSpec((1, tm, d), lambda bb, i: (bb, i, 0)),
        out_shape=jax.ShapeDtypeStruct((b, seq, d), f32),
        scratch_shapes=[pltpu.VMEM((tm + HALO_ROWS, c), f32)],
        compiler_params=pltpu.CompilerParams(
            dimension_semantics=("arbitrary", "arbitrary"), vmem_limit_bytes=VMEM_LIMIT),
        name="mix_merge",
    )(x, y_attn, conv3, conv3, gates3, conv_w, wao, wco, wout)


def _ffn_kernel(h_ref, g1_ref, wg_ref, wu_ref, wd_ref, g2_ref, o_ref):
    h = h_ref[0]
    f = _rms(h, g1_ref[...]).astype(bf16)
    gate = jnp.dot(f, wg_ref[...], preferred_element_type=f32)
    up = jnp.dot(f, wu_ref[...], preferred_element_type=f32)
    act = (gate * _sigmoid(gate)) * up
    h2 = h + jnp.dot(act.astype(bf16), wd_ref[...], preferred_element_type=f32)
    o_ref[0] = _rms(h2, g2_ref[...])


def _ffn(h1, g1, wg, wu, wd, g2):
    b, seq, d = h1.shape
    tm = ROW_TILE
    const = lambda bb, i: (0, 0)
    return pl.pallas_call(
        _ffn_kernel,
        grid=(b, seq // tm),
        in_specs=[
            pl.BlockSpec((1, tm, d), lambda bb, i: (bb, i, 0)),
            pl.BlockSpec((1, d), const),
            pl.BlockSpec(wg.shape, const),
            pl.BlockSpec(wu.shape, const),
            pl.BlockSpec(wd.shape, const),
            pl.BlockSpec((1, d), const),
        ],
        out_specs=pl.BlockSpec((1, tm, d), lambda bb, i: (bb, i, 0)),
        out_shape=jax.ShapeDtypeStruct((b, seq, d), f32),
        compiler_params=pltpu.CompilerParams(
            dimension_semantics=("arbitrary", "arbitrary"), vmem_limit_bytes=VMEM_LIMIT),
        name="ffn_final",
    )(h1, g1, wg, wu, wd, g2)


def kernel(x, meta_tokens, norm_mix_g, w_in, w_attn_out, conv_w, w_conv_out, w_out,
           norm_ffn_g, w_gate, w_up, w_down, norm_final_g):
    b, seq, d = x.shape
    assert w_in.shape[0] == 1, "single-layer block"
    assert seq % ROW_TILE == 0 and meta_tokens.shape[0] == N_META
    c = conv_w.shape[2]
    lp = ROW_TILE + seq
    l_real = N_META + seq
    topk = min(TOPK_MAX, l_real // 4)
    nt = lp // ROW_TILE

    meta = jnp.broadcast_to(meta_tokens[None].astype(x.dtype), (b, N_META, d))
    hp = jnp.concatenate([jnp.zeros((b, FRONT_PAD, d), x.dtype), meta, x], axis=1)
    h2d = hp.reshape(b * lp, d)

    w = w_in[0]
    o_idx = 3 * ATTN_WIDTH
    n_idx = N_IDX_HEADS * IDX_DIM + IDX_DIM + N_IDX_HEADS
    o_conv = o_idx + n_idx
    o_gate = o_conv + 3 * c
    n_idx_pad = -(-n_idx // 128) * 128
    w_qkv = w[:, :o_idx].astype(bf16)
    w_idx = jnp.pad(w[:, o_idx:o_conv], ((0, 0), (0, n_idx_pad - n_idx))).astype(bf16)
    w_conv = w[:, o_conv:o_gate].astype(bf16)
    w_gates = w[:, o_gate:].astype(bf16)
    g_mix = norm_mix_g[0][None]

    tm = 512 if (b * lp) % 512 == 0 else ROW_TILE
    qkv_scale = jnp.concatenate([jnp.full((1, ATTN_WIDTH), LOG2E * HEAD_DIM ** -0.5, f32),
                                 jnp.ones((1, 2 * ATTN_WIDTH), f32)], axis=1)
    qkv = _rms_proj(h2d, g_mix, w_qkv, qkv_scale, bf16, tm).reshape(b, lp, 3 * ATTN_WIDTH)
    idx = _rms_proj(h2d, g_mix, w_idx, jnp.ones((1, n_idx_pad), f32), f32, tm)
    conv3 = _rms_proj(h2d, g_mix, w_conv, jnp.ones((1, 3 * c), f32), bf16, tm).reshape(b, lp, 3 * c)
    gates3 = _rms_proj(h2d, g_mix, w_gates, jnp.ones((1, 2 * d), f32), bf16, tm).reshape(b, lp, 2 * d)

    nqi = N_IDX_HEADS * IDX_DIM
    qi4 = idx[:, :nqi].astype(bf16).reshape(b, lp, N_IDX_HEADS, IDX_DIM).transpose(0, 2, 1, 3)
    kit = idx[:, nqi:nqi + IDX_DIM].astype(bf16).reshape(b, nt, ROW_TILE, IDX_DIM).transpose(0, 1, 3, 2)
    wi = idx[:, nqi + IDX_DIM:n_idx].reshape(b, lp, N_IDX_HEADS)

    y_attn = _attention(qi4, wi, kit, qkv, seq=seq, lp=lp, topk=topk)

    h1 = _mix(x, y_attn, conv3, gates3, conv_w[0], w_attn_out[0].astype(bf16),
              w_conv_out[0].astype(bf16), w_out[0].astype(bf16))
    return _ffn(h1, norm_ffn_g[0][None], w_gate[0].astype(bf16), w_up[0].astype(bf16),
                w_down[0].astype(bf16), norm_final_g[None])
```

```python
import functools

import jax
import jax.numpy as jnp
from jax import lax
from jax.experimental import pallas as pl
from jax.experimental.pallas import tpu as pltpu

N_META = 16
N_HEADS = 8
HEAD_DIM = 128
ATTN_WIDTH = N_HEADS * HEAD_DIM
N_IDX_HEADS = 8
IDX_DIM = 64
TOPK_MAX = 256
CONV_K = 3
EPS = 1e-6
IDX_SCALE = (N_IDX_HEADS ** -0.5) * (IDX_DIM ** -0.5)

ROW_TILE = 256
FRONT_PAD = ROW_TILE - N_META
MAX_KV_TILES_PER_STEP = 13
BISECT_CAP = 32
NEG = -1e30
KEEP_ALL = 1e9
BELOW_ALL = -3e38
HALO_ROWS = 16
LANE_LIST = 12
LOG2E = 1.4426950408889634
VMEM_LIMIT = 58 * 1024 * 1024

f32 = jnp.float32
bf16 = jnp.bfloat16


def _rms(x, g):
    return (x * lax.rsqrt(jnp.mean(x * x, axis=-1, keepdims=True) + EPS)) * g


def _sigmoid(x):
    return 1.0 / (1.0 + jnp.exp(-x))


def _rms_proj_kernel(h_ref, g_ref, w_ref, cs_ref, o_ref):
    a = _rms(h_ref[...], g_ref[...]).astype(bf16)
    y = jnp.dot(a, w_ref[...], preferred_element_type=f32)
    o_ref[...] = (y * cs_ref[...]).astype(o_ref.dtype)


def _rms_proj(h2d, g, w, col_scale, out_dtype, tm):
    rows, d = h2d.shape
    n = w.shape[1]
    return pl.pallas_call(
        _rms_proj_kernel,
        grid=(rows // tm,),
        in_specs=[
            pl.BlockSpec((tm, d), lambda i: (i, 0)),
            pl.BlockSpec((1, d), lambda i: (0, 0)),
            pl.BlockSpec((d, n), lambda i: (0, 0)),
            pl.BlockSpec((1, n), lambda i: (0, 0)),
        ],
        out_specs=pl.BlockSpec((tm, n), lambda i: (i, 0)),
        out_shape=jax.ShapeDtypeStruct((rows, n), out_dtype),
        compiler_params=pltpu.CompilerParams(
            dimension_semantics=("arbitrary",), vmem_limit_bytes=VMEM_LIMIT),
        name="rms_proj",
    )(h2d, g, w, col_scale)


def _attn_kernel(qi_ref, wi_ref, kit_ref, tri_ref, q_ref, k_ref, v_ref, o_ref,
                 s_scr, cand_scr, cand_t_scr, bias_scr, wb_scr, thr_scr,
                 need_scr, carry_scr, m_scr, l_scr, acc_scr, flag_scr, *, topk, g_tiles):
    tq = ROW_TILE
    kt = ROW_TILE
    rg = 128
    nh = kt // 128
    iq = pl.program_id(1)
    j = pl.program_id(2)
    nq = pl.num_programs(1) - 1
    nkb = pl.num_programs(2)
    i = iq + 1
    n_tiles = i + 1
    has_scores = iq < nq
    has_attn = iq >= 1
    t0 = j * g_tiles
    n_sc = jnp.where(has_scores, jnp.clip(n_tiles - t0, 0, g_tiles), 0)
    n_at = jnp.where(has_attn, jnp.clip(iq + 1 - t0, 0, g_tiles), 0)
    n_both = jnp.minimum(n_sc, n_at)
    kf = float(topk)

    def lanes_x(v):
        return jnp.concatenate([v] * nh, axis=1)

    def to_dense(rep):
        return jnp.concatenate(
            [rep[g * 128:(g + 1) * 128, :].T[0:8, :] for g in range(tq // 128)], axis=1)

    def to_rows(dense):
        return jnp.concatenate(
            [jnp.broadcast_to(dense[0:1, g * 128:(g + 1) * 128], (128, 128)).T
             for g in range(tq // 128)], axis=0)

    def reduce_scores(vec, fn, init, comb, red):
        vec_rows = to_rows(vec)
        outs = []
        for g in range(tq // rg):
            rows = slice(g * rg, (g + 1) * rg)
            bv = vec_rows[rows]

            def body(t, acc, rows=rows, bv=bv):
                for hl in range(nh):
                    acc = comb(acc, fn(s_scr[t, rows, hl * 128:(hl + 1) * 128], bv))
                return acc

            acc = lax.fori_loop(0, n_tiles, body, jnp.full((rg, 128), init, f32))
            outs.append(jnp.broadcast_to(red(acc, axis=1, keepdims=True), (rg, 128)))
        return to_dense(jnp.concatenate(outs, axis=0))

    def reduce_lists(vec, fn, init, comb, red):
        accs = [jnp.full((8, tq), init, f32) for _ in range(4)]
        for c in range(LANE_LIST * 128 // 8):
            accs[c % 4] = comb(accs[c % 4], fn(cand_t_scr[c * 8:(c + 1) * 8, :], vec))
        acc = comb(comb(accs[0], accs[1]), comb(accs[2], accs[3]))
        return jnp.broadcast_to(red(acc, axis=0, keepdims=True), (8, tq))

    def any_row(mask):
        return jnp.max(jnp.where(mask, 1.0, 0.0)) > 0.0

    @pl.when(jnp.logical_and(j == 0, has_attn))
    def _init_attention():
        m_scr[...] = jnp.full(m_scr.shape, NEG, f32)
        l_scr[...] = jnp.zeros(l_scr.shape, f32)
        acc_scr[...] = jnp.zeros(acc_scr.shape, f32)
        carry_scr[...] = jnp.zeros(carry_scr.shape, f32)

    @pl.when(jnp.logical_and(j == 0, has_scores))
    def _init_scores():
        w = wi_ref[0] * IDX_SCALE
        for h in range(N_IDX_HEADS):
            wb_scr[h] = jnp.broadcast_to(w[:, h:h + 1], (tq, 128))
        cand_scr[...] = jnp.full(cand_scr.shape, -jnp.inf, f32)

    def score_tile(s, causal):
        t = t0 + s
        kt_tile = kit_ref[0, t]
        acc = None
        for h in range(N_IDX_HEADS):
            x = jnp.dot(qi_ref[0, h], kt_tile, preferred_element_type=f32)
            term = jnp.maximum(x, 0.0) * lanes_x(wb_scr[h])
            acc = term if acc is None else acc + term
        kpos = t * kt + lax.broadcasted_iota(jnp.int32, (1, kt), 1)
        acc = acc + jnp.where(kpos >= FRONT_PAD, 0.0, -jnp.inf)
        if causal:
            qpos1 = i * tq + lax.broadcasted_iota(jnp.int32, (tq, 1), 0)
            acc = jnp.where(kpos <= qpos1, acc, -jnp.inf)
        s_scr[t] = acc
        for g in range(tq // 8):
            rows = slice(g * 8, (g + 1) * 8)
            lists = [cand_scr[k, rows, :] for k in range(LANE_LIST)]
            for hl in range(nh):
                x = acc[rows, hl * 128:(hl + 1) * 128]
                for k in range(LANE_LIST):
                    top = jnp.maximum(lists[k], x)
                    x = jnp.minimum(lists[k], x)
                    lists[k] = top
            for k in range(LANE_LIST):
                cand_scr[k, rows, :] = lists[k]

    def attend_mask(s, buf):
        t = t0 + s
        tie_path = flag_scr[0] > 0

        @pl.when(jnp.logical_not(tie_path))
        def _():
            thr = thr_scr[...]
            for hl in range(nh):
                lanes = slice(hl * 128, (hl + 1) * 128)
                bias_scr[buf, :, lanes] = jnp.where(s_scr[t, :, lanes] >= thr, 0.0, NEG)

        @pl.when(tie_path)
        def _():
            sc = s_scr[t]
            thr = lanes_x(thr_scr[...])
            tie = sc == thr
            counts = jnp.dot(jnp.where(tie, 1.0, 0.0).astype(bf16), tri_ref[...],
                             preferred_element_type=f32)
            seen = carry_scr[...]
            keep = (counts[:, :kt] + lanes_x(seen)) <= lanes_x(need_scr[...])
            bias_scr[buf] = jnp.where(sc > thr, 0.0,
                                      jnp.where(jnp.logical_and(tie, keep), 0.0, NEG))
            carry_scr[...] = seen + counts[:, kt:]

    def attend_heads(s, buf):
        ones_cols = jnp.ones((kt, HEAD_DIM), bf16)
        row0 = pl.multiple_of(s * kt, kt)
        for h in range(N_HEADS):
            cols = slice(h * HEAD_DIM, (h + 1) * HEAD_DIM)
            qh = q_ref[0, :, cols]
            kh = k_ref[0, pl.ds(row0, kt), cols]
            vh = v_ref[0, pl.ds(row0, kt), cols]
            lg = lax.dot_general(qh, kh, (((1,), (1,)), ((), ())),
                                 preferred_element_type=f32) + bias_scr[buf]
            m_old = m_scr[h]
            m_new = jnp.maximum(m_old, jnp.max(lg, axis=1, keepdims=True))
            alpha = jnp.exp2(m_old - m_new)
            p = jnp.exp2(lg - lanes_x(m_new))
            v_ext = jnp.concatenate([vh, ones_cols], axis=1)
            pv = jnp.dot(p.astype(bf16), v_ext, preferred_element_type=f32)
            l_scr[h] = alpha * l_scr[h] + pv[:, HEAD_DIM:]
            acc_scr[:, cols] = alpha * acc_scr[:, cols] + pv[:, :HEAD_DIM]
            m_scr[h] = m_new

    def pair_body(p, carry):
        attend_mask(2 * p, 0)
        attend_mask(2 * p + 1, 1)
        attend_heads(2 * p, 0)
        score_tile(2 * p, causal=False)
        attend_heads(2 * p + 1, 1)
        score_tile(2 * p + 1, causal=False)
        return carry

    def both_body(s, carry):
        attend_mask(s, 0)
        attend_heads(s, 0)
        score_tile(s, causal=False)
        return carry

    def attend_body(s, carry):
        attend_mask(s, 0)
        attend_heads(s, 0)
        return carry

    def score_body(s, carry):
        score_tile(s, causal=True)
        return carry

    n_pairs = n_both // 2
    lax.fori_loop(0, n_pairs, pair_body, 0)
    lax.fori_loop(2 * n_pairs, n_both, both_body, 0)
    lax.fori_loop(n_both, n_at, attend_body, 0)
    lax.fori_loop(n_both, n_sc, score_body, 0)

    @pl.when(jnp.logical_and(has_scores, j == (n_tiles - 1) // g_tiles))
    def _threshold():
        qpos = i * tq + lax.broadcasted_iota(jnp.int32, (8, tq), 1)
        n_vis = (qpos - (FRONT_PAD - 1)).astype(f32)
        short = n_vis <= kf

        def search(reduce):
            def count_ge(v):
                return reduce(v, lambda blk, bv: jnp.where(blk >= bv, 1.0, 0.0), 0.0, jnp.add, jnp.sum)

            def count_gt(v):
                return reduce(v, lambda blk, bv: jnp.where(blk > bv, 1.0, 0.0), 0.0, jnp.add, jnp.sum)

            zero = jnp.zeros((8, tq), f32)
            cge0 = count_ge(zero)
            cgt0 = count_gt(zero)
            at_zero = jnp.logical_and(jnp.logical_not(short),
                                      jnp.logical_and(cgt0 < kf, cge0 >= kf))
            res0 = jnp.where(jnp.logical_or(short, at_zero), 1.0, 0.0)
            thr0 = jnp.where(short, BELOW_ALL, zero)
            need0 = jnp.where(jnp.logical_and(at_zero, cge0 > kf), kf - cgt0, KEEP_ALL)
            lo0 = jnp.where(cgt0 >= kf, jnp.maximum(lo_lists, zero), lo_lists)
            hi0 = jnp.where(cge0 < kf, jnp.minimum(hi_lists, zero), hi_lists)

            def bis_cond(st):
                it, _, _, _, _, active = st
                return jnp.logical_and(it < BISECT_CAP, jnp.max(active) > 0.0)

            def bis_body(st):
                it, lo, hi, thr, resolved, active = st
                piv = lo + (hi - lo) * 0.5
                splits = jnp.logical_and(piv > lo, piv < hi)
                c = count_ge(piv)
                upd = jnp.logical_and(active > 0.0, splits)
                hit = jnp.logical_and(upd, c == kf)
                thr = jnp.where(hit, piv, thr)
                resolved = jnp.where(hit, 1.0, resolved)
                lo = jnp.where(jnp.logical_and(upd, c > kf), piv, lo)
                hi = jnp.where(jnp.logical_and(upd, c < kf), piv, hi)
                active = jnp.where(jnp.logical_and(upd, jnp.logical_not(hit)), 1.0, 0.0)
                return it + 1, lo, hi, thr, resolved, active

            st = (jnp.int32(0), lo0, hi0, thr0, res0, 1.0 - res0)
            _, lo, hi, thr, resolved, _ = lax.while_loop(bis_cond, bis_body, st)

            thr_scr[...] = to_rows(thr)
            need_scr[...] = to_rows(need0)
            flag_scr[0] = jnp.where(any_row(need0 < KEEP_ALL), 1, 0).astype(jnp.int32)
            unresolved = resolved < 0.5

            @pl.when(any_row(unresolved))
            def _fallback():
                v0 = reduce(hi, lambda blk, bv: jnp.where(blk <= bv, blk, -jnp.inf),
                            -jnp.inf, jnp.maximum, jnp.max)

                def fb_cond(st):
                    _, cge, _ = st
                    return any_row(jnp.logical_and(unresolved, cge < kf))

                def fb_body(st):
                    v, cge, cgt = st
                    more = jnp.logical_and(unresolved, cge < kf)
                    v2 = reduce(v, lambda blk, bv: jnp.where(blk < bv, blk, -jnp.inf),
                                -jnp.inf, jnp.maximum, jnp.max)
                    return (jnp.where(more, v2, v), jnp.where(more, count_ge(v2), cge),
                            jnp.where(more, count_gt(v2), cgt))

                v, cge, cgt = lax.while_loop(fb_cond, fb_body, (v0, count_ge(v0), count_gt(v0)))
                need = kf - cgt
                partial = jnp.logical_and(unresolved, (cge - cgt) > need)
                thr_scr[...] = to_rows(jnp.where(unresolved, v, thr))
                need_scr[...] = to_rows(jnp.where(partial, need, need0))

                @pl.when(any_row(partial))
                def _():
                    flag_scr[0] = jnp.int32(1)

        for k in range(LANE_LIST):
            for g in range(tq // 128):
                cand_t_scr[k * 128:(k + 1) * 128, g * 128:(g + 1) * 128] = (
                    cand_scr[k, g * 128:(g + 1) * 128, :].T)

        second = cand_t_scr[128:256, :]
        lo_lists = jnp.broadcast_to(jnp.min(second, axis=0, keepdims=True), (8, tq))
        hi_lists = jnp.broadcast_to(jnp.max(second, axis=0, keepdims=True), (8, tq))

        search(reduce_lists)

        @pl.when(any_row(cand_scr[LANE_LIST - 1] >= thr_scr[...]))
        def _():
            search(reduce_scores)

    @pl.when(jnp.logical_and(has_attn, j == nkb - 1))
    def _finish():
        for h in range(N_HEADS):
            cols = slice(h * HEAD_DIM, (h + 1) * HEAD_DIM)
            o_ref[0, :, cols] = (acc_scr[:, cols] / l_scr[h]).astype(o_ref.dtype)


def _attention(qi4, wi, kit, qkv, *, seq, lp, topk):
    b = qkv.shape[0]
    tq = ROW_TILE
    nt = lp // tq
    g_tiles = max(g for g in range(1, MAX_KV_TILES_PER_STEP + 1) if nt % g == 0)
    nkb = nt // g_tiles
    nq = seq // tq

    def kv_block(iq, j):
        return jnp.minimum(j, iq // g_tiles)

    def score_tile_idx(iq):
        return jnp.minimum(iq + 1, nq)

    ka = lax.broadcasted_iota(jnp.int32, (tq, tq + 128), 0)
    kb = lax.broadcasted_iota(jnp.int32, (tq, tq + 128), 1)
    tri = jnp.where(jnp.logical_or(ka <= kb, kb >= tq), 1.0, 0.0).astype(bf16)

    kern = functools.partial(_attn_kernel, topk=topk, g_tiles=g_tiles)
    return pl.pallas_call(
        kern,
        grid=(b, nq + 1, nkb),
        in_specs=[
            pl.BlockSpec((1, N_IDX_HEADS, tq, IDX_DIM),
                         lambda bb, iq, j: (bb, 0, score_tile_idx(iq), 0)),
            pl.BlockSpec((1, tq, N_IDX_HEADS), lambda bb, iq, j: (bb, score_tile_idx(iq), 0)),
            pl.BlockSpec((1, nt, IDX_DIM, tq), lambda bb, iq, j: (bb, 0, 0, 0),
                         pipeline_mode=pl.Buffered(1)),
            pl.BlockSpec((tq, tq + 128), lambda bb, iq, j: (0, 0)),
            pl.BlockSpec((1, tq, ATTN_WIDTH), lambda bb, iq, j: (bb, iq, 0)),
            pl.BlockSpec((1, g_tiles * tq, ATTN_WIDTH), lambda bb, iq, j: (bb, kv_block(iq, j), 1)),
            pl.BlockSpec((1, g_tiles * tq, ATTN_WIDTH), lambda bb, iq, j: (bb, kv_block(iq, j), 2)),
        ],
        out_specs=pl.BlockSpec((1, tq, ATTN_WIDTH), lambda bb, iq, j: (bb, jnp.maximum(iq - 1, 0), 0)),
        out_shape=jax.ShapeDtypeStruct((b, seq, ATTN_WIDTH), bf16),
        scratch_shapes=[
            pltpu.VMEM((nt, tq, tq), f32),
            pltpu.VMEM((LANE_LIST, tq, 128), f32),
            pltpu.VMEM((LANE_LIST * 128, tq), f32),
            pltpu.VMEM((2, tq, tq), f32),
            pltpu.VMEM((N_IDX_HEADS, tq, 128), f32),
            pltpu.VMEM((tq, 128), f32),
            pltpu.VMEM((tq, 128), f32),
            pltpu.VMEM((tq, 128), f32),
            pltpu.VMEM((N_HEADS, tq, 128), f32),
            pltpu.VMEM((N_HEADS, tq, 128), f32),
            pltpu.VMEM((tq, ATTN_WIDTH), f32),
            pltpu.SMEM((1,), jnp.int32),
        ],
        compiler_params=pltpu.CompilerParams(
            dimension_semantics=("arbitrary", "arbitrary", "arbitrary"),
            vmem_limit_bytes=VMEM_LIMIT),
        name="dsa_attention",
    )(qi4, wi, kit, tri, qkv, qkv, qkv)


def _mix_kernel(x_ref, ya_ref, cv_ref, cvp_ref, gt_ref, cw_ref, wao_ref, wco_ref, wout_ref,
                o_ref, u_scr):
    tm = x_ref.shape[1]
    c = cw_ref.shape[1]
    hr = cvp_ref.shape[1]
    cu = cv_ref[0, :, 0:c].astype(f32)
    cb = cv_ref[0, :, c:2 * c].astype(f32)
    cc = cv_ref[0, :, 2 * c:3 * c].astype(f32)
    u_scr[0:hr, :] = cvp_ref[0, :, 2 * c:3 * c].astype(f32) * cvp_ref[0, :, 0:c].astype(f32)
    u_scr[hr:hr + tm, :] = cc * cu
    w = cw_ref[...]
    conv = (w[2:3] * u_scr[hr:hr + tm, :] + w[1:2] * u_scr[hr - 1:hr - 1 + tm, :]
            + w[0:1] * u_scr[hr - 2:hr - 2 + tm, :])
    y_conv = jnp.dot((cb * conv).astype(bf16), wco_ref[...], preferred_element_type=f32)
    y_attn = jnp.dot(ya_ref[0], wao_ref[...], preferred_element_type=f32)
    d = wout_ref.shape[0]
    mixed = (_sigmoid(gt_ref[0, :, 0:d].astype(f32)) * y_attn
             + _sigmoid(gt_ref[0, :, d:2 * d].astype(f32)) * y_conv)
    o_ref[0] = x_ref[0] + jnp.dot(mixed.astype(bf16), wout_ref[...], preferred_element_type=f32)


def _mix(x, y_attn, conv3, gates3, conv_w, wao, wco, wout):
    b, seq, d = x.shape
    tm = ROW_TILE
    c = conv_w.shape[1]
    const = lambda bb, i: (0, 0)
    return pl.pallas_call(
        _mix_kernel,
        grid=(b, seq // tm),
        in_specs=[
            pl.BlockSpec((1, tm, d), lambda bb, i: (bb, i, 0)),
            pl.BlockSpec((1, tm, ATTN_WIDTH), lambda bb, i: (bb, i, 0)),
            pl.BlockSpec((1, tm, 3 * c), lambda bb, i: (bb, i + 1, 0)),
            pl.BlockSpec((1, HALO_ROWS, 3 * c),
                         lambda bb, i: (bb, (i + 1) * (tm // HALO_ROWS) - 1, 0)),
            pl.BlockSpec((1, tm, 2 * d), lambda bb, i: (bb, i + 1, 0)),
            pl.BlockSpec((CONV_K, c), const),
            pl.BlockSpec(wao.shape, const),
            pl.BlockSpec(wco.shape, const),
            pl.BlockSpec(wout.shape, const),
        ],
        out_specs=pl.BlockSpec((1, tm, d), lambda bb, i: (bb, i, 0)),
        out_shape=jax.ShapeDtypeStruct((b, seq, d), f32),
        scratch_shapes=[pltpu.VMEM((tm + HALO_ROWS, c), f32)],
        compiler_params=pltpu.CompilerParams(
            dimension_semantics=("arbitrary", "arbitrary"), vmem_limit_bytes=VMEM_LIMIT),
        name="mix_merge",
    )(x, y_attn, conv3, conv3, gates3, conv_w, wao, wco, wout)


def _ffn_kernel(h_ref, g1_ref, wg_ref, wu_ref, wd_ref, g2_ref, o_ref):
    h = h_ref[0]
    f = _rms(h, g1_ref[...]).astype(bf16)
    gate = jnp.dot(f, wg_ref[...], preferred_element_type=f32)
    up = jnp.dot(f, wu_ref[...], preferred_element_type=f32)
    act = (gate * _sigmoid(gate)) * up
    h2 = h + jnp.dot(act.astype(bf16), wd_ref[...], preferred_element_type=f32)
    o_ref[0] = _rms(h2, g2_ref[...])


def _ffn(h1, g1, wg, wu, wd, g2):
    b, seq, d = h1.shape
    tm = ROW_TILE
    const = lambda bb, i: (0, 0)
    return pl.pallas_call(
        _ffn_kernel,
        grid=(b, seq // tm),
        in_specs=[
            pl.BlockSpec((1, tm, d), lambda bb, i: (bb, i, 0)),
            pl.BlockSpec((1, d), const),
            pl.BlockSpec(wg.shape, const),
            pl.BlockSpec(wu.shape, const),
            pl.BlockSpec(wd.shape, const),
            pl.BlockSpec((1, d), const),
        ],
        out_specs=pl.BlockSpec((1, tm, d), lambda bb, i: (bb, i, 0)),
        out_shape=jax.ShapeDtypeStruct((b, seq, d), f32),
        compiler_params=pltpu.CompilerParams(
            dimension_semantics=("arbitrary", "arbitrary"), vmem_limit_bytes=VMEM_LIMIT),
        name="ffn_final",
    )(h1, g1, wg, wu, wd, g2)


def kernel(x, meta_tokens, norm_mix_g, w_in, w_attn_out, conv_w, w_conv_out, w_out,
           norm_ffn_g, w_gate, w_up, w_down, norm_final_g):
    b, seq, d = x.shape
    assert w_in.shape[0] == 1, "single-layer block"
    assert seq % ROW_TILE == 0 and meta_tokens.shape[0] == N_META
    c = conv_w.shape[2]
    lp = ROW_TILE + seq
    l_real = N_META + seq
    topk = min(TOPK_MAX, l_real // 4)
    nt = lp // ROW_TILE

    meta = jnp.broadcast_to(meta_tokens[None].astype(x.dtype), (b, N_META, d))
    hp = jnp.concatenate([jnp.zeros((b, FRONT_PAD, d), x.dtype), meta, x], axis=1)
    h2d = hp.reshape(b * lp, d)

    w = w_in[0]
    o_idx = 3 * ATTN_WIDTH
    n_idx = N_IDX_HEADS * IDX_DIM + IDX_DIM + N_IDX_HEADS
    o_conv = o_idx + n_idx
    o_gate = o_conv + 3 * c
    n_idx_pad = -(-n_idx // 128) * 128
    w_qkv = w[:, :o_idx].astype(bf16)
    w_idx = jnp.pad(w[:, o_idx:o_conv], ((0, 0), (0, n_idx_pad - n_idx))).astype(bf16)
    w_conv = w[:, o_conv:o_gate].astype(bf16)
    w_gates = w[:, o_gate:].astype(bf16)
    g_mix = norm_mix_g[0][None]

    tm = 512 if (b * lp) % 512 == 0 else ROW_TILE
    qkv_scale = jnp.concatenate([jnp.full((1, ATTN_WIDTH), LOG2E * HEAD_DIM ** -0.5, f32),
                                 jnp.ones((1, 2 * ATTN_WIDTH), f32)], axis=1)
    qkv = _rms_proj(h2d, g_mix, w_qkv, qkv_scale, bf16, tm).reshape(b, lp, 3 * ATTN_WIDTH)
    idx = _rms_proj(h2d, g_mix, w_idx, jnp.ones((1, n_idx_pad), f32), f32, tm)
    conv3 = _rms_proj(h2d, g_mix, w_conv, jnp.ones((1, 3 * c), f32), bf16, tm).reshape(b, lp, 3 * c)
    gates3 = _rms_proj(h2d, g_mix, w_gates, jnp.ones((1, 2 * d), f32), bf16, tm).reshape(b, lp, 2 * d)

    nqi = N_IDX_HEADS * IDX_DIM
    qi4 = idx[:, :nqi].astype(bf16).reshape(b, lp, N_IDX_HEADS, IDX_DIM).transpose(0, 2, 1, 3)
    kit = idx[:, nqi:nqi + IDX_DIM].astype(bf16).reshape(b, nt, ROW_TILE, IDX_DIM).transpose(0, 1, 3, 2)
    wi = idx[:, nqi + IDX_DIM:n_idx].reshape(b, lp, N_IDX_HEADS)

    y_attn = _attention(qi4, wi, kit, qkv, seq=seq, lp=lp, topk=topk)

    h1 = _mix(x, y_attn, conv3, gates3, conv_w[0], w_attn_out[0].astype(bf16),
              w_conv_out[0].astype(bf16), w_out[0].astype(bf16))
    return _ffn(h1, norm_ffn_g[0][None], w_gate[0].astype(bf16), w_up[0].astype(bf16),
                w_down[0].astype(bf16), norm_final_g[None])
```

```python
import functools

import jax
import jax.numpy as jnp
from jax import lax
from jax.experimental import pallas as pl
from jax.experimental.pallas import tpu as pltpu

N_META = 16
N_HEADS = 8
HEAD_DIM = 128
ATTN_WIDTH = N_HEADS * HEAD_DIM
N_IDX_HEADS = 8
IDX_DIM = 64
TOPK_MAX = 256
CONV_K = 3
EPS = 1e-6
IDX_SCALE = (N_IDX_HEADS ** -0.5) * (IDX_DIM ** -0.5)

ROW_TILE = 256
FRONT_PAD = ROW_TILE - N_META
MAX_KV_TILES_PER_STEP = 13
BISECT_CAP = 32
NEG = -1e30
KEEP_ALL = 1e9
BELOW_ALL = -3e38
HALO_ROWS = 16
LANE_LIST = 12
LOG2E = 1.4426950408889634
VMEM_LIMIT = 58 * 1024 * 1024

f32 = jnp.float32
bf16 = jnp.bfloat16


def _rms(x, g):
    return (x * lax.rsqrt(jnp.mean(x * x, axis=-1, keepdims=True) + EPS)) * g


def _sigmoid(x):
    return 1.0 / (1.0 + jnp.exp(-x))


def _rms_proj_kernel(h_ref, g_ref, w_ref, cs_ref, o_ref):
    a = _rms(h_ref[...], g_ref[...]).astype(bf16)
    y = jnp.dot(a, w_ref[...], preferred_element_type=f32)
    o_ref[...] = (y * cs_ref[...]).astype(o_ref.dtype)


def _rms_proj(h2d, g, w, col_scale, out_dtype, tm):
    rows, d = h2d.shape
    n = w.shape[1]
    return pl.pallas_call(
        _rms_proj_kernel,
        grid=(rows // tm,),
        in_specs=[
            pl.BlockSpec((tm, d), lambda i: (i, 0)),
            pl.BlockSpec((1, d), lambda i: (0, 0)),
            pl.BlockSpec((d, n), lambda i: (0, 0)),
            pl.BlockSpec((1, n), lambda i: (0, 0)),
        ],
        out_specs=pl.BlockSpec((tm, n), lambda i: (i, 0)),
        out_shape=jax.ShapeDtypeStruct((rows, n), out_dtype),
        compiler_params=pltpu.CompilerParams(
            dimension_semantics=("arbitrary",), vmem_limit_bytes=VMEM_LIMIT),
        name="rms_proj",
    )(h2d, g, w, col_scale)


def _attn_kernel(qi_ref, wi_ref, kit_ref, tri_ref, q_ref, k_ref, v_ref, o_ref,
                 s_scr, cand_scr, cand_t_scr, bias_scr, wb_scr, thr_scr,
                 need_scr, carry_scr, m_scr, l_scr, acc_scr, *, topk, g_tiles):
    tq = ROW_TILE
    kt = ROW_TILE
    rg = 128
    nh = kt // 128
    iq = pl.program_id(1)
    j = pl.program_id(2)
    nq = pl.num_programs(1) - 1
    nkb = pl.num_programs(2)
    i = iq + 1
    n_tiles = i + 1
    has_scores = iq < nq
    has_attn = iq >= 1
    t0 = j * g_tiles
    n_sc = jnp.where(has_scores, jnp.clip(n_tiles - t0, 0, g_tiles), 0)
    n_at = jnp.where(has_attn, jnp.clip(iq + 1 - t0, 0, g_tiles), 0)
    n_both = jnp.minimum(n_sc, n_at)
    kf = float(topk)

    def lanes_x(v):
        return jnp.concatenate([v] * nh, axis=1)

    def to_dense(rep):
        return jnp.concatenate(
            [rep[g * 128:(g + 1) * 128, :].T[0:8, :] for g in range(tq // 128)], axis=1)

    def to_rows(dense):
        return jnp.concatenate(
            [jnp.broadcast_to(dense[0:1, g * 128:(g + 1) * 128], (128, 128)).T
             for g in range(tq // 128)], axis=0)

    def reduce_scores(vec, fn, init, comb, red):
        vec_rows = to_rows(vec)
        outs = []
        for g in range(tq // rg):
            rows = slice(g * rg, (g + 1) * rg)
            bv = vec_rows[rows]

            def body(t, acc, rows=rows, bv=bv):
                for hl in range(nh):
                    acc = comb(acc, fn(s_scr[t, rows, hl * 128:(hl + 1) * 128], bv))
                return acc

            acc = lax.fori_loop(0, n_tiles, body, jnp.full((rg, 128), init, f32))
            outs.append(jnp.broadcast_to(red(acc, axis=1, keepdims=True), (rg, 128)))
        return to_dense(jnp.concatenate(outs, axis=0))

    def reduce_lists(vec, fn, init, comb, red):
        accs = [jnp.full((8, tq), init, f32) for _ in range(4)]
        for c in range(LANE_LIST * 128 // 8):
            accs[c % 4] = comb(accs[c % 4], fn(cand_t_scr[c * 8:(c + 1) * 8, :], vec))
        acc = comb(comb(accs[0], accs[1]), comb(accs[2], accs[3]))
        return jnp.broadcast_to(red(acc, axis=0, keepdims=True), (8, tq))

    def any_row(mask):
        return jnp.max(jnp.where(mask, 1.0, 0.0)) > 0.0

    @pl.when(jnp.logical_and(j == 0, has_attn))
    def _init_attention():
        m_scr[...] = jnp.full(m_scr.shape, NEG, f32)
        l_scr[...] = jnp.zeros(l_scr.shape, f32)
        acc_scr[...] = jnp.zeros(acc_scr.shape, f32)
        carry_scr[...] = jnp.zeros(carry_scr.shape, f32)

    @pl.when(jnp.logical_and(j == 0, has_scores))
    def _init_scores():
        w = wi_ref[0] * IDX_SCALE
        for h in range(N_IDX_HEADS):
            wb_scr[h] = jnp.broadcast_to(w[:, h:h + 1], (tq, 128))
        cand_scr[...] = jnp.full(cand_scr.shape, -jnp.inf, f32)

    def score_tile(s, causal):
        t = t0 + s
        kt_tile = kit_ref[0, t]
        acc = None
        for h in range(N_IDX_HEADS):
            x = jnp.dot(qi_ref[0, h], kt_tile, preferred_element_type=f32)
            term = jnp.maximum(x, 0.0) * lanes_x(wb_scr[h])
            acc = term if acc is None else acc + term
        kpos = t * kt + lax.broadcasted_iota(jnp.int32, (1, kt), 1)
        acc = acc + jnp.where(kpos >= FRONT_PAD, 0.0, -jnp.inf)
        if causal:
            qpos1 = i * tq + lax.broadcasted_iota(jnp.int32, (tq, 1), 0)
            acc = jnp.where(kpos <= qpos1, acc, -jnp.inf)
        s_scr[t] = acc
        for g in range(tq // 8):
            rows = slice(g * 8, (g + 1) * 8)
            lists = [cand_scr[k, rows, :] for k in range(LANE_LIST)]
            for hl in range(nh):
                x = acc[rows, hl * 128:(hl + 1) * 128]
                for k in range(LANE_LIST):
                    top = jnp.maximum(lists[k], x)
                    x = jnp.minimum(lists[k], x)
                    lists[k] = top
            for k in range(LANE_LIST):
                cand_scr[k, rows, :] = lists[k]

    def attend_mask(s, buf):
        t = t0 + s
        sc = s_scr[t]
        thr = lanes_x(thr_scr[...])
        tie = sc == thr
        counts = jnp.dot(jnp.where(tie, 1.0, 0.0).astype(bf16), tri_ref[...],
                         preferred_element_type=f32)
        seen = carry_scr[...]
        keep = (counts[:, :kt] + lanes_x(seen)) <= lanes_x(need_scr[...])
        bias_scr[buf] = jnp.where(sc > thr, 0.0,
                                  jnp.where(jnp.logical_and(tie, keep), 0.0, NEG))
        carry_scr[...] = seen + counts[:, kt:]

    def attend_heads(s, buf):
        ones_cols = jnp.ones((kt, HEAD_DIM), bf16)
        row0 = pl.multiple_of(s * kt, kt)
        for h in range(N_HEADS):
            cols = slice(h * HEAD_DIM, (h + 1) * HEAD_DIM)
            qh = q_ref[0, :, cols]
            kh = k_ref[0, pl.ds(row0, kt), cols]
            vh = v_ref[0, pl.ds(row0, kt), cols]
            lg = lax.dot_general(qh, kh, (((1,), (1,)), ((), ())),
                                 preferred_element_type=f32) + bias_scr[buf]
            m_old = m_scr[h]
            m_new = jnp.maximum(m_old, jnp.max(lg, axis=1, keepdims=True))
            alpha = jnp.exp2(m_old - m_new)
            p = jnp.exp2(lg - lanes_x(m_new))
            v_ext = jnp.concatenate([vh, ones_cols], axis=1)
            pv = jnp.dot(p.astype(bf16), v_ext, preferred_element_type=f32)
            l_scr[h] = alpha * l_scr[h] + pv[:, HEAD_DIM:]
            acc_scr[:, cols] = alpha * acc_scr[:, cols] + pv[:, :HEAD_DIM]
            m_scr[h] = m_new

    def pair_body(p, carry):
        attend_mask(2 * p, 0)
        attend_mask(2 * p + 1, 1)
        attend_heads(2 * p, 0)
        score_tile(2 * p, causal=False)
        attend_heads(2 * p + 1, 1)
        score_tile(2 * p + 1, causal=False)
        return carry

    def both_body(s, carry):
        attend_mask(s, 0)
        attend_heads(s, 0)
        score_tile(s, causal=False)
        return carry

    def attend_body(s, carry):
        attend_mask(s, 0)
        attend_heads(s, 0)
        return carry

    def score_body(s, carry):
        score_tile(s, causal=True)
        return carry

    n_pairs = n_both // 2
    lax.fori_loop(0, n_pairs, pair_body, 0)
    lax.fori_loop(2 * n_pairs, n_both, both_body, 0)
    lax.fori_loop(n_both, n_at, attend_body, 0)
    lax.fori_loop(n_both, n_sc, score_body, 0)

    @pl.when(jnp.logical_and(has_scores, j == (n_tiles - 1) // g_tiles))
    def _threshold():
        qpos = i * tq + lax.broadcasted_iota(jnp.int32, (8, tq), 1)
        n_vis = (qpos - (FRONT_PAD - 1)).astype(f32)
        short = n_vis <= kf

        def search(reduce):
            def count_ge(v):
                return reduce(v, lambda blk, bv: jnp.where(blk >= bv, 1.0, 0.0), 0.0, jnp.add, jnp.sum)

            def count_gt(v):
                return reduce(v, lambda blk, bv: jnp.where(blk > bv, 1.0, 0.0), 0.0, jnp.add, jnp.sum)

            zero = jnp.zeros((8, tq), f32)
            cge0 = count_ge(zero)
            cgt0 = count_gt(zero)
            at_zero = jnp.logical_and(jnp.logical_not(short),
                                      jnp.logical_and(cgt0 < kf, cge0 >= kf))
            res0 = jnp.where(jnp.logical_or(short, at_zero), 1.0, 0.0)
            thr0 = jnp.where(short, BELOW_ALL, zero)
            need0 = jnp.where(jnp.logical_and(at_zero, cge0 > kf), kf - cgt0, KEEP_ALL)
            lo0 = jnp.where(cgt0 >= kf, jnp.maximum(lo_lists, zero), lo_lists)
            hi0 = jnp.where(cge0 < kf, jnp.minimum(hi_lists, zero), hi_lists)

            def bis_cond(st):
                it, _, _, _, _, active = st
                return jnp.logical_and(it < BISECT_CAP, jnp.max(active) > 0.0)

            def bis_body(st):
                it, lo, hi, thr, resolved, active = st
                piv = lo + (hi - lo) * 0.5
                splits = jnp.logical_and(piv > lo, piv < hi)
                c = count_ge(piv)
                upd = jnp.logical_and(active > 0.0, splits)
                hit = jnp.logical_and(upd, c == kf)
                thr = jnp.where(hit, piv, thr)
                resolved = jnp.where(hit, 1.0, resolved)
                lo = jnp.where(jnp.logical_and(upd, c > kf), piv, lo)
                hi = jnp.where(jnp.logical_and(upd, c < kf), piv, hi)
                active = jnp.where(jnp.logical_and(upd, jnp.logical_not(hit)), 1.0, 0.0)
                return it + 1, lo, hi, thr, resolved, active

            st = (jnp.int32(0), lo0, hi0, thr0, res0, 1.0 - res0)
            _, lo, hi, thr, resolved, _ = lax.while_loop(bis_cond, bis_body, st)

            thr_scr[...] = to_rows(thr)
            need_scr[...] = to_rows(need0)
            unresolved = resolved < 0.5

            @pl.when(any_row(unresolved))
            def _fallback():
                v0 = reduce(hi, lambda blk, bv: jnp.where(blk <= bv, blk, -jnp.inf),
                            -jnp.inf, jnp.maximum, jnp.max)

                def fb_cond(st):
                    _, cge, _ = st
                    return any_row(jnp.logical_and(unresolved, cge < kf))

                def fb_body(st):
                    v, cge, cgt = st
                    more = jnp.logical_and(unresolved, cge < kf)
                    v2 = reduce(v, lambda blk, bv: jnp.where(blk < bv, blk, -jnp.inf),
                                -jnp.inf, jnp.maximum, jnp.max)
                    return (jnp.where(more, v2, v), jnp.where(more, count_ge(v2), cge),
                            jnp.where(more, count_gt(v2), cgt))

                v, cge, cgt = lax.while_loop(fb_cond, fb_body, (v0, count_ge(v0), count_gt(v0)))
                need = kf - cgt
                partial = jnp.logical_and(unresolved, (cge - cgt) > need)
                thr_scr[...] = to_rows(jnp.where(unresolved, v, thr))
                need_scr[...] = to_rows(jnp.where(partial, need, need0))

        for k in range(LANE_LIST):
            for g in range(tq // 128):
                cand_t_scr[k * 128:(k + 1) * 128, g * 128:(g + 1) * 128] = (
                    cand_scr[k, g * 128:(g + 1) * 128, :].T)

        second = cand_t_scr[128:256, :]
        lo_lists = jnp.broadcast_to(jnp.min(second, axis=0, keepdims=True), (8, tq))
        hi_lists = jnp.broadcast_to(jnp.max(second, axis=0, keepdims=True), (8, tq))

        search(reduce_lists)

        @pl.when(any_row(cand_scr[LANE_LIST - 1] >= thr_scr[...]))
        def _():
            search(reduce_scores)

    @pl.when(jnp.logical_and(has_attn, j == nkb - 1))
    def _finish():
        for h in range(N_HEADS):
            cols = slice(h * HEAD_DIM, (h + 1) * HEAD_DIM)
            o_ref[0, :, cols] = (acc_scr[:, cols] / l_scr[h]).astype(o_ref.dtype)


def _attention(qi4, wi, kit, qkv, *, seq, lp, topk):
    b = qkv.shape[0]
    tq = ROW_TILE
    nt = lp // tq
    g_tiles = max(g for g in range(1, MAX_KV_TILES_PER_STEP + 1) if nt % g == 0)
    nkb = nt // g_tiles
    nq = seq // tq

    def kv_block(iq, j):
        return jnp.minimum(j, iq // g_tiles)

    def score_tile_idx(iq):
        return jnp.minimum(iq + 1, nq)

    ka = lax.broadcasted_iota(jnp.int32, (tq, tq + 128), 0)
    kb = lax.broadcasted_iota(jnp.int32, (tq, tq + 128), 1)
    tri = jnp.where(jnp.logical_or(ka <= kb, kb >= tq), 1.0, 0.0).astype(bf16)

    kern = functools.partial(_attn_kernel, topk=topk, g_tiles=g_tiles)
    return pl.pallas_call(
        kern,
        grid=(b, nq + 1, nkb),
        in_specs=[
            pl.BlockSpec((1, N_IDX_HEADS, tq, IDX_DIM),
                         lambda bb, iq, j: (bb, 0, score_tile_idx(iq), 0)),
            pl.BlockSpec((1, tq, N_IDX_HEADS), lambda bb, iq, j: (bb, score_tile_idx(iq), 0)),
            pl.BlockSpec((1, nt, IDX_DIM, tq), lambda bb, iq, j: (bb, 0, 0, 0),
                         pipeline_mode=pl.Buffered(1)),
            pl.BlockSpec((tq, tq + 128), lambda bb, iq, j: (0, 0)),
            pl.BlockSpec((1, tq, ATTN_WIDTH), lambda bb, iq, j: (bb, iq, 0)),
            pl.BlockSpec((1, g_tiles * tq, ATTN_WIDTH), lambda bb, iq, j: (bb, kv_block(iq, j), 1)),
            pl.BlockSpec((1, g_tiles * tq, ATTN_WIDTH), lambda bb, iq, j: (bb, kv_block(iq, j), 2)),
        ],
        out_specs=pl.BlockSpec((1, tq, ATTN_WIDTH), lambda bb, iq, j: (bb, jnp.maximum(iq - 1, 0), 0)),
        out_shape=jax.ShapeDtypeStruct((b, seq, ATTN_WIDTH), bf16),
        scratch_shapes=[
            pltpu.VMEM((nt, tq, tq), f32),
            pltpu.VMEM((LANE_LIST, tq, 128), f32),
            pltpu.VMEM((LANE_LIST * 128, tq), f32),
            pltpu.VMEM((2, tq, tq), f32),
            pltpu.VMEM((N_IDX_HEADS, tq, 128), f32),
            pltpu.VMEM((tq, 128), f32),
            pltpu.VMEM((tq, 128), f32),
            pltpu.VMEM((tq, 128), f32),
            pltpu.VMEM((N_HEADS, tq, 128), f32),
            pltpu.VMEM((N_HEADS, tq, 128), f32),
            pltpu.VMEM((tq, ATTN_WIDTH), f32),
        ],
        compiler_params=pltpu.CompilerParams(
            dimension_semantics=("arbitrary", "arbitrary", "arbitrary"),
            vmem_limit_bytes=VMEM_LIMIT),
        name="dsa_attention",
    )(qi4, wi, kit, tri, qkv, qkv, qkv)


def _mix_kernel(x_ref, ya_ref, cv_ref, cvp_ref, gt_ref, cw_ref, wao_ref, wco_ref, wout_ref,
                o_ref, u_scr):
    tm = x_ref.shape[1]
    c = cw_ref.shape[1]
    hr = cvp_ref.shape[1]
    cu = cv_ref[0, :, 0:c].astype(f32)
    cb = cv_ref[0, :, c:2 * c].astype(f32)
    cc = cv_ref[0, :, 2 * c:3 * c].astype(f32)
    u_scr[0:hr, :] = cvp_ref[0, :, 2 * c:3 * c].astype(f32) * cvp_ref[0, :, 0:c].astype(f32)
    u_scr[hr:hr + tm, :] = cc * cu
    w = cw_ref[...]
    conv = (w[2:3] * u_scr[hr:hr + tm, :] + w[1:2] * u_scr[hr - 1:hr - 1 + tm, :]
            + w[0:1] * u_scr[hr - 2:hr - 2 + tm, :])
    y_conv = jnp.dot((cb * conv).astype(bf16), wco_ref[...], preferred_element_type=f32)
    y_attn = jnp.dot(ya_ref[0], wao_ref[...], preferred_element_type=f32)
    d = wout_ref.shape[0]
    mixed = (_sigmoid(gt_ref[0, :, 0:d].astype(f32)) * y_attn
             + _sigmoid(gt_ref[0, :, d:2 * d].astype(f32)) * y_conv)
    o_ref[0] = x_ref[0] + jnp.dot(mixed.astype(bf16), wout_ref[...], preferred_element_type=f32)


def _mix(x, y_attn, conv3, gates3, conv_w, wao, wco, wout):
    b, seq, d = x.shape
    tm = ROW_TILE
    c = conv_w.shape[1]
    const = lambda bb, i: (0, 0)
    return pl.pallas_call(
        _mix_kernel,
        grid=(b, seq // tm),
        in_specs=[
            pl.BlockSpec((1, tm, d), lambda bb, i: (bb, i, 0)),
            pl.BlockSpec((1, tm, ATTN_WIDTH), lambda bb, i: (bb, i, 0)),
            pl.BlockSpec((1, tm, 3 * c), lambda bb, i: (bb, i + 1, 0)),
            pl.BlockSpec((1, HALO_ROWS, 3 * c),
                         lambda bb, i: (bb, (i + 1) * (tm // HALO_ROWS) - 1, 0)),
            pl.BlockSpec((1, tm, 2 * d), lambda bb, i: (bb, i + 1, 0)),
            pl.BlockSpec((CONV_K, c), const),
            pl.BlockSpec(wao.shape, const),
            pl.BlockSpec(wco.shape, const),
            pl.BlockSpec(wout.shape, const),
        ],
        out_specs=pl.BlockSpec((1, tm, d), lambda bb, i: (bb, i, 0)),
        out_shape=jax.ShapeDtypeStruct((b, seq, d), f32),
        scratch_shapes=[pltpu.VMEM((tm + HALO_ROWS, c), f32)],
        compiler_params=pltpu.CompilerParams(
            dimension_semantics=("arbitrary", "arbitrary"), vmem_limit_bytes=VMEM_LIMIT),
        name="mix_merge",
    )(x, y_attn, conv3, conv3, gates3, conv_w, wao, wco, wout)


def _ffn_kernel(h_ref, g1_ref, wg_ref, wu_ref, wd_ref, g2_ref, o_ref):
    h = h_ref[0]
    f = _rms(h, g1_ref[...]).astype(bf16)
    gate = jnp.dot(f, wg_ref[...], preferred_element_type=f32)
    up = jnp.dot(f, wu_ref[...], preferred_element_type=f32)
    act = (gate * _sigmoid(gate)) * up
    h2 = h + jnp.dot(act.astype(bf16), wd_ref[...], preferred_element_type=f32)
    o_ref[0] = _rms(h2, g2_ref[...])


def _ffn(h1, g1, wg, wu, wd, g2):
    b, seq, d = h1.shape
    tm = ROW_TILE
    const = lambda bb, i: (0, 0)
    return pl.pallas_call(
        _ffn_kernel,
        grid=(b, seq // tm),
        in_specs=[
            pl.BlockSpec((1, tm, d), lambda bb, i: (bb, i, 0)),
            pl.BlockSpec((1, d), const),
            pl.BlockSpec(wg.shape, const),
            pl.BlockSpec(wu.shape, const),
            pl.BlockSpec(wd.shape, const),
            pl.BlockSpec((1, d), const),
        ],
        out_specs=pl.BlockSpec((1, tm, d), lambda bb, i: (bb, i, 0)),
        out_shape=jax.ShapeDtypeStruct((b, seq, d), f32),
        compiler_params=pltpu.CompilerParams(
            dimension_semantics=("arbitrary", "arbitrary"), vmem_limit_bytes=VMEM_LIMIT),
        name="ffn_final",
    )(h1, g1, wg, wu, wd, g2)


def kernel(x, meta_tokens, norm_mix_g, w_in, w_attn_out, conv_w, w_conv_out, w_out,
           norm_ffn_g, w_gate, w_up, w_down, norm_final_g):
    b, seq, d = x.shape
    assert w_in.shape[0] == 1, "single-layer block"
    assert seq % ROW_TILE == 0 and meta_tokens.shape[0] == N_META
    c = conv_w.shape[2]
    lp = ROW_TILE + seq
    l_real = N_META + seq
    topk = min(TOPK_MAX, l_real // 4)
    nt = lp // ROW_TILE

    meta = jnp.broadcast_to(meta_tokens[None].astype(x.dtype), (b, N_META, d))
    hp = jnp.concatenate([jnp.zeros((b, FRONT_PAD, d), x.dtype), meta, x], axis=1)
    h2d = hp.reshape(b * lp, d)

    w = w_in[0]
    o_idx = 3 * ATTN_WIDTH
    n_idx = N_IDX_HEADS * IDX_DIM + IDX_DIM + N_IDX_HEADS
    o_conv = o_idx + n_idx
    o_gate = o_conv + 3 * c
    n_idx_pad = -(-n_idx // 128) * 128
    w_qkv = w[:, :o_idx].astype(bf16)
    w_idx = jnp.pad(w[:, o_idx:o_conv], ((0, 0), (0, n_idx_pad - n_idx))).astype(bf16)
    w_conv = w[:, o_conv:o_gate].astype(bf16)
    w_gates = w[:, o_gate:].astype(bf16)
    g_mix = norm_mix_g[0][None]

    tm = 512 if (b * lp) % 512 == 0 else ROW_TILE
    qkv_scale = jnp.concatenate([jnp.full((1, ATTN_WIDTH), LOG2E * HEAD_DIM ** -0.5, f32),
                                 jnp.ones((1, 2 * ATTN_WIDTH), f32)], axis=1)
    qkv = _rms_proj(h2d, g_mix, w_qkv, qkv_scale, bf16, tm).reshape(b, lp, 3 * ATTN_WIDTH)
    idx = _rms_proj(h2d, g_mix, w_idx, jnp.ones((1, n_idx_pad), f32), f32, tm)
    conv3 = _rms_proj(h2d, g_mix, w_conv, jnp.ones((1, 3 * c), f32), bf16, tm).reshape(b, lp, 3 * c)
    gates3 = _rms_proj(h2d, g_mix, w_gates, jnp.ones((1, 2 * d), f32), bf16, tm).reshape(b, lp, 2 * d)

    nqi = N_IDX_HEADS * IDX_DIM
    qi4 = idx[:, :nqi].astype(bf16).reshape(b, lp, N_IDX_HEADS, IDX_DIM).transpose(0, 2, 1, 3)
    kit = idx[:, nqi:nqi + IDX_DIM].astype(bf16).reshape(b, nt, ROW_TILE, IDX_DIM).transpose(0, 1, 3, 2)
    wi = idx[:, nqi + IDX_DIM:n_idx].reshape(b, lp, N_IDX_HEADS)

    y_attn = _attention(qi4, wi, kit, qkv, seq=seq, lp=lp, topk=topk)

    h1 = _mix(x, y_attn, conv3, gates3, conv_w[0], w_attn_out[0].astype(bf16),
              w_conv_out[0].astype(bf16), w_out[0].astype(bf16))
    return _ffn(h1, norm_ffn_g[0][None], w_gate[0].astype(bf16), w_up[0].astype(bf16),
                w_down[0].astype(bf16), norm_final_g[None])
```

```python
import functools

import jax
import jax.numpy as jnp
from jax import lax
from jax.experimental import pallas as pl
from jax.experimental.pallas import tpu as pltpu

N_META = 16
N_HEADS = 8
HEAD_DIM = 128
ATTN_WIDTH = N_HEADS * HEAD_DIM
N_IDX_HEADS = 8
IDX_DIM = 64
TOPK_MAX = 256
CONV_K = 3
EPS = 1e-6
IDX_SCALE = (N_IDX_HEADS ** -0.5) * (IDX_DIM ** -0.5)

ROW_TILE = 256
FRONT_PAD = ROW_TILE - N_META
MAX_KV_TILES_PER_STEP = 13
BISECT_CAP = 32
NEG = -1e30
KEEP_ALL = 1e9
BELOW_ALL = -3e38
HALO_ROWS = 16
LANE_LIST = 12
LOG2E = 1.4426950408889634
VMEM_LIMIT = 58 * 1024 * 1024

f32 = jnp.float32
bf16 = jnp.bfloat16


def _rms(x, g):
    return (x * lax.rsqrt(jnp.mean(x * x, axis=-1, keepdims=True) + EPS)) * g


def _sigmoid(x):
    return 1.0 / (1.0 + jnp.exp(-x))


def _rms_proj_kernel(h_ref, g_ref, w_ref, cs_ref, o_ref):
    a = _rms(h_ref[...], g_ref[...]).astype(bf16)
    y = jnp.dot(a, w_ref[...], preferred_element_type=f32)
    o_ref[...] = (y * cs_ref[...]).astype(o_ref.dtype)


def _rms_proj(h2d, g, w, col_scale, out_dtype, tm):
    rows, d = h2d.shape
    n = w.shape[1]
    return pl.pallas_call(
        _rms_proj_kernel,
        grid=(rows // tm,),
        in_specs=[
            pl.BlockSpec((tm, d), lambda i: (i, 0)),
            pl.BlockSpec((1, d), lambda i: (0, 0)),
            pl.BlockSpec((d, n), lambda i: (0, 0)),
            pl.BlockSpec((1, n), lambda i: (0, 0)),
        ],
        out_specs=pl.BlockSpec((tm, n), lambda i: (i, 0)),
        out_shape=jax.ShapeDtypeStruct((rows, n), out_dtype),
        compiler_params=pltpu.CompilerParams(
            dimension_semantics=("arbitrary",), vmem_limit_bytes=VMEM_LIMIT),
        name="rms_proj",
    )(h2d, g, w, col_scale)


def _attn_kernel(qi_ref, wi_ref, kit_ref, tri_ref, q_ref, k_ref, v_ref, o_ref,
                 s_scr, cand_scr, cand_t_scr, bias_scr, wb_scr, thr_scr,
                 need_scr, carry_scr, m_scr, l_scr, acc_scr, flag_scr, *, topk, g_tiles):
    tq = ROW_TILE
    kt = ROW_TILE
    rg = 128
    nh = kt // 128
    iq = pl.program_id(1)
    j = pl.program_id(2)
    nq = pl.num_programs(1) - 1
    nkb = pl.num_programs(2)
    i = iq + 1
    n_tiles = i + 1
    has_scores = iq < nq
    has_attn = iq >= 1
    t0 = j * g_tiles
    n_sc = jnp.where(has_scores, jnp.clip(n_tiles - t0, 0, g_tiles), 0)
    n_at = jnp.where(has_attn, jnp.clip(iq + 1 - t0, 0, g_tiles), 0)
    n_both = jnp.minimum(n_sc, n_at)
    kf = float(topk)

    def lanes_x(v):
        return jnp.concatenate([v] * nh, axis=1)

    def to_dense(rep):
        return jnp.concatenate(
            [rep[g * 128:(g + 1) * 128, :].T[0:8, :] for g in range(tq // 128)], axis=1)

    def to_rows(dense):
        return jnp.concatenate(
            [jnp.broadcast_to(dense[0:1, g * 128:(g + 1) * 128], (128, 128)).T
             for g in range(tq // 128)], axis=0)

    def reduce_scores(vec, fn, init, comb, red):
        vec_rows = to_rows(vec)
        outs = []
        for g in range(tq // rg):
            rows = slice(g * rg, (g + 1) * rg)
            bv = vec_rows[rows]

            def body(t, acc, rows=rows, bv=bv):
                for hl in range(nh):
                    acc = comb(acc, fn(s_scr[t, rows, hl * 128:(hl + 1) * 128], bv))
                return acc

            acc = lax.fori_loop(0, n_tiles, body, jnp.full((rg, 128), init, f32))
            outs.append(jnp.broadcast_to(red(acc, axis=1, keepdims=True), (rg, 128)))
        return to_dense(jnp.concatenate(outs, axis=0))

    def reduce_lists(vec, fn, init, comb, red):
        accs = [jnp.full((8, tq), init, f32) for _ in range(4)]
        for c in range(LANE_LIST * 128 // 8):
            accs[c % 4] = comb(accs[c % 4], fn(cand_t_scr[c * 8:(c + 1) * 8, :], vec))
        acc = comb(comb(accs[0], accs[1]), comb(accs[2], accs[3]))
        return jnp.broadcast_to(red(acc, axis=0, keepdims=True), (8, tq))

    def any_row(mask):
        return jnp.max(jnp.where(mask, 1.0, 0.0)) > 0.0

    @pl.when(jnp.logical_and(j == 0, has_attn))
    def _init_attention():
        m_scr[...] = jnp.full(m_scr.shape, NEG, f32)
        l_scr[...] = jnp.zeros(l_scr.shape, f32)
        acc_scr[...] = jnp.zeros(acc_scr.shape, f32)
        carry_scr[...] = jnp.zeros(carry_scr.shape, f32)

    @pl.when(jnp.logical_and(j == 0, has_scores))
    def _init_scores():
        w = wi_ref[0] * IDX_SCALE
        for h in range(N_IDX_HEADS):
            wb_scr[h] = jnp.broadcast_to(w[:, h:h + 1], (tq, 128))
        cand_scr[...] = jnp.full(cand_scr.shape, -jnp.inf, f32)

    def score_tile(s, causal):
        t = t0 + s
        kt_tile = kit_ref[0, t]
        acc = None
        for h in range(N_IDX_HEADS):
            x = jnp.dot(qi_ref[0, h], kt_tile, preferred_element_type=f32)
            term = jnp.maximum(x, 0.0) * lanes_x(wb_scr[h])
            acc = term if acc is None else acc + term
        kpos = t * kt + lax.broadcasted_iota(jnp.int32, (1, kt), 1)
        acc = acc + jnp.where(kpos >= FRONT_PAD, 0.0, -jnp.inf)
        if causal:
            qpos1 = i * tq + lax.broadcasted_iota(jnp.int32, (tq, 1), 0)
            acc = jnp.where(kpos <= qpos1, acc, -jnp.inf)
        s_scr[t] = acc
        for g in range(tq // 8):
            rows = slice(g * 8, (g + 1) * 8)
            lists = [cand_scr[k, rows, :] for k in range(LANE_LIST)]
            for hl in range(nh):
                x = acc[rows, hl * 128:(hl + 1) * 128]
                for k in range(LANE_LIST):
                    top = jnp.maximum(lists[k], x)
                    x = jnp.minimum(lists[k], x)
                    lists[k] = top
            for k in range(LANE_LIST):
                cand_scr[k, rows, :] = lists[k]

    def attend_mask(s, buf):
        t = t0 + s
        tie_path = flag_scr[0] > 0

        @pl.when(jnp.logical_not(tie_path))
        def _():
            thr = thr_scr[...]
            for hl in range(nh):
                lanes = slice(hl * 128, (hl + 1) * 128)
                bias_scr[buf, :, lanes] = jnp.where(s_scr[t, :, lanes] >= thr, 0.0, NEG)

        @pl.when(tie_path)
        def _():
            sc = s_scr[t]
            thr = lanes_x(thr_scr[...])
            tie = sc == thr
            counts = jnp.dot(jnp.where(tie, 1.0, 0.0).astype(bf16), tri_ref[...],
                             preferred_element_type=f32)
            seen = carry_scr[...]
            keep = (counts[:, :kt] + lanes_x(seen)) <= lanes_x(need_scr[...])
            bias_scr[buf] = jnp.where(sc > thr, 0.0,
                                      jnp.where(jnp.logical_and(tie, keep), 0.0, NEG))
            carry_scr[...] = seen + counts[:, kt:]

    def attend_heads(s, n):
        ones_cols = jnp.ones((n * kt, HEAD_DIM), bf16)
        row0 = pl.multiple_of(s * kt, kt)
        bias = jnp.concatenate([bias_scr[u] for u in range(n)], axis=1)
        for h in range(N_HEADS):
            cols = slice(h * HEAD_DIM, (h + 1) * HEAD_DIM)
            qh = q_ref[0, :, cols]
            kh = k_ref[0, pl.ds(row0, n * kt), cols]
            vh = v_ref[0, pl.ds(row0, n * kt), cols]
            lg = lax.dot_general(qh, kh, (((1,), (1,)), ((), ())),
                                 preferred_element_type=f32) + bias
            m_old = m_scr[h]
            m_new = jnp.maximum(m_old, jnp.max(lg, axis=1, keepdims=True))
            alpha = jnp.exp2(m_old - m_new)
            p = jnp.exp2(lg - jnp.concatenate([m_new] * (n * nh), axis=1))
            v_ext = jnp.concatenate([vh, ones_cols], axis=1)
            pv = jnp.dot(p.astype(bf16), v_ext, preferred_element_type=f32)
            l_scr[h] = alpha * l_scr[h] + pv[:, HEAD_DIM:]
            acc_scr[:, cols] = alpha * acc_scr[:, cols] + pv[:, :HEAD_DIM]
            m_scr[h] = m_new

    def pair_body(p, carry):
        attend_mask(2 * p, 0)
        attend_mask(2 * p + 1, 1)
        attend_heads(2 * p, 2)
        score_tile(2 * p, causal=False)
        score_tile(2 * p + 1, causal=False)
        return carry

    def both_body(s, carry):
        attend_mask(s, 0)
        attend_heads(s, 1)
        score_tile(s, causal=False)
        return carry

    def attend_body(s, carry):
        attend_mask(s, 0)
        attend_heads(s, 1)
        return carry

    def score_body(s, carry):
        score_tile(s, causal=True)
        return carry

    n_pairs = n_both // 2
    lax.fori_loop(0, n_pairs, pair_body, 0)
    lax.fori_loop(2 * n_pairs, n_both, both_body, 0)
    lax.fori_loop(n_both, n_at, attend_body, 0)
    lax.fori_loop(n_both, n_sc, score_body, 0)

    @pl.when(jnp.logical_and(has_scores, j == (n_tiles - 1) // g_tiles))
    def _threshold():
        qpos = i * tq + lax.broadcasted_iota(jnp.int32, (8, tq), 1)
        n_vis = (qpos - (FRONT_PAD - 1)).astype(f32)
        short = n_vis <= kf

        def search(reduce):
            def count_ge(v):
                return reduce(v, lambda blk, bv: jnp.where(blk >= bv, 1.0, 0.0), 0.0, jnp.add, jnp.sum)

            def count_gt(v):
                return reduce(v, lambda blk, bv: jnp.where(blk > bv, 1.0, 0.0), 0.0, jnp.add, jnp.sum)

            zero = jnp.zeros((8, tq), f32)
            cge0 = count_ge(zero)
            cgt0 = count_gt(zero)
            at_zero = jnp.logical_and(jnp.logical_not(short),
                                      jnp.logical_and(cgt0 < kf, cge0 >= kf))
            res0 = jnp.where(jnp.logical_or(short, at_zero), 1.0, 0.0)
            thr0 = jnp.where(short, BELOW_ALL, zero)
            need0 = jnp.where(jnp.logical_and(at_zero, cge0 > kf), kf - cgt0, KEEP_ALL)
            lo0 = jnp.where(cgt0 >= kf, jnp.maximum(lo_lists, zero), lo_lists)
            hi0 = jnp.where(cge0 < kf, jnp.minimum(hi_lists, zero), hi_lists)

            def bis_cond(st):
                it, _, _, _, _, active = st
                return jnp.logical_and(it < BISECT_CAP, jnp.max(active) > 0.0)

            def bis_body(st):
                it, lo, hi, thr, resolved, active = st
                piv = lo + (hi - lo) * 0.5
                splits = jnp.logical_and(piv > lo, piv < hi)
                c = count_ge(piv)
                upd = jnp.logical_and(active > 0.0, splits)
                hit = jnp.logical_and(upd, c == kf)
                thr = jnp.where(hit, piv, thr)
                resolved = jnp.where(hit, 1.0, resolved)
                lo = jnp.where(jnp.logical_and(upd, c > kf), piv, lo)
                hi = jnp.where(jnp.logical_and(upd, c < kf), piv, hi)
                active = jnp.where(jnp.logical_and(upd, jnp.logical_not(hit)), 1.0, 0.0)
                return it + 1, lo, hi, thr, resolved, active

            st = (jnp.int32(0), lo0, hi0, thr0, res0, 1.0 - res0)
            _, lo, hi, thr, resolved, _ = lax.while_loop(bis_cond, bis_body, st)

            thr_scr[...] = to_rows(thr)
            need_scr[...] = to_rows(need0)
            flag_scr[0] = jnp.where(any_row(need0 < KEEP_ALL), 1, 0).astype(jnp.int32)
            unresolved = resolved < 0.5

            @pl.when(any_row(unresolved))
            def _fallback():
                v0 = reduce(hi, lambda blk, bv: jnp.where(blk <= bv, blk, -jnp.inf),
                            -jnp.inf, jnp.maximum, jnp.max)

                def fb_cond(st):
                    _, cge, _ = st
                    return any_row(jnp.logical_and(unresolved, cge < kf))

                def fb_body(st):
                    v, cge, cgt = st
                    more = jnp.logical_and(unresolved, cge < kf)
                    v2 = reduce(v, lambda blk, bv: jnp.where(blk < bv, blk, -jnp.inf),
                                -jnp.inf, jnp.maximum, jnp.max)
                    return (jnp.where(more, v2, v), jnp.where(more, count_ge(v2), cge),
                            jnp.where(more, count_gt(v2), cgt))

                v, cge, cgt = lax.while_loop(fb_cond, fb_body, (v0, count_ge(v0), count_gt(v0)))
                need = kf - cgt
                partial = jnp.logical_and(unresolved, (cge - cgt) > need)
                thr_scr[...] = to_rows(jnp.where(unresolved, v, thr))
                need_scr[...] = to_rows(jnp.where(partial, need, need0))

                @pl.when(any_row(partial))
                def _():
                    flag_scr[0] = jnp.int32(1)

        for k in range(LANE_LIST):
            for g in range(tq // 128):
                cand_t_scr[k * 128:(k + 1) * 128, g * 128:(g + 1) * 128] = (
                    cand_scr[k, g * 128:(g + 1) * 128, :].T)

        second = cand_t_scr[128:256, :]
        lo_lists = jnp.broadcast_to(jnp.min(second, axis=0, keepdims=True), (8, tq))
        hi_lists = jnp.broadcast_to(jnp.max(second, axis=0, keepdims=True), (8, tq))

        search(reduce_lists)

        @pl.when(any_row(cand_scr[LANE_LIST - 1] >= thr_scr[...]))
        def _():
            search(reduce_scores)

    @pl.when(jnp.logical_and(has_attn, j == nkb - 1))
    def _finish():
        for h in range(N_HEADS):
            cols = slice(h * HEAD_DIM, (h + 1) * HEAD_DIM)
            o_ref[0, :, cols] = (acc_scr[:, cols] / l_scr[h]).astype(o_ref.dtype)


def _attention(qi4, wi, kit, qkv, *, seq, lp, topk):
    b = qkv.shape[0]
    tq = ROW_TILE
    nt = lp // tq
    g_tiles = max(g for g in range(1, MAX_KV_TILES_PER_STEP + 1) if nt % g == 0)
    nkb = nt // g_tiles
    nq = seq // tq

    def kv_block(iq, j):
        return jnp.minimum(j, iq // g_tiles)

    def score_tile_idx(iq):
        return jnp.minimum(iq + 1, nq)

    ka = lax.broadcasted_iota(jnp.int32, (tq, tq + 128), 0)
    kb = lax.broadcasted_iota(jnp.int32, (tq, tq + 128), 1)
    tri = jnp.where(jnp.logical_or(ka <= kb, kb >= tq), 1.0, 0.0).astype(bf16)

    kern = functools.partial(_attn_kernel, topk=topk, g_tiles=g_tiles)
    return pl.pallas_call(
        kern,
        grid=(b, nq + 1, nkb),
        in_specs=[
            pl.BlockSpec((1, N_IDX_HEADS, tq, IDX_DIM),
                         lambda bb, iq, j: (bb, 0, score_tile_idx(iq), 0)),
            pl.BlockSpec((1, tq, N_IDX_HEADS), lambda bb, iq, j: (bb, score_tile_idx(iq), 0)),
            pl.BlockSpec((1, nt, IDX_DIM, tq), lambda bb, iq, j: (bb, 0, 0, 0),
                         pipeline_mode=pl.Buffered(1)),
            pl.BlockSpec((tq, tq + 128), lambda bb, iq, j: (0, 0)),
            pl.BlockSpec((1, tq, ATTN_WIDTH), lambda bb, iq, j: (bb, iq, 0)),
            pl.BlockSpec((1, g_tiles * tq, ATTN_WIDTH), lambda bb, iq, j: (bb, kv_block(iq, j), 1)),
            pl.BlockSpec((1, g_tiles * tq, ATTN_WIDTH), lambda bb, iq, j: (bb, kv_block(iq, j), 2)),
        ],
        out_specs=pl.BlockSpec((1, tq, ATTN_WIDTH), lambda bb, iq, j: (bb, jnp.maximum(iq - 1, 0), 0)),
        out_shape=jax.ShapeDtypeStruct((b, seq, ATTN_WIDTH), bf16),
        scratch_shapes=[
            pltpu.VMEM((nt, tq, tq), f32),
            pltpu.VMEM((LANE_LIST, tq, 128), f32),
            pltpu.VMEM((LANE_LIST * 128, tq), f32),
            pltpu.VMEM((2, tq, tq), f32),
            pltpu.VMEM((N_IDX_HEADS, tq, 128), f32),
            pltpu.VMEM((tq, 128), f32),
            pltpu.VMEM((tq, 128), f32),
            pltpu.VMEM((tq, 128), f32),
            pltpu.VMEM((N_HEADS, tq, 128), f32),
            pltpu.VMEM((N_HEADS, tq, 128), f32),
            pltpu.VMEM((tq, ATTN_WIDTH), f32),
            pltpu.SMEM((1,), jnp.int32),
        ],
        compiler_params=pltpu.CompilerParams(
            dimension_semantics=("arbitrary", "arbitrary", "arbitrary"),
            vmem_limit_bytes=VMEM_LIMIT),
        name="dsa_attention",
    )(qi4, wi, kit, tri, qkv, qkv, qkv)


def _mix_kernel(x_ref, ya_ref, cv_ref, cvp_ref, gt_ref, cw_ref, wao_ref, wco_ref, wout_ref,
                o_ref, u_scr):
    tm = x_ref.shape[1]
    c = cw_ref.shape[1]
    hr = cvp_ref.shape[1]
    cu = cv_ref[0, :, 0:c].astype(f32)
    cb = cv_ref[0, :, c:2 * c].astype(f32)
    cc = cv_ref[0, :, 2 * c:3 * c].astype(f32)
    u_scr[0:hr, :] = cvp_ref[0, :, 2 * c:3 * c].astype(f32) * cvp_ref[0, :, 0:c].astype(f32)
    u_scr[hr:hr + tm, :] = cc * cu
    w = cw_ref[...]
    conv = (w[2:3] * u_scr[hr:hr + tm, :] + w[1:2] * u_scr[hr - 1:hr - 1 + tm, :]
            + w[0:1] * u_scr[hr - 2:hr - 2 + tm, :])
    y_conv = jnp.dot((cb * conv).astype(bf16), wco_ref[...], preferred_element_type=f32)
    y_attn = jnp.dot(ya_ref[0], wao_ref[...], preferred_element_type=f32)
    d = wout_ref.shape[0]
    mixed = (_sigmoid(gt_ref[0, :, 0:d].astype(f32)) * y_attn
             + _sigmoid(gt_ref[0, :, d:2 * d].astype(f32)) * y_conv)
    o_ref[0] = x_ref[0] + jnp.dot(mixed.astype(bf16), wout_ref[...], preferred_element_type=f32)


def _mix(x, y_attn, conv3, gates3, conv_w, wao, wco, wout):
    b, seq, d = x.shape
    tm = ROW_TILE
    c = conv_w.shape[1]
    const = lambda bb, i: (0, 0)
    return pl.pallas_call(
        _mix_kernel,
        grid=(b, seq // tm),
        in_specs=[
            pl.BlockSpec((1, tm, d), lambda bb, i: (bb, i, 0)),
            pl.BlockSpec((1, tm, ATTN_WIDTH), lambda bb, i: (bb, i, 0)),
            pl.BlockSpec((1, tm, 3 * c), lambda bb, i: (bb, i + 1, 0)),
            pl.BlockSpec((1, HALO_ROWS, 3 * c),
                         lambda bb, i: (bb, (i + 1) * (tm // HALO_ROWS) - 1, 0)),
            pl.BlockSpec((1, tm, 2 * d), lambda bb, i: (bb, i + 1, 0)),
            pl.BlockSpec((CONV_K, c), const),
            pl.BlockSpec(wao.shape, const),
            pl.BlockSpec(wco.shape, const),
            pl.BlockSpec(wout.shape, const),
        ],
        out_specs=pl.BlockSpec((1, tm, d), lambda bb, i: (bb, i, 0)),
        out_shape=jax.ShapeDtypeStruct((b, seq, d), f32),
        scratch_shapes=[pltpu.VMEM((tm + HALO_ROWS, c), f32)],
        compiler_params=pltpu.CompilerParams(
            dimension_semantics=("arbitrary", "arbitrary"), vmem_limit_bytes=VMEM_LIMIT),
        name="mix_merge",
    )(x, y_attn, conv3, conv3, gates3, conv_w, wao, wco, wout)


def _ffn_kernel(h_ref, g1_ref, wg_ref, wu_ref, wd_ref, g2_ref, o_ref):
    h = h_ref[0]
    f = _rms(h, g1_ref[...]).astype(bf16)
    gate = jnp.dot(f, wg_ref[...], preferred_element_type=f32)
    up = jnp.dot(f, wu_ref[...], preferred_element_type=f32)
    act = (gate * _sigmoid(gate)) * up
    h2 = h + jnp.dot(act.astype(bf16), wd_ref[...], preferred_element_type=f32)
    o_ref[0] = _rms(h2, g2_ref[...])


def _ffn(h1, g1, wg, wu, wd, g2):
    b, seq, d = h1.shape
    tm = ROW_TILE
    const = lambda bb, i: (0, 0)
    return pl.pallas_call(
        _ffn_kernel,
        grid=(b, seq // tm),
        in_specs=[
            pl.BlockSpec((1, tm, d), lambda bb, i: (bb, i, 0)),
            pl.BlockSpec((1, d), const),
            pl.BlockSpec(wg.shape, const),
            pl.BlockSpec(wu.shape, const),
            pl.BlockSpec(wd.shape, const),
            pl.BlockSpec((1, d), const),
        ],
        out_specs=pl.BlockSpec((1, tm, d), lambda bb, i: (bb, i, 0)),
        out_shape=jax.ShapeDtypeStruct((b, seq, d), f32),
        compiler_params=pltpu.CompilerParams(
            dimension_semantics=("arbitrary", "arbitrary"), vmem_limit_bytes=VMEM_LIMIT),
        name="ffn_final",
    )(h1, g1, wg, wu, wd, g2)


def kernel(x, meta_tokens, norm_mix_g, w_in, w_attn_out, conv_w, w_conv_out, w_out,
           norm_ffn_g, w_gate, w_up, w_down, norm_final_g):
    b, seq, d = x.shape
    assert w_in.shape[0] == 1, "single-layer block"
    assert seq % ROW_TILE == 0 and meta_tokens.shape[0] == N_META
    c = conv_w.shape[2]
    lp = ROW_TILE + seq
    l_real = N_META + seq
    topk = min(TOPK_MAX, l_real // 4)
    nt = lp // ROW_TILE

    meta = jnp.broadcast_to(meta_tokens[None].astype(x.dtype), (b, N_META, d))
    hp = jnp.concatenate([jnp.zeros((b, FRONT_PAD, d), x.dtype), meta, x], axis=1)
    h2d = hp.reshape(b * lp, d)

    w = w_in[0]
    o_idx = 3 * ATTN_WIDTH
    n_idx = N_IDX_HEADS * IDX_DIM + IDX_DIM + N_IDX_HEADS
    o_conv = o_idx + n_idx
    o_gate = o_conv + 3 * c
    n_idx_pad = -(-n_idx // 128) * 128
    w_qkv = w[:, :o_idx].astype(bf16)
    w_idx = jnp.pad(w[:, o_idx:o_conv], ((0, 0), (0, n_idx_pad - n_idx))).astype(bf16)
    w_conv = w[:, o_conv:o_gate].astype(bf16)
    w_gates = w[:, o_gate:].astype(bf16)
    g_mix = norm_mix_g[0][None]

    tm = 512 if (b * lp) % 512 == 0 else ROW_TILE
    qkv_scale = jnp.concatenate([jnp.full((1, ATTN_WIDTH), LOG2E * HEAD_DIM ** -0.5, f32),
                                 jnp.ones((1, 2 * ATTN_WIDTH), f32)], axis=1)
    qkv = _rms_proj(h2d, g_mix, w_qkv, qkv_scale, bf16, tm).reshape(b, lp, 3 * ATTN_WIDTH)
    idx = _rms_proj(h2d, g_mix, w_idx, jnp.ones((1, n_idx_pad), f32), f32, tm)
    conv3 = _rms_proj(h2d, g_mix, w_conv, jnp.ones((1, 3 * c), f32), bf16, tm).reshape(b, lp, 3 * c)
    gates3 = _rms_proj(h2d, g_mix, w_gates, jnp.ones((1, 2 * d), f32), bf16, tm).reshape(b, lp, 2 * d)

    nqi = N_IDX_HEADS * IDX_DIM
    qi4 = idx[:, :nqi].astype(bf16).reshape(b, lp, N_IDX_HEADS, IDX_DIM).transpose(0, 2, 1, 3)
    kit = idx[:, nqi:nqi + IDX_DIM].astype(bf16).reshape(b, nt, ROW_TILE, IDX_DIM).transpose(0, 1, 3, 2)
    wi = idx[:, nqi + IDX_DIM:n_idx].reshape(b, lp, N_IDX_HEADS)

    y_attn = _attention(qi4, wi, kit, qkv, seq=seq, lp=lp, topk=topk)

    h1 = _mix(x, y_attn, conv3, gates3, conv_w[0], w_attn_out[0].astype(bf16),
              w_conv_out[0].astype(bf16), w_out[0].astype(bf16))
    return _ffn(h1, norm_ffn_g[0][None], w_gate[0].astype(bf16), w_up[0].astype(bf16),
                w_down[0].astype(bf16), norm_final_g[None])
```

```python
import functools

import jax
import jax.numpy as jnp
from jax import lax
from jax.experimental import pallas as pl
from jax.experimental.pallas import tpu as pltpu

N_META = 16
N_HEADS = 8
HEAD_DIM = 128
ATTN_WIDTH = N_HEADS * HEAD_DIM
N_IDX_HEADS = 8
IDX_DIM = 64
TOPK_MAX = 256
CONV_K = 3
EPS = 1e-6
IDX_SCALE = (N_IDX_HEADS ** -0.5) * (IDX_DIM ** -0.5)

ROW_TILE = 256
FRONT_PAD = ROW_TILE - N_META
MAX_KV_TILES_PER_STEP = 13
BISECT_CAP = 32
NEG = -1e30
KEEP_ALL = 1e9
BELOW_ALL = -3e38
HALO_ROWS = 16
LANE_LIST = 12
LOG2E = 1.4426950408889634
VMEM_LIMIT = 58 * 1024 * 1024

f32 = jnp.float32
bf16 = jnp.bfloat16


def _rms(x, g):
    return (x * lax.rsqrt(jnp.mean(x * x, axis=-1, keepdims=True) + EPS)) * g


def _sigmoid(x):
    return 1.0 / (1.0 + jnp.exp(-x))


def _rms_proj_kernel(h_ref, g_ref, w_ref, cs_ref, o_ref):
    a = _rms(h_ref[...], g_ref[...]).astype(bf16)
    y = jnp.dot(a, w_ref[...], preferred_element_type=f32)
    o_ref[...] = (y * cs_ref[...]).astype(o_ref.dtype)


def _rms_proj(h2d, g, w, col_scale, out_dtype, tm):
    rows, d = h2d.shape
    n = w.shape[1]
    return pl.pallas_call(
        _rms_proj_kernel,
        grid=(rows // tm,),
        in_specs=[
            pl.BlockSpec((tm, d), lambda i: (i, 0)),
            pl.BlockSpec((1, d), lambda i: (0, 0)),
            pl.BlockSpec((d, n), lambda i: (0, 0)),
            pl.BlockSpec((1, n), lambda i: (0, 0)),
        ],
        out_specs=pl.BlockSpec((tm, n), lambda i: (i, 0)),
        out_shape=jax.ShapeDtypeStruct((rows, n), out_dtype),
        compiler_params=pltpu.CompilerParams(
            dimension_semantics=("arbitrary",), vmem_limit_bytes=VMEM_LIMIT),
        name="rms_proj",
    )(h2d, g, w, col_scale)


def _rms_proj_idx_kernel(h_ref, g_ref, w_ref, qi_ref, kit_ref, wi_ref):
    a = _rms(h_ref[...], g_ref[...]).astype(bf16)
    y = jnp.dot(a, w_ref[...], preferred_element_type=f32)
    nqi = N_IDX_HEADS * IDX_DIM
    for h in range(N_IDX_HEADS):
        qi_ref[0, h] = y[:, h * IDX_DIM:(h + 1) * IDX_DIM].astype(bf16)
    kit_ref[0, 0] = y[:, nqi:nqi + 128].T[:IDX_DIM, :].astype(bf16)
    wi_ref[0] = y[:, nqi + IDX_DIM:nqi + IDX_DIM + N_IDX_HEADS]


def _rms_proj_idx(h2d, g, w, *, b, lp):
    d = h2d.shape[1]
    n = w.shape[1]
    tm = ROW_TILE
    nt = lp // tm
    return pl.pallas_call(
        _rms_proj_idx_kernel,
        grid=(b, nt),
        in_specs=[
            pl.BlockSpec((tm, d), lambda bb, i: (bb * nt + i, 0)),
            pl.BlockSpec((1, d), lambda bb, i: (0, 0)),
            pl.BlockSpec((d, n), lambda bb, i: (0, 0)),
        ],
        out_specs=[
            pl.BlockSpec((1, N_IDX_HEADS, tm, IDX_DIM), lambda bb, i: (bb, 0, i, 0)),
            pl.BlockSpec((1, 1, IDX_DIM, tm), lambda bb, i: (bb, i, 0, 0)),
            pl.BlockSpec((1, tm, N_IDX_HEADS), lambda bb, i: (bb, i, 0)),
        ],
        out_shape=[
            jax.ShapeDtypeStruct((b, N_IDX_HEADS, lp, IDX_DIM), bf16),
            jax.ShapeDtypeStruct((b, nt, IDX_DIM, tm), bf16),
            jax.ShapeDtypeStruct((b, lp, N_IDX_HEADS), f32),
        ],
        compiler_params=pltpu.CompilerParams(
            dimension_semantics=("arbitrary", "arbitrary"), vmem_limit_bytes=VMEM_LIMIT),
        name="rms_proj_idx",
    )(h2d, g, w)


def _attn_kernel(qi_ref, wi_ref, kit_ref, tri_ref, q_ref, k_ref, v_ref, o_ref,
                 s_scr, cand_scr, cand_t_scr, bias_scr, wb_scr, thr_scr,
                 need_scr, carry_scr, m_scr, l_scr, acc_scr, flag_scr, *, topk, g_tiles):
    tq = ROW_TILE
    kt = ROW_TILE
    rg = 128
    nh = kt // 128
    iq = pl.program_id(1)
    j = pl.program_id(2)
    nq = pl.num_programs(1) - 1
    nkb = pl.num_programs(2)
    i = iq + 1
    n_tiles = i + 1
    has_scores = iq < nq
    has_attn = iq >= 1
    t0 = j * g_tiles
    n_sc = jnp.where(has_scores, jnp.clip(n_tiles - t0, 0, g_tiles), 0)
    n_at = jnp.where(has_attn, jnp.clip(iq + 1 - t0, 0, g_tiles), 0)
    n_both = jnp.minimum(n_sc, n_at)
    kf = float(topk)

    def lanes_x(v):
        return jnp.concatenate([v] * nh, axis=1)

    def to_dense(rep):
        return jnp.concatenate(
            [rep[g * 128:(g + 1) * 128, :].T[0:8, :] for g in range(tq // 128)], axis=1)

    def to_rows(dense):
        return jnp.concatenate(
            [jnp.broadcast_to(dense[0:1, g * 128:(g + 1) * 128], (128, 128)).T
             for g in range(tq // 128)], axis=0)

    def reduce_scores(vec, fn, init, comb, red):
        vec_rows = to_rows(vec)
        outs = []
        for g in range(tq // rg):
            rows = slice(g * rg, (g + 1) * rg)
            bv = vec_rows[rows]

            def body(t, acc, rows=rows, bv=bv):
                for hl in range(nh):
                    acc = comb(acc, fn(s_scr[t, rows, hl * 128:(hl + 1) * 128], bv))
                return acc

            acc = lax.fori_loop(0, n_tiles, body, jnp.full((rg, 128), init, f32))
            outs.append(jnp.broadcast_to(red(acc, axis=1, keepdims=True), (rg, 128)))
        return to_dense(jnp.concatenate(outs, axis=0))

    def reduce_lists(vec, fn, init, comb, red):
        accs = [jnp.full((8, tq), init, f32) for _ in range(4)]
        for c in range(LANE_LIST * 128 // 8):
            accs[c % 4] = comb(accs[c % 4], fn(cand_t_scr[c * 8:(c + 1) * 8, :], vec))
        acc = comb(comb(accs[0], accs[1]), comb(accs[2], accs[3]))
        return jnp.broadcast_to(red(acc, axis=0, keepdims=True), (8, tq))

    def any_row(mask):
        return jnp.max(jnp.where(mask, 1.0, 0.0)) > 0.0

    @pl.when(jnp.logical_and(j == 0, has_attn))
    def _init_attention():
        m_scr[...] = jnp.full(m_scr.shape, NEG, f32)
        l_scr[...] = jnp.zeros(l_scr.shape, f32)
        acc_scr[...] = jnp.zeros(acc_scr.shape, f32)
        carry_scr[...] = jnp.zeros(carry_scr.shape, f32)

    @pl.when(jnp.logical_and(j == 0, has_scores))
    def _init_scores():
        w = wi_ref[0] * IDX_SCALE
        for h in range(N_IDX_HEADS):
            wb_scr[h] = jnp.broadcast_to(w[:, h:h + 1], (tq, 128))
        cand_scr[...] = jnp.full(cand_scr.shape, -jnp.inf, f32)

    def score_tile(s, causal):
        t = t0 + s
        kt_tile = kit_ref[0, t]
        acc = None
        for h in range(N_IDX_HEADS):
            x = jnp.dot(qi_ref[0, h], kt_tile, preferred_element_type=f32)
            term = jnp.maximum(x, 0.0) * lanes_x(wb_scr[h])
            acc = term if acc is None else acc + term
        kpos = t * kt + lax.broadcasted_iota(jnp.int32, (1, kt), 1)
        acc = acc + jnp.where(kpos >= FRONT_PAD, 0.0, -jnp.inf)
        if causal:
            qpos1 = i * tq + lax.broadcasted_iota(jnp.int32, (tq, 1), 0)
            acc = jnp.where(kpos <= qpos1, acc, -jnp.inf)
        s_scr[t] = acc
        for g in range(tq // 8):
            rows = slice(g * 8, (g + 1) * 8)
            lists = [cand_scr[k, rows, :] for k in range(LANE_LIST)]
            for hl in range(nh):
                x = acc[rows, hl * 128:(hl + 1) * 128]
                for k in range(LANE_LIST):
                    top = jnp.maximum(lists[k], x)
                    x = jnp.minimum(lists[k], x)
                    lists[k] = top
            for k in range(LANE_LIST):
                cand_scr[k, rows, :] = lists[k]

    def attend_mask(s, buf):
        t = t0 + s
        tie_path = flag_scr[0] > 0

        @pl.when(jnp.logical_not(tie_path))
        def _():
            thr = thr_scr[...]
            for hl in range(nh):
                lanes = slice(hl * 128, (hl + 1) * 128)
                bias_scr[buf, :, lanes] = jnp.where(s_scr[t, :, lanes] >= thr, 0.0, NEG)

        @pl.when(tie_path)
        def _():
            sc = s_scr[t]
            thr = lanes_x(thr_scr[...])
            tie = sc == thr
            counts = jnp.dot(jnp.where(tie, 1.0, 0.0).astype(bf16), tri_ref[...],
                             preferred_element_type=f32)
            seen = carry_scr[...]
            keep = (counts[:, :kt] + lanes_x(seen)) <= lanes_x(need_scr[...])
            bias_scr[buf] = jnp.where(sc > thr, 0.0,
                                      jnp.where(jnp.logical_and(tie, keep), 0.0, NEG))
            carry_scr[...] = seen + counts[:, kt:]

    def attend_heads(s, n):
        ones_cols = jnp.ones((n * kt, HEAD_DIM), bf16)
        row0 = pl.multiple_of(s * kt, kt)
        bias = jnp.concatenate([bias_scr[u] for u in range(n)], axis=1)
        for h in range(N_HEADS):
            cols = slice(h * HEAD_DIM, (h + 1) * HEAD_DIM)
            qh = q_ref[0, :, cols]
            kh = k_ref[0, pl.ds(row0, n * kt), cols]
            vh = v_ref[0, pl.ds(row0, n * kt), cols]
            lg = lax.dot_general(qh, kh, (((1,), (1,)), ((), ())),
                                 preferred_element_type=f32) + bias
            m_old = m_scr[h]
            m_new = jnp.maximum(m_old, jnp.max(lg, axis=1, keepdims=True))
            alpha = jnp.exp2(m_old - m_new)
            p = jnp.exp2(lg - jnp.concatenate([m_new] * (n * nh), axis=1))
            v_ext = jnp.concatenate([vh, ones_cols], axis=1)
            pv = jnp.dot(p.astype(bf16), v_ext, preferred_element_type=f32)
            l_scr[h] = alpha * l_scr[h] + pv[:, HEAD_DIM:]
            acc_scr[:, cols] = alpha * acc_scr[:, cols] + pv[:, :HEAD_DIM]
            m_scr[h] = m_new

    def pair_body(p, carry):
        attend_mask(2 * p, 0)
        attend_mask(2 * p + 1, 1)
        attend_heads(2 * p, 2)
        score_tile(2 * p, causal=False)
        score_tile(2 * p + 1, causal=False)
        return carry

    def both_body(s, carry):
        attend_mask(s, 0)
        attend_heads(s, 1)
        score_tile(s, causal=False)
        return carry

    def attend_body(s, carry):
        attend_mask(s, 0)
        attend_heads(s, 1)
        return carry

    def score_body(s, carry):
        score_tile(s, causal=True)
        return carry

    n_pairs = n_both // 2
    lax.fori_loop(0, n_pairs, pair_body, 0)
    lax.fori_loop(2 * n_pairs, n_both, both_body, 0)
    lax.fori_loop(n_both, n_at, attend_body, 0)
    lax.fori_loop(n_both, n_sc, score_body, 0)

    @pl.when(jnp.logical_and(has_scores, j == (n_tiles - 1) // g_tiles))
    def _threshold():
        qpos = i * tq + lax.broadcasted_iota(jnp.int32, (8, tq), 1)
        n_vis = (qpos - (FRONT_PAD - 1)).astype(f32)
        short = n_vis <= kf

        def search(reduce):
            def count_ge(v):
                return reduce(v, lambda blk, bv: jnp.where(blk >= bv, 1.0, 0.0), 0.0, jnp.add, jnp.sum)

            def count_gt(v):
                return reduce(v, lambda blk, bv: jnp.where(blk > bv, 1.0, 0.0), 0.0, jnp.add, jnp.sum)

            zero = jnp.zeros((8, tq), f32)
            cge0 = count_ge(zero)
            cgt0 = count_gt(zero)
            at_zero = jnp.logical_and(jnp.logical_not(short),
                                      jnp.logical_and(cgt0 < kf, cge0 >= kf))
            res0 = jnp.where(jnp.logical_or(short, at_zero), 1.0, 0.0)
            thr0 = jnp.where(short, BELOW_ALL, zero)
            need0 = jnp.where(jnp.logical_and(at_zero, cge0 > kf), kf - cgt0, KEEP_ALL)
            lo0 = jnp.where(cgt0 >= kf, jnp.maximum(lo_lists, zero), lo_lists)
            hi0 = jnp.where(cge0 < kf, jnp.minimum(hi_lists, zero), hi_lists)

            def bis_cond(st):
                it, _, _, _, _, active = st
                return jnp.logical_and(it < BISECT_CAP, jnp.max(active) > 0.0)

            def bis_body(st):
                it, lo, hi, thr, resolved, active = st
                piv = lo + (hi - lo) * 0.5
                splits = jnp.logical_and(piv > lo, piv < hi)
                c = count_ge(piv)
                upd = jnp.logical_and(active > 0.0, splits)
                hit = jnp.logical_and(upd, c == kf)
                thr = jnp.where(hit, piv, thr)
                resolved = jnp.where(hit, 1.0, resolved)
                lo = jnp.where(jnp.logical_and(upd, c > kf), piv, lo)
                hi = jnp.where(jnp.logical_and(upd, c < kf), piv, hi)
                active = jnp.where(jnp.logical_and(upd, jnp.logical_not(hit)), 1.0, 0.0)
                return it + 1, lo, hi, thr, resolved, active

            st = (jnp.int32(0), lo0, hi0, thr0, res0, 1.0 - res0)
            _, lo, hi, thr, resolved, _ = lax.while_loop(bis_cond, bis_body, st)

            thr_scr[...] = to_rows(thr)
            need_scr[...] = to_rows(need0)
            flag_scr[0] = jnp.where(any_row(need0 < KEEP_ALL), 1, 0).astype(jnp.int32)
            unresolved = resolved < 0.5

            @pl.when(any_row(unresolved))
            def _fallback():
                v0 = reduce(hi, lambda blk, bv: jnp.where(blk <= bv, blk, -jnp.inf),
                            -jnp.inf, jnp.maximum, jnp.max)

                def fb_cond(st):
                    _, cge, _ = st
                    return any_row(jnp.logical_and(unresolved, cge < kf))

                def fb_body(st):
                    v, cge, cgt = st
                    more = jnp.logical_and(unresolved, cge < kf)
                    v2 = reduce(v, lambda blk, bv: jnp.where(blk < bv, blk, -jnp.inf),
                                -jnp.inf, jnp.maximum, jnp.max)
                    return (jnp.where(more, v2, v), jnp.where(more, count_ge(v2), cge),
                            jnp.where(more, count_gt(v2), cgt))

                v, cge, cgt = lax.while_loop(fb_cond, fb_body, (v0, count_ge(v0), count_gt(v0)))
                need = kf - cgt
                partial = jnp.logical_and(unresolved, (cge - cgt) > need)
                thr_scr[...] = to_rows(jnp.where(unresolved, v, thr))
                need_scr[...] = to_rows(jnp.where(partial, need, need0))

                @pl.when(any_row(partial))
                def _():
                    flag_scr[0] = jnp.int32(1)

        for k in range(LANE_LIST):
            for g in range(tq // 128):
                cand_t_scr[k * 128:(k + 1) * 128, g * 128:(g + 1) * 128] = (
                    cand_scr[k, g * 128:(g + 1) * 128, :].T)

        second = cand_t_scr[128:256, :]
        lo_lists = jnp.broadcast_to(jnp.min(second, axis=0, keepdims=True), (8, tq))
        hi_lists = jnp.broadcast_to(jnp.max(second, axis=0, keepdims=True), (8, tq))

        search(reduce_lists)

        @pl.when(any_row(cand_scr[LANE_LIST - 1] >= thr_scr[...]))
        def _():
            search(reduce_scores)

    @pl.when(jnp.logical_and(has_attn, j == nkb - 1))
    def _finish():
        for h in range(N_HEADS):
            cols = slice(h * HEAD_DIM, (h + 1) * HEAD_DIM)
            o_ref[0, :, cols] = (acc_scr[:, cols] / l_scr[h]).astype(o_ref.dtype)


def _attention(qi4, wi, kit, qkv, *, seq, lp, topk):
    b = qkv.shape[0]
    tq = ROW_TILE
    nt = lp // tq
    g_tiles = max(g for g in range(1, MAX_KV_TILES_PER_STEP + 1) if nt % g == 0)
    nkb = nt // g_tiles
    nq = seq // tq

    def kv_block(iq, j):
        return jnp.minimum(j, iq // g_tiles)

    def score_tile_idx(iq):
        return jnp.minimum(iq + 1, nq)

    ka = lax.broadcasted_iota(jnp.int32, (tq, tq + 128), 0)
    kb = lax.broadcasted_iota(jnp.int32, (tq, tq + 128), 1)
    tri = jnp.where(jnp.logical_or(ka <= kb, kb >= tq), 1.0, 0.0).astype(bf16)

    kern = functools.partial(_attn_kernel, topk=topk, g_tiles=g_tiles)
    return pl.pallas_call(
        kern,
        grid=(b, nq + 1, nkb),
        in_specs=[
            pl.BlockSpec((1, N_IDX_HEADS, tq, IDX_DIM),
                         lambda bb, iq, j: (bb, 0, score_tile_idx(iq), 0)),
            pl.BlockSpec((1, tq, N_IDX_HEADS), lambda bb, iq, j: (bb, score_tile_idx(iq), 0)),
            pl.BlockSpec((1, nt, IDX_DIM, tq), lambda bb, iq, j: (bb, 0, 0, 0),
                         pipeline_mode=pl.Buffered(1)),
            pl.BlockSpec((tq, tq + 128), lambda bb, iq, j: (0, 0)),
            pl.BlockSpec((1, tq, ATTN_WIDTH), lambda bb, iq, j: (bb, iq, 0)),
            pl.BlockSpec((1, g_tiles * tq, ATTN_WIDTH), lambda bb, iq, j: (bb, kv_block(iq, j), 1)),
            pl.BlockSpec((1, g_tiles * tq, ATTN_WIDTH), lambda bb, iq, j: (bb, kv_block(iq, j), 2)),
        ],
        out_specs=pl.BlockSpec((1, tq, ATTN_WIDTH), lambda bb, iq, j: (bb, jnp.maximum(iq - 1, 0), 0)),
        out_shape=jax.ShapeDtypeStruct((b, seq, ATTN_WIDTH), bf16),
        scratch_shapes=[
            pltpu.VMEM((nt, tq, tq), f32),
            pltpu.VMEM((LANE_LIST, tq, 128), f32),
            pltpu.VMEM((LANE_LIST * 128, tq), f32),
            pltpu.VMEM((2, tq, tq), f32),
            pltpu.VMEM((N_IDX_HEADS, tq, 128), f32),
            pltpu.VMEM((tq, 128), f32),
            pltpu.VMEM((tq, 128), f32),
            pltpu.VMEM((tq, 128), f32),
            pltpu.VMEM((N_HEADS, tq, 128), f32),
            pltpu.VMEM((N_HEADS, tq, 128), f32),
            pltpu.VMEM((tq, ATTN_WIDTH), f32),
            pltpu.SMEM((1,), jnp.int32),
        ],
        compiler_params=pltpu.CompilerParams(
            dimension_semantics=("arbitrary", "arbitrary", "arbitrary"),
            vmem_limit_bytes=VMEM_LIMIT),
        name="dsa_attention",
    )(qi4, wi, kit, tri, qkv, qkv, qkv)


def _mix_kernel(x_ref, ya_ref, cv_ref, cvp_ref, gt_ref, cw_ref, wao_ref, wco_ref, wout_ref,
                o_ref, u_scr):
    tm = x_ref.shape[1]
    c = cw_ref.shape[1]
    hr = cvp_ref.shape[1]
    cu = cv_ref[0, :, 0:c].astype(f32)
    cb = cv_ref[0, :, c:2 * c].astype(f32)
    cc = cv_ref[0, :, 2 * c:3 * c].astype(f32)
    u_scr[0:hr, :] = cvp_ref[0, :, 2 * c:3 * c].astype(f32) * cvp_ref[0, :, 0:c].astype(f32)
    u_scr[hr:hr + tm, :] = cc * cu
    w = cw_ref[...]
    conv = (w[2:3] * u_scr[hr:hr + tm, :] + w[1:2] * u_scr[hr - 1:hr - 1 + tm, :]
            + w[0:1] * u_scr[hr - 2:hr - 2 + tm, :])
    y_conv = jnp.dot((cb * conv).astype(bf16), wco_ref[...], preferred_element_type=f32)
    y_attn = jnp.dot(ya_ref[0], wao_ref[...], preferred_element_type=f32)
    d = wout_ref.shape[0]
    mixed = (_sigmoid(gt_ref[0, :, 0:d].astype(f32)) * y_attn
             + _sigmoid(gt_ref[0, :, d:2 * d].astype(f32)) * y_conv)
    o_ref[0] = x_ref[0] + jnp.dot(mixed.astype(bf16), wout_ref[...], preferred_element_type=f32)


def _mix(x, y_attn, conv3, gates3, conv_w, wao, wco, wout):
    b, seq, d = x.shape
    tm = ROW_TILE
    c = conv_w.shape[1]
    const = lambda bb, i: (0, 0)
    return pl.pallas_call(
        _mix_kernel,
        grid=(b, seq // tm),
        in_specs=[
            pl.BlockSpec((1, tm, d), lambda bb, i: (bb, i, 0)),
            pl.BlockSpec((1, tm, ATTN_WIDTH), lambda bb, i: (bb, i, 0)),
            pl.BlockSpec((1, tm, 3 * c), lambda bb, i: (bb, i + 1, 0)),
            pl.BlockSpec((1, HALO_ROWS, 3 * c),
                         lambda bb, i: (bb, (i + 1) * (tm // HALO_ROWS) - 1, 0)),
            pl.BlockSpec((1, tm, 2 * d), lambda bb, i: (bb, i + 1, 0)),
            pl.BlockSpec((CONV_K, c), const),
            pl.BlockSpec(wao.shape, const),
            pl.BlockSpec(wco.shape, const),
            pl.BlockSpec(wout.shape, const),
        ],
        out_specs=pl.BlockSpec((1, tm, d), lambda bb, i: (bb, i, 0)),
        out_shape=jax.ShapeDtypeStruct((b, seq, d), f32),
        scratch_shapes=[pltpu.VMEM((tm + HALO_ROWS, c), f32)],
        compiler_params=pltpu.CompilerParams(
            dimension_semantics=("arbitrary", "arbitrary"), vmem_limit_bytes=VMEM_LIMIT),
        name="mix_merge",
    )(x, y_attn, conv3, conv3, gates3, conv_w, wao, wco, wout)


def _ffn_kernel(h_ref, g1_ref, wg_ref, wu_ref, wd_ref, g2_ref, o_ref):
    h = h_ref[0]
    f = _rms(h, g1_ref[...]).astype(bf16)
    gate = jnp.dot(f, wg_ref[...], preferred_element_type=f32)
    up = jnp.dot(f, wu_ref[...], preferred_element_type=f32)
    act = (gate * _sigmoid(gate)) * up
    h2 = h + jnp.dot(act.astype(bf16), wd_ref[...], preferred_element_type=f32)
    o_ref[0] = _rms(h2, g2_ref[...])


def _ffn(h1, g1, wg, wu, wd, g2):
    b, seq, d = h1.shape
    tm = ROW_TILE
    const = lambda bb, i: (0, 0)
    return pl.pallas_call(
        _ffn_kernel,
        grid=(b, seq // tm),
        in_specs=[
            pl.BlockSpec((1, tm, d), lambda bb, i: (bb, i, 0)),
            pl.BlockSpec((1, d), const),
            pl.BlockSpec(wg.shape, const),
            pl.BlockSpec(wu.shape, const),
            pl.BlockSpec(wd.shape, const),
            pl.BlockSpec((1, d), const),
        ],
        out_specs=pl.BlockSpec((1, tm, d), lambda bb, i: (bb, i, 0)),
        out_shape=jax.ShapeDtypeStruct((b, seq, d), f32),
        compiler_params=pltpu.CompilerParams(
            dimension_semantics=("arbitrary", "arbitrary"), vmem_limit_bytes=VMEM_LIMIT),
        name="ffn_final",
    )(h1, g1, wg, wu, wd, g2)


def kernel(x, meta_tokens, norm_mix_g, w_in, w_attn_out, conv_w, w_conv_out, w_out,
           norm_ffn_g, w_gate, w_up, w_down, norm_final_g):
    b, seq, d = x.shape
    assert w_in.shape[0] == 1, "single-layer block"
    assert seq % ROW_TILE == 0 and meta_tokens.shape[0] == N_META
    c = conv_w.shape[2]
    lp = ROW_TILE + seq
    l_real = N_META + seq
    topk = min(TOPK_MAX, l_real // 4)
    nt = lp // ROW_TILE

    meta = jnp.broadcast_to(meta_tokens[None].astype(x.dtype), (b, N_META, d))
    hp = jnp.concatenate([jnp.zeros((b, FRONT_PAD, d), x.dtype), meta, x], axis=1)
    h2d = hp.reshape(b * lp, d)

    w = w_in[0]
    o_idx = 3 * ATTN_WIDTH
    n_idx = N_IDX_HEADS * IDX_DIM + IDX_DIM + N_IDX_HEADS
    o_conv = o_idx + n_idx
    o_gate = o_conv + 3 * c
    n_idx_pad = N_IDX_HEADS * IDX_DIM + 128
    w_qkv = w[:, :o_idx].astype(bf16)
    w_idx = jnp.pad(w[:, o_idx:o_conv], ((0, 0), (0, n_idx_pad - n_idx))).astype(bf16)
    w_conv = w[:, o_conv:o_gate].astype(bf16)
    w_gates = w[:, o_gate:].astype(bf16)
    g_mix = norm_mix_g[0][None]

    tm = 512 if (b * lp) % 512 == 0 else ROW_TILE
    qkv_scale = jnp.concatenate([jnp.full((1, ATTN_WIDTH), LOG2E * HEAD_DIM ** -0.5, f32),
                                 jnp.ones((1, 2 * ATTN_WIDTH), f32)], axis=1)
    qkv = _rms_proj(h2d, g_mix, w_qkv, qkv_scale, bf16, tm).reshape(b, lp, 3 * ATTN_WIDTH)
    qi4, kit, wi = _rms_proj_idx(h2d, g_mix, w_idx, b=b, lp=lp)
    conv3 = _rms_proj(h2d, g_mix, w_conv, jnp.ones((1, 3 * c), f32), bf16, tm).reshape(b, lp, 3 * c)
    gates3 = _rms_proj(h2d, g_mix, w_gates, jnp.ones((1, 2 * d), f32), bf16, tm).reshape(b, lp, 2 * d)

    y_attn = _attention(qi4, wi, kit, qkv, seq=seq, lp=lp, topk=topk)

    h1 = _mix(x, y_attn, conv3, gates3, conv_w[0], w_attn_out[0].astype(bf16),
              w_conv_out[0].astype(bf16), w_out[0].astype(bf16))
    return _ffn(h1, norm_ffn_g[0][None], w_gate[0].astype(bf16), w_up[0].astype(bf16),
                w_down[0].astype(bf16), norm_final_g[None])
```

```python
import functools

import jax
import jax.numpy as jnp
from jax import lax
from jax.experimental import pallas as pl
from jax.experimental.pallas import tpu as pltpu

N_META = 16
N_HEADS = 8
HEAD_DIM = 128
ATTN_WIDTH = N_HEADS * HEAD_DIM
N_IDX_HEADS = 8
IDX_DIM = 64
TOPK_MAX = 256
CONV_K = 3
EPS = 1e-6
IDX_SCALE = (N_IDX_HEADS ** -0.5) * (IDX_DIM ** -0.5)

ROW_TILE = 256
FRONT_PAD = ROW_TILE - N_META
MAX_KV_TILES_PER_STEP = 13
BISECT_CAP = 32
NEG = -1e30
KEEP_ALL = 1e9
BELOW_ALL = -3e38
HALO_ROWS = 16
LANE_LIST = 12
LOG2E = 1.4426950408889634
VMEM_LIMIT = 58 * 1024 * 1024

f32 = jnp.float32
bf16 = jnp.bfloat16


def _rms(x, g):
    return (x * lax.rsqrt(jnp.mean(x * x, axis=-1, keepdims=True) + EPS)) * g


def _sigmoid(x):
    return 1.0 / (1.0 + jnp.exp(-x))


def _rms_proj_kernel(h_ref, g_ref, wq_ref, wc_ref, wg_ref, qs_ref, oq_ref, oc_ref, og_ref):
    a = _rms(h_ref[...], g_ref[...]).astype(bf16)
    q = jnp.dot(a, wq_ref[...], preferred_element_type=f32)
    oq_ref[...] = (q * qs_ref[...]).astype(oq_ref.dtype)
    oc_ref[...] = jnp.dot(a, wc_ref[...], preferred_element_type=f32).astype(oc_ref.dtype)
    og_ref[...] = jnp.dot(a, wg_ref[...], preferred_element_type=f32).astype(og_ref.dtype)


def _rms_proj(h2d, g, w_qkv, w_conv, w_gates, q_scale, tm):
    rows, d = h2d.shape
    const = lambda i: (0, 0)
    resident = lambda w: pl.BlockSpec(w.shape, const, pipeline_mode=pl.Buffered(1))
    widths = (w_qkv.shape[1], w_conv.shape[1], w_gates.shape[1])
    return pl.pallas_call(
        _rms_proj_kernel,
        grid=(rows // tm,),
        in_specs=[
            pl.BlockSpec((tm, d), lambda i: (i, 0)),
            pl.BlockSpec((1, d), const),
            resident(w_qkv), resident(w_conv), resident(w_gates),
            pl.BlockSpec((1, widths[0]), const),
        ],
        out_specs=[pl.BlockSpec((tm, n), lambda i: (i, 0)) for n in widths],
        out_shape=[jax.ShapeDtypeStruct((rows, n), bf16) for n in widths],
        compiler_params=pltpu.CompilerParams(
            dimension_semantics=("arbitrary",), vmem_limit_bytes=VMEM_LIMIT),
        name="rms_proj",
    )(h2d, g, w_qkv, w_conv, w_gates, q_scale)


def _rms_proj_idx_kernel(h_ref, g_ref, w_ref, qi_ref, kit_ref, wi_ref):
    a = _rms(h_ref[...], g_ref[...]).astype(bf16)
    y = jnp.dot(a, w_ref[...], preferred_element_type=f32)
    nqi = N_IDX_HEADS * IDX_DIM
    for h in range(N_IDX_HEADS):
        qi_ref[0, h] = y[:, h * IDX_DIM:(h + 1) * IDX_DIM].astype(bf16)
    kit_ref[0, 0] = y[:, nqi:nqi + 128].T[:IDX_DIM, :].astype(bf16)
    wi_ref[0] = y[:, nqi + IDX_DIM:nqi + IDX_DIM + N_IDX_HEADS]


def _rms_proj_idx(h2d, g, w, *, b, lp):
    d = h2d.shape[1]
    n = w.shape[1]
    tm = ROW_TILE
    nt = lp // tm
    return pl.pallas_call(
        _rms_proj_idx_kernel,
        grid=(b, nt),
        in_specs=[
            pl.BlockSpec((tm, d), lambda bb, i: (bb * nt + i, 0)),
            pl.BlockSpec((1, d), lambda bb, i: (0, 0)),
            pl.BlockSpec((d, n), lambda bb, i: (0, 0)),
        ],
        out_specs=[
            pl.BlockSpec((1, N_IDX_HEADS, tm, IDX_DIM), lambda bb, i: (bb, 0, i, 0)),
            pl.BlockSpec((1, 1, IDX_DIM, tm), lambda bb, i: (bb, i, 0, 0)),
            pl.BlockSpec((1, tm, N_IDX_HEADS), lambda bb, i: (bb, i, 0)),
        ],
        out_shape=[
            jax.ShapeDtypeStruct((b, N_IDX_HEADS, lp, IDX_DIM), bf16),
            jax.ShapeDtypeStruct((b, nt, IDX_DIM, tm), bf16),
            jax.ShapeDtypeStruct((b, lp, N_IDX_HEADS), f32),
        ],
        compiler_params=pltpu.CompilerParams(
            dimension_semantics=("arbitrary", "arbitrary"), vmem_limit_bytes=VMEM_LIMIT),
        name="rms_proj_idx",
    )(h2d, g, w)


def _attn_kernel(qi_ref, wi_ref, kit_ref, tri_ref, q_ref, k_ref, v_ref, o_ref,
                 s_scr, cand_scr, cand_t_scr, bias_scr, wb_scr, thr_scr,
                 need_scr, carry_scr, m_scr, l_scr, acc_scr, flag_scr, *, topk, g_tiles):
    tq = ROW_TILE
    kt = ROW_TILE
    rg = 128
    nh = kt // 128
    iq = pl.program_id(1)
    j = pl.program_id(2)
    nq = pl.num_programs(1) - 1
    nkb = pl.num_programs(2)
    i = iq + 1
    n_tiles = i + 1
    has_scores = iq < nq
    has_attn = iq >= 1
    t0 = j * g_tiles
    n_sc = jnp.where(has_scores, jnp.clip(n_tiles - t0, 0, g_tiles), 0)
    n_at = jnp.where(has_attn, jnp.clip(iq + 1 - t0, 0, g_tiles), 0)
    n_both = jnp.minimum(n_sc, n_at)
    kf = float(topk)

    def lanes_x(v):
        return jnp.concatenate([v] * nh, axis=1)

    def to_dense(rep):
        return jnp.concatenate(
            [rep[g * 128:(g + 1) * 128, :].T[0:8, :] for g in range(tq // 128)], axis=1)

    def to_rows(dense):
        return jnp.concatenate(
            [jnp.broadcast_to(dense[0:1, g * 128:(g + 1) * 128], (128, 128)).T
             for g in range(tq // 128)], axis=0)

    def reduce_scores(vec, fn, init, comb, red):
        vec_rows = to_rows(vec)
        outs = []
        for g in range(tq // rg):
            rows = slice(g * rg, (g + 1) * rg)
            bv = vec_rows[rows]

            def body(t, acc, rows=rows, bv=bv):
                for hl in range(nh):
                    acc = comb(acc, fn(s_scr[t, rows, hl * 128:(hl + 1) * 128], bv))
                return acc

            acc = lax.fori_loop(0, n_tiles, body, jnp.full((rg, 128), init, f32))
            outs.append(jnp.broadcast_to(red(acc, axis=1, keepdims=True), (rg, 128)))
        return to_dense(jnp.concatenate(outs, axis=0))

    def reduce_lists(vec, fn, init, comb, red):
        accs = [jnp.full((8, tq), init, f32) for _ in range(4)]
        for c in range(LANE_LIST * 128 // 8):
            accs[c % 4] = comb(accs[c % 4], fn(cand_t_scr[c * 8:(c + 1) * 8, :], vec))
        acc = comb(comb(accs[0], accs[1]), comb(accs[2], accs[3]))
        return jnp.broadcast_to(red(acc, axis=0, keepdims=True), (8, tq))

    def any_row(mask):
        return jnp.max(jnp.where(mask, 1.0, 0.0)) > 0.0

    @pl.when(jnp.logical_and(j == 0, has_attn))
    def _init_attention():
        m_scr[...] = jnp.full(m_scr.shape, NEG, f32)
        l_scr[...] = jnp.zeros(l_scr.shape, f32)
        acc_scr[...] = jnp.zeros(acc_scr.shape, f32)
        carry_scr[...] = jnp.zeros(carry_scr.shape, f32)

    @pl.when(jnp.logical_and(j == 0, has_scores))
    def _init_scores():
        w = wi_ref[0] * IDX_SCALE
        for h in range(N_IDX_HEADS):
            wb_scr[h] = jnp.broadcast_to(w[:, h:h + 1], (tq, 128))
        cand_scr[...] = jnp.full(cand_scr.shape, -jnp.inf, f32)

    def score_tile(s, causal):
        t = t0 + s
        kt_tile = kit_ref[0, t]
        acc = None
        for h in range(N_IDX_HEADS):
            x = jnp.dot(qi_ref[0, h], kt_tile, preferred_element_type=f32)
            term = jnp.maximum(x, 0.0) * lanes_x(wb_scr[h])
            acc = term if acc is None else acc + term
        kpos = t * kt + lax.broadcasted_iota(jnp.int32, (1, kt), 1)
        acc = acc + jnp.where(kpos >= FRONT_PAD, 0.0, -jnp.inf)
        if causal:
            qpos1 = i * tq + lax.broadcasted_iota(jnp.int32, (tq, 1), 0)
            acc = jnp.where(kpos <= qpos1, acc, -jnp.inf)
        s_scr[t] = acc
        for g in range(tq // 8):
            rows = slice(g * 8, (g + 1) * 8)
            lists = [cand_scr[k, rows, :] for k in range(LANE_LIST)]
            for hl in range(nh):
                x = acc[rows, hl * 128:(hl + 1) * 128]
                for k in range(LANE_LIST):
                    top = jnp.maximum(lists[k], x)
                    x = jnp.minimum(lists[k], x)
                    lists[k] = top
            for k in range(LANE_LIST):
                cand_scr[k, rows, :] = lists[k]

    def attend_mask(s, buf):
        t = t0 + s
        tie_path = flag_scr[0] > 0

        @pl.when(jnp.logical_not(tie_path))
        def _():
            thr = thr_scr[...]
            for hl in range(nh):
                lanes = slice(hl * 128, (hl + 1) * 128)
                bias_scr[buf, :, lanes] = jnp.where(s_scr[t, :, lanes] >= thr, 0.0, NEG)

        @pl.when(tie_path)
        def _():
            sc = s_scr[t]
            thr = lanes_x(thr_scr[...])
            tie = sc == thr
            counts = jnp.dot(jnp.where(tie, 1.0, 0.0).astype(bf16), tri_ref[...],
                             preferred_element_type=f32)
            seen = carry_scr[...]
            keep = (counts[:, :kt] + lanes_x(seen)) <= lanes_x(need_scr[...])
            bias_scr[buf] = jnp.where(sc > thr, 0.0,
                                      jnp.where(jnp.logical_and(tie, keep), 0.0, NEG))
            carry_scr[...] = seen + counts[:, kt:]

    def attend_heads(s, n):
        ones_cols = jnp.ones((n * kt, HEAD_DIM), bf16)
        row0 = pl.multiple_of(s * kt, kt)
        bias = jnp.concatenate([bias_scr[u] for u in range(n)], axis=1)
        for h in range(N_HEADS):
            cols = slice(h * HEAD_DIM, (h + 1) * HEAD_DIM)
            qh = q_ref[0, :, cols]
            kh = k_ref[0, pl.ds(row0, n * kt), cols]
            vh = v_ref[0, pl.ds(row0, n * kt), cols]
            lg = lax.dot_general(qh, kh, (((1,), (1,)), ((), ())),
                                 preferred_element_type=f32) + bias
            m_old = m_scr[h]
            m_new = jnp.maximum(m_old, jnp.max(lg, axis=1, keepdims=True))
            alpha = jnp.exp2(m_old - m_new)
            p = jnp.exp2(lg - jnp.concatenate([m_new] * (n * nh), axis=1))
            v_ext = jnp.concatenate([vh, ones_cols], axis=1)
            pv = jnp.dot(p.astype(bf16), v_ext, preferred_element_type=f32)
            l_scr[h] = alpha * l_scr[h] + pv[:, HEAD_DIM:]
            acc_scr[:, cols] = alpha * acc_scr[:, cols] + pv[:, :HEAD_DIM]
            m_scr[h] = m_new

    def pair_body(p, carry):
        attend_mask(2 * p, 0)
        attend_mask(2 * p + 1, 1)
        attend_heads(2 * p, 2)
        score_tile(2 * p, causal=False)
        score_tile(2 * p + 1, causal=False)
        return carry

    def both_body(s, carry):
        attend_mask(s, 0)
        attend_heads(s, 1)
        score_tile(s, causal=False)
        return carry

    def attend_body(s, carry):
        attend_mask(s, 0)
        attend_heads(s, 1)
        return carry

    def score_body(s, carry):
        score_tile(s, causal=True)
        return carry

    n_pairs = n_both // 2
    lax.fori_loop(0, n_pairs, pair_body, 0)
    lax.fori_loop(2 * n_pairs, n_both, both_body, 0)
    lax.fori_loop(n_both, n_at, attend_body, 0)
    lax.fori_loop(n_both, n_sc, score_body, 0)

    @pl.when(jnp.logical_and(has_scores, j == (n_tiles - 1) // g_tiles))
    def _threshold():
        qpos = i * tq + lax.broadcasted_iota(jnp.int32, (8, tq), 1)
        n_vis = (qpos - (FRONT_PAD - 1)).astype(f32)
        short = n_vis <= kf

        def search(reduce):
            def count_ge(v):
                return reduce(v, lambda blk, bv: jnp.where(blk >= bv, 1.0, 0.0), 0.0, jnp.add, jnp.sum)

            def count_gt(v):
                return reduce(v, lambda blk, bv: jnp.where(blk > bv, 1.0, 0.0), 0.0, jnp.add, jnp.sum)

            zero = jnp.zeros((8, tq), f32)
            cge0 = count_ge(zero)
            cgt0 = count_gt(zero)
            at_zero = jnp.logical_and(jnp.logical_not(short),
                                      jnp.logical_and(cgt0 < kf, cge0 >= kf))
            res0 = jnp.where(jnp.logical_or(short, at_zero), 1.0, 0.0)
            thr0 = jnp.where(short, BELOW_ALL, zero)
            need0 = jnp.where(jnp.logical_and(at_zero, cge0 > kf), kf - cgt0, KEEP_ALL)
            lo0 = jnp.where(cgt0 >= kf, jnp.maximum(lo_lists, zero), lo_lists)
            hi0 = jnp.where(cge0 < kf, jnp.minimum(hi_lists, zero), hi_lists)

            def bis_cond(st):
                it, _, _, _, _, active = st
                return jnp.logical_and(it < BISECT_CAP, jnp.max(active) > 0.0)

            def bis_body(st):
                it, lo, hi, thr, resolved, active = st
                piv = lo + (hi - lo) * 0.5
                splits = jnp.logical_and(piv > lo, piv < hi)
                c = count_ge(piv)
                upd = jnp.logical_and(active > 0.0, splits)
                hit = jnp.logical_and(upd, c == kf)
                thr = jnp.where(hit, piv, thr)
                resolved = jnp.where(hit, 1.0, resolved)
                lo = jnp.where(jnp.logical_and(upd, c > kf), piv, lo)
                hi = jnp.where(jnp.logical_and(upd, c < kf), piv, hi)
                active = jnp.where(jnp.logical_and(upd, jnp.logical_not(hit)), 1.0, 0.0)
                return it + 1, lo, hi, thr, resolved, active

            st = (jnp.int32(0), lo0, hi0, thr0, res0, 1.0 - res0)
            _, lo, hi, thr, resolved, _ = lax.while_loop(bis_cond, bis_body, st)

            thr_scr[...] = to_rows(thr)
            need_scr[...] = to_rows(need0)
            flag_scr[0] = jnp.where(any_row(need0 < KEEP_ALL), 1, 0).astype(jnp.int32)
            unresolved = resolved < 0.5

            @pl.when(any_row(unresolved))
            def _fallback():
                v0 = reduce(hi, lambda blk, bv: jnp.where(blk <= bv, blk, -jnp.inf),
                            -jnp.inf, jnp.maximum, jnp.max)

                def fb_cond(st):
                    _, cge, _ = st
                    return any_row(jnp.logical_and(unresolved, cge < kf))

                def fb_body(st):
                    v, cge, cgt = st
                    more = jnp.logical_and(unresolved, cge < kf)
                    v2 = reduce(v, lambda blk, bv: jnp.where(blk < bv, blk, -jnp.inf),
                                -jnp.inf, jnp.maximum, jnp.max)
                    return (jnp.where(more, v2, v), jnp.where(more, count_ge(v2), cge),
                            jnp.where(more, count_gt(v2), cgt))

                v, cge, cgt = lax.while_loop(fb_cond, fb_body, (v0, count_ge(v0), count_gt(v0)))
                need = kf - cgt
                partial = jnp.logical_and(unresolved, (cge - cgt) > need)
                thr_scr[...] = to_rows(jnp.where(unresolved, v, thr))
                need_scr[...] = to_rows(jnp.where(partial, need, need0))

                @pl.when(any_row(partial))
                def _():
                    flag_scr[0] = jnp.int32(1)

        for k in range(LANE_LIST):
            for g in range(tq // 128):
                cand_t_scr[k * 128:(k + 1) * 128, g * 128:(g + 1) * 128] = (
                    cand_scr[k, g * 128:(g + 1) * 128, :].T)

        second = cand_t_scr[128:256, :]
        lo_lists = jnp.broadcast_to(jnp.min(second, axis=0, keepdims=True), (8, tq))
        hi_lists = jnp.broadcast_to(jnp.max(second, axis=0, keepdims=True), (8, tq))

        search(reduce_lists)

        @pl.when(any_row(cand_scr[LANE_LIST - 1] >= thr_scr[...]))
        def _():
            search(reduce_scores)

    @pl.when(jnp.logical_and(has_attn, j == nkb - 1))
    def _finish():
        for h in range(N_HEADS):
            cols = slice(h * HEAD_DIM, (h + 1) * HEAD_DIM)
            o_ref[0, :, cols] = (acc_scr[:, cols] / l_scr[h]).astype(o_ref.dtype)


def _attention(qi4, wi, kit, qkv, *, seq, lp, topk):
    b = qkv.shape[0]
    tq = ROW_TILE
    nt = lp // tq
    g_tiles = max(g for g in range(1, MAX_KV_TILES_PER_STEP + 1) if nt % g == 0)
    nkb = nt // g_tiles
    nq = seq // tq

    def kv_block(iq, j):
        return jnp.minimum(j, iq // g_tiles)

    def score_tile_idx(iq):
        return jnp.minimum(iq + 1, nq)

    ka = lax.broadcasted_iota(jnp.int32, (tq, tq + 128), 0)
    kb = lax.broadcasted_iota(jnp.int32, (tq, tq + 128), 1)
    tri = jnp.where(jnp.logical_or(ka <= kb, kb >= tq), 1.0, 0.0).astype(bf16)

    kern = functools.partial(_attn_kernel, topk=topk, g_tiles=g_tiles)
    return pl.pallas_call(
        kern,
        grid=(b, nq + 1, nkb),
        in_specs=[
            pl.BlockSpec((1, N_IDX_HEADS, tq, IDX_DIM),
                         lambda bb, iq, j: (bb, 0, score_tile_idx(iq), 0)),
            pl.BlockSpec((1, tq, N_IDX_HEADS), lambda bb, iq, j: (bb, score_tile_idx(iq), 0)),
            pl.BlockSpec((1, nt, IDX_DIM, tq), lambda bb, iq, j: (bb, 0, 0, 0),
                         pipeline_mode=pl.Buffered(1)),
            pl.BlockSpec((tq, tq + 128), lambda bb, iq, j: (0, 0)),
            pl.BlockSpec((1, tq, ATTN_WIDTH), lambda bb, iq, j: (bb, iq, 0)),
            pl.BlockSpec((1, g_tiles * tq, ATTN_WIDTH), lambda bb, iq, j: (bb, kv_block(iq, j), 1)),
            pl.BlockSpec((1, g_tiles * tq, ATTN_WIDTH), lambda bb, iq, j: (bb, kv_block(iq, j), 2)),
        ],
        out_specs=pl.BlockSpec((1, tq, ATTN_WIDTH), lambda bb, iq, j: (bb, jnp.maximum(iq - 1, 0), 0)),
        out_shape=jax.ShapeDtypeStruct((b, seq, ATTN_WIDTH), bf16),
        scratch_shapes=[
            pltpu.VMEM((nt, tq, tq), f32),
            pltpu.VMEM((LANE_LIST, tq, 128), f32),
            pltpu.VMEM((LANE_LIST * 128, tq), f32),
            pltpu.VMEM((2, tq, tq), f32),
            pltpu.VMEM((N_IDX_HEADS, tq, 128), f32),
            pltpu.VMEM((tq, 128), f32),
            pltpu.VMEM((tq, 128), f32),
            pltpu.VMEM((tq, 128), f32),
            pltpu.VMEM((N_HEADS, tq, 128), f32),
            pltpu.VMEM((N_HEADS, tq, 128), f32),
            pltpu.VMEM((tq, ATTN_WIDTH), f32),
            pltpu.SMEM((1,), jnp.int32),
        ],
        compiler_params=pltpu.CompilerParams(
            dimension_semantics=("arbitrary", "arbitrary", "arbitrary"),
            vmem_limit_bytes=VMEM_LIMIT),
        name="dsa_attention",
    )(qi4, wi, kit, tri, qkv, qkv, qkv)


def _mix_kernel(x_ref, ya_ref, cv_ref, cvp_ref, gt_ref, cw_ref, wao_ref, wco_ref, wout_ref,
                o_ref, u_scr):
    tm = x_ref.shape[1]
    c = cw_ref.shape[1]
    hr = cvp_ref.shape[1]
    cu = cv_ref[0, :, 0:c].astype(f32)
    cb = cv_ref[0, :, c:2 * c].astype(f32)
    cc = cv_ref[0, :, 2 * c:3 * c].astype(f32)
    u_scr[0:hr, :] = cvp_ref[0, :, 2 * c:3 * c].astype(f32) * cvp_ref[0, :, 0:c].astype(f32)
    u_scr[hr:hr + tm, :] = cc * cu
    w = cw_ref[...]
    conv = (w[2:3] * u_scr[hr:hr + tm, :] + w[1:2] * u_scr[hr - 1:hr - 1 + tm, :]
            + w[0:1] * u_scr[hr - 2:hr - 2 + tm, :])
    y_conv = jnp.dot((cb * conv).astype(bf16), wco_ref[...], preferred_element_type=f32)
    y_attn = jnp.dot(ya_ref[0], wao_ref[...], preferred_element_type=f32)
    d = wout_ref.shape[0]
    mixed = (_sigmoid(gt_ref[0, :, 0:d].astype(f32)) * y_attn
             + _sigmoid(gt_ref[0, :, d:2 * d].astype(f32)) * y_conv)
    o_ref[0] = x_ref[0] + jnp.dot(mixed.astype(bf16), wout_ref[...], preferred_element_type=f32)


def _mix(x, y_attn, conv3, gates3, conv_w, wao, wco, wout):
    b, seq, d = x.shape
    tm = ROW_TILE
    c = conv_w.shape[1]
    const = lambda bb, i: (0, 0)
    return pl.pallas_call(
        _mix_kernel,
        grid=(b, seq // tm),
        in_specs=[
            pl.BlockSpec((1, tm, d), lambda bb, i: (bb, i, 0)),
            pl.BlockSpec((1, tm, ATTN_WIDTH), lambda bb, i: (bb, i, 0)),
            pl.BlockSpec((1, tm, 3 * c), lambda bb, i: (bb, i + 1, 0)),
            pl.BlockSpec((1, HALO_ROWS, 3 * c),
                         lambda bb, i: (bb, (i + 1) * (tm // HALO_ROWS) - 1, 0)),
            pl.BlockSpec((1, tm, 2 * d), lambda bb, i: (bb, i + 1, 0)),
            pl.BlockSpec((CONV_K, c), const),
            pl.BlockSpec(wao.shape, const),
            pl.BlockSpec(wco.shape, const),
            pl.BlockSpec(wout.shape, const),
        ],
        out_specs=pl.BlockSpec((1, tm, d), lambda bb, i: (bb, i, 0)),
        out_shape=jax.ShapeDtypeStruct((b, seq, d), f32),
        scratch_shapes=[pltpu.VMEM((tm + HALO_ROWS, c), f32)],
        compiler_params=pltpu.CompilerParams(
            dimension_semantics=("arbitrary", "arbitrary"), vmem_limit_bytes=VMEM_LIMIT),
        name="mix_merge",
    )(x, y_attn, conv3, conv3, gates3, conv_w, wao, wco, wout)


def _ffn_kernel(h_ref, g1_ref, wg_ref, wu_ref, wd_ref, g2_ref, o_ref):
    h = h_ref[0]
    f = _rms(h, g1_ref[...]).astype(bf16)
    gate = jnp.dot(f, wg_ref[...], preferred_element_type=f32)
    up = jnp.dot(f, wu_ref[...], preferred_element_type=f32)
    act = (gate * _sigmoid(gate)) * up
    h2 = h + jnp.dot(act.astype(bf16), wd_ref[...], preferred_element_type=f32)
    o_ref[0] = _rms(h2, g2_ref[...])


def _ffn(h1, g1, wg, wu, wd, g2):
    b, seq, d = h1.shape
    tm = ROW_TILE
    const = lambda bb, i: (0, 0)
    return pl.pallas_call(
        _ffn_kernel,
        grid=(b, seq // tm),
        in_specs=[
            pl.BlockSpec((1, tm, d), lambda bb, i: (bb, i, 0)),
            pl.BlockSpec((1, d), const),
            pl.BlockSpec(wg.shape, const),
            pl.BlockSpec(wu.shape, const),
            pl.BlockSpec(wd.shape, const),
            pl.BlockSpec((1, d), const),
        ],
        out_specs=pl.BlockSpec((1, tm, d), lambda bb, i: (bb, i, 0)),
        out_shape=jax.ShapeDtypeStruct((b, seq, d), f32),
        compiler_params=pltpu.CompilerParams(
            dimension_semantics=("arbitrary", "arbitrary"), vmem_limit_bytes=VMEM_LIMIT),
        name="ffn_final",
    )(h1, g1, wg, wu, wd, g2)


def kernel(x, meta_tokens, norm_mix_g, w_in, w_attn_out, conv_w, w_conv_out, w_out,
           norm_ffn_g, w_gate, w_up, w_down, norm_final_g):
    b, seq, d = x.shape
    assert w_in.shape[0] == 1, "single-layer block"
    assert seq % ROW_TILE == 0 and meta_tokens.shape[0] == N_META
    c = conv_w.shape[2]
    lp = ROW_TILE + seq
    l_real = N_META + seq
    topk = min(TOPK_MAX, l_real // 4)
    nt = lp // ROW_TILE

    meta = jnp.broadcast_to(meta_tokens[None].astype(x.dtype), (b, N_META, d))
    hp = jnp.concatenate([jnp.zeros((b, FRONT_PAD, d), x.dtype), meta, x], axis=1)
    h2d = hp.reshape(b * lp, d)

    w = w_in[0]
    o_idx = 3 * ATTN_WIDTH
    n_idx = N_IDX_HEADS * IDX_DIM + IDX_DIM + N_IDX_HEADS
    o_conv = o_idx + n_idx
    o_gate = o_conv + 3 * c
    n_idx_pad = N_IDX_HEADS * IDX_DIM + 128
    w_qkv = w[:, :o_idx].astype(bf16)
    w_idx = jnp.pad(w[:, o_idx:o_conv], ((0, 0), (0, n_idx_pad - n_idx))).astype(bf16)
    w_conv = w[:, o_conv:o_gate].astype(bf16)
    w_gates = w[:, o_gate:].astype(bf16)
    g_mix = norm_mix_g[0][None]

    tm = 512 if (b * lp) % 512 == 0 else ROW_TILE
    qkv_scale = jnp.concatenate([jnp.full((1, ATTN_WIDTH), LOG2E * HEAD_DIM ** -0.5, f32),
                                 jnp.ones((1, 2 * ATTN_WIDTH), f32)], axis=1)
    qkv, conv3, gates3 = _rms_proj(h2d, g_mix, w_qkv, w_conv, w_gates, qkv_scale, tm)
    qkv = qkv.reshape(b, lp, 3 * ATTN_WIDTH)
    conv3 = conv3.reshape(b, lp, 3 * c)
    gates3 = gates3.reshape(b, lp, 2 * d)
    qi4, kit, wi = _rms_proj_idx(h2d, g_mix, w_idx, b=b, lp=lp)

    y_attn = _attention(qi4, wi, kit, qkv, seq=seq, lp=lp, topk=topk)

    h1 = _mix(x, y_attn, conv3, gates3, conv_w[0], w_attn_out[0].astype(bf16),
              w_conv_out[0].astype(bf16), w_out[0].astype(bf16))
    return _ffn(h1, norm_ffn_g[0][None], w_gate[0].astype(bf16), w_up[0].astype(bf16),
                w_down[0].astype(bf16), norm_final_g[None])
```

```python
import functools

import jax
import jax.numpy as jnp
from jax import lax
from jax.experimental import pallas as pl
from jax.experimental.pallas import tpu as pltpu

N_META = 16
N_HEADS = 8
HEAD_DIM = 128
ATTN_WIDTH = N_HEADS * HEAD_DIM
N_IDX_HEADS = 8
IDX_DIM = 64
TOPK_MAX = 256
CONV_K = 3
EPS = 1e-6
IDX_SCALE = (N_IDX_HEADS ** -0.5) * (IDX_DIM ** -0.5)

ROW_TILE = 256
FRONT_PAD = ROW_TILE - N_META
MAX_KV_TILES_PER_STEP = 13
BISECT_CAP = 32
NEG = -1e30
KEEP_ALL = 1e9
BELOW_ALL = -3e38
HALO_ROWS = 16
LANE_LIST = 12
LOG2E = 1.4426950408889634
VMEM_LIMIT = 58 * 1024 * 1024

f32 = jnp.float32
bf16 = jnp.bfloat16


def _rms(x, g):
    return (x * lax.rsqrt(jnp.mean(x * x, axis=-1, keepdims=True) + EPS)) * g


def _sigmoid(x):
    return 1.0 / (1.0 + jnp.exp(-x))


def _rms_proj_kernel(h_ref, g_ref, wq_ref, wc_ref, wg_ref, qs_ref, oq_ref, oc_ref, og_ref):
    a = _rms(h_ref[...], g_ref[...]).astype(bf16)
    q = jnp.dot(a, wq_ref[...], preferred_element_type=f32)
    oq_ref[...] = (q * qs_ref[...]).astype(oq_ref.dtype)
    oc_ref[...] = jnp.dot(a, wc_ref[...], preferred_element_type=f32).astype(oc_ref.dtype)
    og_ref[...] = jnp.dot(a, wg_ref[...], preferred_element_type=f32).astype(og_ref.dtype)


def _rms_proj(h2d, g, w_qkv, w_conv, w_gates, q_scale, tm):
    rows, d = h2d.shape
    const = lambda i: (0, 0)
    resident = lambda w: pl.BlockSpec(w.shape, const, pipeline_mode=pl.Buffered(1))
    widths = (w_qkv.shape[1], w_conv.shape[1], w_gates.shape[1])
    return pl.pallas_call(
        _rms_proj_kernel,
        grid=(rows // tm,),
        in_specs=[
            pl.BlockSpec((tm, d), lambda i: (i, 0)),
            pl.BlockSpec((1, d), const),
            resident(w_qkv), resident(w_conv), resident(w_gates),
            pl.BlockSpec((1, widths[0]), const),
        ],
        out_specs=[pl.BlockSpec((tm, n), lambda i: (i, 0)) for n in widths],
        out_shape=[jax.ShapeDtypeStruct((rows, n), bf16) for n in widths],
        compiler_params=pltpu.CompilerParams(
            dimension_semantics=("arbitrary",), vmem_limit_bytes=VMEM_LIMIT),
        name="rms_proj",
    )(h2d, g, w_qkv, w_conv, w_gates, q_scale)


def _rms_proj_idx_kernel(h_ref, g_ref, w_ref, qi_ref, kit_ref, wi_ref):
    a = _rms(h_ref[...], g_ref[...]).astype(bf16)
    y = jnp.dot(a, w_ref[...], preferred_element_type=f32)
    nqi = N_IDX_HEADS * IDX_DIM
    for h in range(N_IDX_HEADS):
        qi_ref[0, h] = y[:, h * IDX_DIM:(h + 1) * IDX_DIM].astype(bf16)
    kit_ref[0, 0] = y[:, nqi:nqi + 128].T[:IDX_DIM, :].astype(bf16)
    wi_ref[0] = y[:, nqi + IDX_DIM:nqi + IDX_DIM + N_IDX_HEADS]


def _rms_proj_idx(h2d, g, w, *, b, lp):
    d = h2d.shape[1]
    n = w.shape[1]
    tm = ROW_TILE
    nt = lp // tm
    return pl.pallas_call(
        _rms_proj_idx_kernel,
        grid=(b, nt),
        in_specs=[
            pl.BlockSpec((tm, d), lambda bb, i: (bb * nt + i, 0)),
            pl.BlockSpec((1, d), lambda bb, i: (0, 0)),
            pl.BlockSpec((d, n), lambda bb, i: (0, 0)),
        ],
        out_specs=[
            pl.BlockSpec((1, N_IDX_HEADS, tm, IDX_DIM), lambda bb, i: (bb, 0, i, 0)),
            pl.BlockSpec((1, 1, IDX_DIM, tm), lambda bb, i: (bb, i, 0, 0)),
            pl.BlockSpec((1, tm, N_IDX_HEADS), lambda bb, i: (bb, i, 0)),
        ],
        out_shape=[
            jax.ShapeDtypeStruct((b, N_IDX_HEADS, lp, IDX_DIM), bf16),
            jax.ShapeDtypeStruct((b, nt, IDX_DIM, tm), bf16),
            jax.ShapeDtypeStruct((b, lp, N_IDX_HEADS), f32),
        ],
        compiler_params=pltpu.CompilerParams(
            dimension_semantics=("arbitrary", "arbitrary"), vmem_limit_bytes=VMEM_LIMIT),
        name="rms_proj_idx",
    )(h2d, g, w)


def _attn_kernel(qi_ref, wi_ref, kit_ref, tri_ref, q_ref, k_ref, v_ref, o_ref,
                 s_scr, cand_scr, cand_t_scr, bias_scr, wb_scr, thr_scr,
                 need_scr, m_scr, l_scr, acc_scr, flag_scr, *, topk, g_tiles):
    tq = ROW_TILE
    kt = ROW_TILE
    rg = 128
    nh = kt // 128
    iq = pl.program_id(1)
    j = pl.program_id(2)
    nq = pl.num_programs(1) - 1
    nkb = pl.num_programs(2)
    i = iq + 1
    n_tiles = i + 1
    has_scores = iq < nq
    has_attn = iq >= 1
    t0 = j * g_tiles
    n_sc = jnp.where(has_scores, jnp.clip(n_tiles - t0, 0, g_tiles), 0)
    n_at = jnp.where(has_attn, jnp.clip(iq + 1 - t0, 0, g_tiles), 0)
    n_both = jnp.minimum(n_sc, n_at)
    kf = float(topk)

    def lanes_x(v):
        return jnp.concatenate([v] * nh, axis=1)

    def to_dense(rep):
        return jnp.concatenate(
            [rep[g * 128:(g + 1) * 128, :].T[0:8, :] for g in range(tq // 128)], axis=1)

    def to_rows(dense):
        return jnp.concatenate(
            [jnp.broadcast_to(dense[0:1, g * 128:(g + 1) * 128], (128, 128)).T
             for g in range(tq // 128)], axis=0)

    def reduce_scores(vec, fn, init, comb, red):
        vec_rows = to_rows(vec)
        outs = []
        for g in range(tq // rg):
            rows = slice(g * rg, (g + 1) * rg)
            bv = vec_rows[rows]

            def body(t, acc, rows=rows, bv=bv):
                for hl in range(nh):
                    acc = comb(acc, fn(s_scr[t, rows, hl * 128:(hl + 1) * 128], bv))
                return acc

            acc = lax.fori_loop(0, n_tiles, body, jnp.full((rg, 128), init, f32))
            outs.append(jnp.broadcast_to(red(acc, axis=1, keepdims=True), (rg, 128)))
        return to_dense(jnp.concatenate(outs, axis=0))

    def reduce_lists(vec, fn, init, comb, red):
        accs = [jnp.full((8, tq), init, f32) for _ in range(4)]
        for c in range(LANE_LIST * 128 // 8):
            accs[c % 4] = comb(accs[c % 4], fn(cand_t_scr[c * 8:(c + 1) * 8, :], vec))
        acc = comb(comb(accs[0], accs[1]), comb(accs[2], accs[3]))
        return jnp.broadcast_to(red(acc, axis=0, keepdims=True), (8, tq))

    def any_row(mask):
        return jnp.max(jnp.where(mask, 1.0, 0.0)) > 0.0

    @pl.when(jnp.logical_and(j == 0, has_attn))
    def _init_attention():
        m_scr[...] = jnp.full(m_scr.shape, NEG, f32)
        l_scr[...] = jnp.zeros(l_scr.shape, f32)
        acc_scr[...] = jnp.zeros(acc_scr.shape, f32)

    @pl.when(jnp.logical_and(j == 0, has_scores))
    def _init_scores():
        w = wi_ref[0] * IDX_SCALE
        for h in range(N_IDX_HEADS):
            wb_scr[h] = jnp.broadcast_to(w[:, h:h + 1], (tq, 128))
        cand_scr[...] = jnp.full(cand_scr.shape, -jnp.inf, f32)

    def score_tile(s, causal):
        t = t0 + s
        kt_tile = kit_ref[0, t]
        acc = None
        for h in range(N_IDX_HEADS):
            x = jnp.dot(qi_ref[0, h], kt_tile, preferred_element_type=f32)
            term = jnp.maximum(x, 0.0) * lanes_x(wb_scr[h])
            acc = term if acc is None else acc + term
        kpos = t * kt + lax.broadcasted_iota(jnp.int32, (1, kt), 1)
        acc = acc + jnp.where(kpos >= FRONT_PAD, 0.0, -jnp.inf)
        if causal:
            qpos1 = i * tq + lax.broadcasted_iota(jnp.int32, (tq, 1), 0)
            acc = jnp.where(kpos <= qpos1, acc, -jnp.inf)
        s_scr[t] = acc
        for g in range(tq // 8):
            rows = slice(g * 8, (g + 1) * 8)
            lists = [cand_scr[k, rows, :] for k in range(LANE_LIST)]
            for hl in range(nh):
                x = acc[rows, hl * 128:(hl + 1) * 128]
                for k in range(LANE_LIST):
                    top = jnp.maximum(lists[k], x)
                    x = jnp.minimum(lists[k], x)
                    lists[k] = top
            for k in range(LANE_LIST):
                cand_scr[k, rows, :] = lists[k]

    def attend_mask(s, buf):
        t = t0 + s
        thr = thr_scr[...]
        for hl in range(nh):
            lanes = slice(hl * 128, (hl + 1) * 128)
            bias_scr[buf, :, lanes] = jnp.where(s_scr[t, :, lanes] >= thr, 0.0, NEG)

    def attend_heads(s, n):
        ones_cols = jnp.ones((n * kt, HEAD_DIM), bf16)
        row0 = pl.multiple_of(s * kt, kt)
        bias = jnp.concatenate([bias_scr[u] for u in range(n)], axis=1)
        for h in range(N_HEADS):
            cols = slice(h * HEAD_DIM, (h + 1) * HEAD_DIM)
            qh = q_ref[0, :, cols]
            kh = k_ref[0, pl.ds(row0, n * kt), cols]
            vh = v_ref[0, pl.ds(row0, n * kt), cols]
            lg = lax.dot_general(qh, kh, (((1,), (1,)), ((), ())),
                                 preferred_element_type=f32) + bias
            m_old = m_scr[h]
            m_new = jnp.maximum(m_old, jnp.max(lg, axis=1, keepdims=True))
            alpha = jnp.exp2(m_old - m_new)
            p = jnp.exp2(lg - jnp.concatenate([m_new] * (n * nh), axis=1))
            v_ext = jnp.concatenate([vh, ones_cols], axis=1)
            pv = jnp.dot(p.astype(bf16), v_ext, preferred_element_type=f32)
            l_scr[h] = alpha * l_scr[h] + pv[:, HEAD_DIM:]
            acc_scr[:, cols] = alpha * acc_scr[:, cols] + pv[:, :HEAD_DIM]
            m_scr[h] = m_new

    def pair_body(p, carry):
        attend_mask(2 * p, 0)
        attend_mask(2 * p + 1, 1)
        attend_heads(2 * p, 2)
        score_tile(2 * p, causal=False)
        score_tile(2 * p + 1, causal=False)
        return carry

    def both_body(s, carry):
        attend_mask(s, 0)
        attend_heads(s, 1)
        score_tile(s, causal=False)
        return carry

    def attend_body(s, carry):
        attend_mask(s, 0)
        attend_heads(s, 1)
        return carry

    def score_body(s, carry):
        score_tile(s, causal=True)
        return carry

    n_pairs = n_both // 2
    lax.fori_loop(0, n_pairs, pair_body, 0)
    lax.fori_loop(2 * n_pairs, n_both, both_body, 0)
    lax.fori_loop(n_both, n_at, attend_body, 0)
    lax.fori_loop(n_both, n_sc, score_body, 0)

    @pl.when(jnp.logical_and(has_scores, j == (n_tiles - 1) // g_tiles))
    def _threshold():
        qpos = i * tq + lax.broadcasted_iota(jnp.int32, (8, tq), 1)
        n_vis = (qpos - (FRONT_PAD - 1)).astype(f32)
        short = n_vis <= kf

        def search(reduce):
            def count_ge(v):
                return reduce(v, lambda blk, bv: jnp.where(blk >= bv, 1.0, 0.0), 0.0, jnp.add, jnp.sum)

            def count_gt(v):
                return reduce(v, lambda blk, bv: jnp.where(blk > bv, 1.0, 0.0), 0.0, jnp.add, jnp.sum)

            zero = jnp.zeros((8, tq), f32)
            cge0 = count_ge(zero)
            cgt0 = count_gt(zero)
            at_zero = jnp.logical_and(jnp.logical_not(short),
                                      jnp.logical_and(cgt0 < kf, cge0 >= kf))
            res0 = jnp.where(jnp.logical_or(short, at_zero), 1.0, 0.0)
            thr0 = jnp.where(short, BELOW_ALL, zero)
            need0 = jnp.where(jnp.logical_and(at_zero, cge0 > kf), kf - cgt0, KEEP_ALL)
            lo0 = jnp.where(cgt0 >= kf, jnp.maximum(lo_lists, zero), lo_lists)
            hi0 = jnp.where(cge0 < kf, jnp.minimum(hi_lists, zero), hi_lists)

            def bis_cond(st):
                it, _, _, _, _, active = st
                return jnp.logical_and(it < BISECT_CAP, jnp.max(active) > 0.0)

            def bis_body(st):
                it, lo, hi, thr, resolved, active = st
                piv = lo + (hi - lo) * 0.5
                splits = jnp.logical_and(piv > lo, piv < hi)
                c = count_ge(piv)
                upd = jnp.logical_and(active > 0.0, splits)
                hit = jnp.logical_and(upd, c == kf)
                thr = jnp.where(hit, piv, thr)
                resolved = jnp.where(hit, 1.0, resolved)
                lo = jnp.where(jnp.logical_and(upd, c > kf), piv, lo)
                hi = jnp.where(jnp.logical_and(upd, c < kf), piv, hi)
                active = jnp.where(jnp.logical_and(upd, jnp.logical_not(hit)), 1.0, 0.0)
                return it + 1, lo, hi, thr, resolved, active

            st = (jnp.int32(0), lo0, hi0, thr0, res0, 1.0 - res0)
            _, lo, hi, thr, resolved, _ = lax.while_loop(bis_cond, bis_body, st)

            thr_scr[...] = to_rows(thr)
            need_scr[...] = to_rows(need0)
            flag_scr[0] = jnp.where(any_row(need0 < KEEP_ALL), 1, 0).astype(jnp.int32)
            unresolved = resolved < 0.5

            @pl.when(any_row(unresolved))
            def _fallback():
                v0 = reduce(hi, lambda blk, bv: jnp.where(blk <= bv, blk, -jnp.inf),
                            -jnp.inf, jnp.maximum, jnp.max)

                def fb_cond(st):
                    _, cge, _ = st
                    return any_row(jnp.logical_and(unresolved, cge < kf))

                def fb_body(st):
                    v, cge, cgt = st
                    more = jnp.logical_and(unresolved, cge < kf)
                    v2 = reduce(v, lambda blk, bv: jnp.where(blk < bv, blk, -jnp.inf),
                                -jnp.inf, jnp.maximum, jnp.max)
                    return (jnp.where(more, v2, v), jnp.where(more, count_ge(v2), cge),
                            jnp.where(more, count_gt(v2), cgt))

                v, cge, cgt = lax.while_loop(fb_cond, fb_body, (v0, count_ge(v0), count_gt(v0)))
                need = kf - cgt
                partial = jnp.logical_and(unresolved, (cge - cgt) > need)
                thr_scr[...] = to_rows(jnp.where(unresolved, v, thr))
                need_scr[...] = to_rows(jnp.where(partial, need, need0))

                @pl.when(any_row(partial))
                def _():
                    flag_scr[0] = jnp.int32(1)

        for k in range(LANE_LIST):
            for g in range(tq // 128):
                cand_t_scr[k * 128:(k + 1) * 128, g * 128:(g + 1) * 128] = (
                    cand_scr[k, g * 128:(g + 1) * 128, :].T)

        second = cand_t_scr[128:256, :]
        lo_lists = jnp.broadcast_to(jnp.min(second, axis=0, keepdims=True), (8, tq))
        hi_lists = jnp.broadcast_to(jnp.max(second, axis=0, keepdims=True), (8, tq))

        search(reduce_lists)

        @pl.when(any_row(cand_scr[LANE_LIST - 1] >= thr_scr[...]))
        def _():
            search(reduce_scores)

        @pl.when(flag_scr[0] > 0)
        def _():
            thr = lanes_x(thr_scr[...])
            need = lanes_x(need_scr[...])

            def strike(t, seen):
                sc = s_scr[t]
                tie = sc == thr
                counts = jnp.dot(jnp.where(tie, 1.0, 0.0).astype(bf16), tri_ref[...],
                                 preferred_element_type=f32)
                over = (counts[:, :kt] + lanes_x(seen)) > need
                s_scr[t] = jnp.where(jnp.logical_and(tie, over), -jnp.inf, sc)
                return seen + counts[:, kt:]

            lax.fori_loop(0, n_tiles, strike, jnp.zeros((tq, 128), f32))

    @pl.when(jnp.logical_and(has_attn, j == nkb - 1))
    def _finish():
        for h in range(N_HEADS):
            cols = slice(h * HEAD_DIM, (h + 1) * HEAD_DIM)
            o_ref[0, :, cols] = (acc_scr[:, cols] / l_scr[h]).astype(o_ref.dtype)


def _attention(qi4, wi, kit, qkv, *, seq, lp, topk):
    b = qkv.shape[0]
    tq = ROW_TILE
    nt = lp // tq
    g_tiles = max(g for g in range(1, MAX_KV_TILES_PER_STEP + 1) if nt % g == 0)
    nkb = nt // g_tiles
    nq = seq // tq

    def kv_block(iq, j):
        return jnp.minimum(j, iq // g_tiles)

    def score_tile_idx(iq):
        return jnp.minimum(iq + 1, nq)

    ka = lax.broadcasted_iota(jnp.int32, (tq, tq + 128), 0)
    kb = lax.broadcasted_iota(jnp.int32, (tq, tq + 128), 1)
    tri = jnp.where(jnp.logical_or(ka <= kb, kb >= tq), 1.0, 0.0).astype(bf16)

    kern = functools.partial(_attn_kernel, topk=topk, g_tiles=g_tiles)
    return pl.pallas_call(
        kern,
        grid=(b, nq + 1, nkb),
        in_specs=[
            pl.BlockSpec((1, N_IDX_HEADS, tq, IDX_DIM),
                         lambda bb, iq, j: (bb, 0, score_tile_idx(iq), 0)),
            pl.BlockSpec((1, tq, N_IDX_HEADS), lambda bb, iq, j: (bb, score_tile_idx(iq), 0)),
            pl.BlockSpec((1, nt, IDX_DIM, tq), lambda bb, iq, j: (bb, 0, 0, 0),
                         pipeline_mode=pl.Buffered(1)),
            pl.BlockSpec((tq, tq + 128), lambda bb, iq, j: (0, 0)),
            pl.BlockSpec((1, tq, ATTN_WIDTH), lambda bb, iq, j: (bb, iq, 0)),
            pl.BlockSpec((1, g_tiles * tq, ATTN_WIDTH), lambda bb, iq, j: (bb, kv_block(iq, j), 1)),
            pl.BlockSpec((1, g_tiles * tq, ATTN_WIDTH), lambda bb, iq, j: (bb, kv_block(iq, j), 2)),
        ],
        out_specs=pl.BlockSpec((1, tq, ATTN_WIDTH), lambda bb, iq, j: (bb, jnp.maximum(iq - 1, 0), 0)),
        out_shape=jax.ShapeDtypeStruct((b, seq, ATTN_WIDTH), bf16),
        scratch_shapes=[
            pltpu.VMEM((nt, tq, tq), f32),
            pltpu.VMEM((LANE_LIST, tq, 128), f32),
            pltpu.VMEM((LANE_LIST * 128, tq), f32),
            pltpu.VMEM((2, tq, tq), f32),
            pltpu.VMEM((N_IDX_HEADS, tq, 128), f32),
            pltpu.VMEM((tq, 128), f32),
            pltpu.VMEM((tq, 128), f32),
            pltpu.VMEM((N_HEADS, tq, 128), f32),
            pltpu.VMEM((N_HEADS, tq, 128), f32),
            pltpu.VMEM((tq, ATTN_WIDTH), f32),
            pltpu.SMEM((1,), jnp.int32),
        ],
        compiler_params=pltpu.CompilerParams(
            dimension_semantics=("arbitrary", "arbitrary", "arbitrary"),
            vmem_limit_bytes=VMEM_LIMIT),
        name="dsa_attention",
    )(qi4, wi, kit, tri, qkv, qkv, qkv)


def _mix_kernel(x_ref, ya_ref, cv_ref, cvp_ref, gt_ref, cw_ref, wao_ref, wco_ref, wout_ref,
                o_ref, u_scr):
    tm = x_ref.shape[1]
    c = cw_ref.shape[1]
    hr = cvp_ref.shape[1]
    cu = cv_ref[0, :, 0:c].astype(f32)
    cb = cv_ref[0, :, c:2 * c].astype(f32)
    cc = cv_ref[0, :, 2 * c:3 * c].astype(f32)
    u_scr[0:hr, :] = cvp_ref[0, :, 2 * c:3 * c].astype(f32) * cvp_ref[0, :, 0:c].astype(f32)
    u_scr[hr:hr + tm, :] = cc * cu
    w = cw_ref[...]
    conv = (w[2:3] * u_scr[hr:hr + tm, :] + w[1:2] * u_scr[hr - 1:hr - 1 + tm, :]
            + w[0:1] * u_scr[hr - 2:hr - 2 + tm, :])
    y_conv = jnp.dot((cb * conv).astype(bf16), wco_ref[...], preferred_element_type=f32)
    y_attn = jnp.dot(ya_ref[0], wao_ref[...], preferred_element_type=f32)
    d = wout_ref.shape[0]
    mixed = (_sigmoid(gt_ref[0, :, 0:d].astype(f32)) * y_attn
             + _sigmoid(gt_ref[0, :, d:2 * d].astype(f32)) * y_conv)
    o_ref[0] = x_ref[0] + jnp.dot(mixed.astype(bf16), wout_ref[...], preferred_element_type=f32)


def _mix(x, y_attn, conv3, gates3, conv_w, wao, wco, wout):
    b, seq, d = x.shape
    tm = ROW_TILE
    c = conv_w.shape[1]
    const = lambda bb, i: (0, 0)
    return pl.pallas_call(
        _mix_kernel,
        grid=(b, seq // tm),
        in_specs=[
            pl.BlockSpec((1, tm, d), lambda bb, i: (bb, i, 0)),
            pl.BlockSpec((1, tm, ATTN_WIDTH), lambda bb, i: (bb, i, 0)),
            pl.BlockSpec((1, tm, 3 * c), lambda bb, i: (bb, i + 1, 0)),
            pl.BlockSpec((1, HALO_ROWS, 3 * c),
                         lambda bb, i: (bb, (i + 1) * (tm // HALO_ROWS) - 1, 0)),
            pl.BlockSpec((1, tm, 2 * d), lambda bb, i: (bb, i + 1, 0)),
            pl.BlockSpec((CONV_K, c), const),
            pl.BlockSpec(wao.shape, const),
            pl.BlockSpec(wco.shape, const),
            pl.BlockSpec(wout.shape, const),
        ],
        out_specs=pl.BlockSpec((1, tm, d), lambda bb, i: (bb, i, 0)),
        out_shape=jax.ShapeDtypeStruct((b, seq, d), f32),
        scratch_shapes=[pltpu.VMEM((tm + HALO_ROWS, c), f32)],
        compiler_params=pltpu.CompilerParams(
            dimension_semantics=("arbitrary", "arbitrary"), vmem_limit_bytes=VMEM_LIMIT),
        name="mix_merge",
    )(x, y_attn, conv3, conv3, gates3, conv_w, wao, wco, wout)


def _ffn_kernel(h_ref, g1_ref, wg_ref, wu_ref, wd_ref, g2_ref, o_ref):
    h = h_ref[0]
    f = _rms(h, g1_ref[...]).astype(bf16)
    gate = jnp.dot(f, wg_ref[...], preferred_element_type=f32)
    up = jnp.dot(f, wu_ref[...], preferred_element_type=f32)
    act = (gate * _sigmoid(gate)) * up
    h2 = h + jnp.dot(act.astype(bf16), wd_ref[...], preferred_element_type=f32)
    o_ref[0] = _rms(h2, g2_ref[...])


def _ffn(h1, g1, wg, wu, wd, g2):
    b, seq, d = h1.shape
    tm = ROW_TILE
    const = lambda bb, i: (0, 0)
    return pl.pallas_call(
        _ffn_kernel,
        grid=(b, seq // tm),
        in_specs=[
            pl.BlockSpec((1, tm, d), lambda bb, i: (bb, i, 0)),
            pl.BlockSpec((1, d), const),
            pl.BlockSpec(wg.shape, const),
            pl.BlockSpec(wu.shape, const),
            pl.BlockSpec(wd.shape, const),
            pl.BlockSpec((1, d), const),
        ],
        out_specs=pl.BlockSpec((1, tm, d), lambda bb, i: (bb, i, 0)),
        out_shape=jax.ShapeDtypeStruct((b, seq, d), f32),
        compiler_params=pltpu.CompilerParams(
            dimension_semantics=("arbitrary", "arbitrary"), vmem_limit_bytes=VMEM_LIMIT),
        name="ffn_final",
    )(h1, g1, wg, wu, wd, g2)


def kernel(x, meta_tokens, norm_mix_g, w_in, w_attn_out, conv_w, w_conv_out, w_out,
           norm_ffn_g, w_gate, w_up, w_down, norm_final_g):
    b, seq, d = x.shape
    assert w_in.shape[0] == 1, "single-layer block"
    assert seq % ROW_TILE == 0 and meta_tokens.shape[0] == N_META
    c = conv_w.shape[2]
    lp = ROW_TILE + seq
    l_real = N_META + seq
    topk = min(TOPK_MAX, l_real // 4)
    nt = lp // ROW_TILE

    meta = jnp.broadcast_to(meta_tokens[None].astype(x.dtype), (b, N_META, d))
    hp = jnp.concatenate([jnp.zeros((b, FRONT_PAD, d), x.dtype), meta, x], axis=1)
    h2d = hp.reshape(b * lp, d)

    w = w_in[0]
    o_idx = 3 * ATTN_WIDTH
    n_idx = N_IDX_HEADS * IDX_DIM + IDX_DIM + N_IDX_HEADS
    o_conv = o_idx + n_idx
    o_gate = o_conv + 3 * c
    n_idx_pad = N_IDX_HEADS * IDX_DIM + 128
    w_qkv = w[:, :o_idx].astype(bf16)
    w_idx = jnp.pad(w[:, o_idx:o_conv], ((0, 0), (0, n_idx_pad - n_idx))).astype(bf16)
    w_conv = w[:, o_conv:o_gate].astype(bf16)
    w_gates = w[:, o_gate:].astype(bf16)
    g_mix = norm_mix_g[0][None]

    tm = 512 if (b * lp) % 512 == 0 else ROW_TILE
    qkv_scale = jnp.concatenate([jnp.full((1, ATTN_WIDTH), LOG2E * HEAD_DIM ** -0.5, f32),
                                 jnp.ones((1, 2 * ATTN_WIDTH), f32)], axis=1)
    qkv, conv3, gates3 = _rms_proj(h2d, g_mix, w_qkv, w_conv, w_gates, qkv_scale, tm)
    qkv = qkv.reshape(b, lp, 3 * ATTN_WIDTH)
    conv3 = conv3.reshape(b, lp, 3 * c)
    gates3 = gates3.reshape(b, lp, 2 * d)
    qi4, kit, wi = _rms_proj_idx(h2d, g_mix, w_idx, b=b, lp=lp)

    y_attn = _attention(qi4, wi, kit, qkv, seq=seq, lp=lp, topk=topk)

    h1 = _mix(x, y_attn, conv3, gates3, conv_w[0], w_attn_out[0].astype(bf16),
              w_conv_out[0].astype(bf16), w_out[0].astype(bf16))
    return _ffn(h1, norm_ffn_g[0][None], w_gate[0].astype(bf16), w_up[0].astype(bf16),
                w_down[0].astype(bf16), norm_final_g[None])
```

```python
import functools

import jax
import jax.numpy as jnp
from jax import lax
from jax.experimental import pallas as pl
from jax.experimental.pallas import tpu as pltpu

N_META = 16
N_HEADS = 8
HEAD_DIM = 128
ATTN_WIDTH = N_HEADS * HEAD_DIM
N_IDX_HEADS = 8
IDX_DIM = 64
TOPK_MAX = 256
CONV_K = 3
EPS = 1e-6
IDX_SCALE = (N_IDX_HEADS ** -0.5) * (IDX_DIM ** -0.5)

ROW_TILE = 256
FRONT_PAD = ROW_TILE - N_META
MAX_KV_TILES_PER_STEP = 13
BISECT_CAP = 32
NEG = -1e30
KEEP_ALL = 1e9
BELOW_ALL = -3e38
HALO_ROWS = 16
LANE_LIST = 12
LOG2E = 1.4426950408889634
VMEM_LIMIT = 58 * 1024 * 1024

f32 = jnp.float32
bf16 = jnp.bfloat16


def _rms(x, g):
    return (x * lax.rsqrt(jnp.mean(x * x, axis=-1, keepdims=True) + EPS)) * g


def _sigmoid(x):
    return 1.0 / (1.0 + jnp.exp(-x))


def _rms_proj_kernel(h_ref, g_ref, wq_ref, wc_ref, wg_ref, qs_ref, oq_ref, oc_ref, og_ref):
    a = _rms(h_ref[...], g_ref[...]).astype(bf16)
    q = jnp.dot(a, wq_ref[...], preferred_element_type=f32)
    oq_ref[...] = (q * qs_ref[...]).astype(oq_ref.dtype)
    oc_ref[...] = jnp.dot(a, wc_ref[...], preferred_element_type=f32).astype(oc_ref.dtype)
    og_ref[...] = jnp.dot(a, wg_ref[...], preferred_element_type=f32).astype(og_ref.dtype)


def _rms_proj(h2d, g, w_qkv, w_conv, w_gates, q_scale, tm):
    rows, d = h2d.shape
    const = lambda i: (0, 0)
    resident = lambda w: pl.BlockSpec(w.shape, const, pipeline_mode=pl.Buffered(1))
    widths = (w_qkv.shape[1], w_conv.shape[1], w_gates.shape[1])
    return pl.pallas_call(
        _rms_proj_kernel,
        grid=(rows // tm,),
        in_specs=[
            pl.BlockSpec((tm, d), lambda i: (i, 0)),
            pl.BlockSpec((1, d), const),
            resident(w_qkv), resident(w_conv), resident(w_gates),
            pl.BlockSpec((1, widths[0]), const),
        ],
        out_specs=[pl.BlockSpec((tm, n), lambda i: (i, 0)) for n in widths],
        out_shape=[jax.ShapeDtypeStruct((rows, n), bf16) for n in widths],
        compiler_params=pltpu.CompilerParams(
            dimension_semantics=("arbitrary",), vmem_limit_bytes=VMEM_LIMIT),
        name="rms_proj",
    )(h2d, g, w_qkv, w_conv, w_gates, q_scale)


def _rms_proj_idx_kernel(h_ref, g_ref, w_ref, qi_ref, kit_ref, wi_ref):
    a = _rms(h_ref[...], g_ref[...]).astype(bf16)
    y = jnp.dot(a, w_ref[...], preferred_element_type=f32)
    nqi = N_IDX_HEADS * IDX_DIM
    for h in range(N_IDX_HEADS):
        qi_ref[0, h] = y[:, h * IDX_DIM:(h + 1) * IDX_DIM].astype(bf16)
    kit_ref[0, 0] = y[:, nqi:nqi + 128].T[:IDX_DIM, :].astype(bf16)
    wi_ref[0] = y[:, nqi + IDX_DIM:nqi + IDX_DIM + N_IDX_HEADS]


def _rms_proj_idx(h2d, g, w, *, b, lp):
    d = h2d.shape[1]
    n = w.shape[1]
    tm = ROW_TILE
    nt = lp // tm
    return pl.pallas_call(
        _rms_proj_idx_kernel,
        grid=(b, nt),
        in_specs=[
            pl.BlockSpec((tm, d), lambda bb, i: (bb * nt + i, 0)),
            pl.BlockSpec((1, d), lambda bb, i: (0, 0)),
            pl.BlockSpec((d, n), lambda bb, i: (0, 0)),
        ],
        out_specs=[
            pl.BlockSpec((1, N_IDX_HEADS, tm, IDX_DIM), lambda bb, i: (bb, 0, i, 0)),
            pl.BlockSpec((1, 1, IDX_DIM, tm), lambda bb, i: (bb, i, 0, 0)),
            pl.BlockSpec((1, tm, N_IDX_HEADS), lambda bb, i: (bb, i, 0)),
        ],
        out_shape=[
            jax.ShapeDtypeStruct((b, N_IDX_HEADS, lp, IDX_DIM), bf16),
            jax.ShapeDtypeStruct((b, nt, IDX_DIM, tm), bf16),
            jax.ShapeDtypeStruct((b, lp, N_IDX_HEADS), f32),
        ],
        compiler_params=pltpu.CompilerParams(
            dimension_semantics=("arbitrary", "arbitrary"), vmem_limit_bytes=VMEM_LIMIT),
        name="rms_proj_idx",
    )(h2d, g, w)


def _attn_kernel(qi_ref, wi_ref, kit_ref, tri_ref, q_ref, k_ref, v_ref, o_ref,
                 s_scr, cand_scr, cand_t_scr, bias_scr, wb_scr, thr_scr,
                 need_scr, m_scr, l_scr, acc_scr, flag_scr, *, topk, g_tiles):
    tq = ROW_TILE
    kt = ROW_TILE
    rg = 128
    nh = kt // 128
    iq = pl.program_id(1)
    j = pl.program_id(2)
    nq = pl.num_programs(1) - 1
    nkb = pl.num_programs(2)
    i = iq + 1
    n_tiles = i + 1
    has_scores = iq < nq
    has_attn = iq >= 1
    t0 = j * g_tiles
    n_sc = jnp.where(has_scores, jnp.clip(n_tiles - t0, 0, g_tiles), 0)
    n_at = jnp.where(has_attn, jnp.clip(iq + 1 - t0, 0, g_tiles), 0)
    n_both = jnp.minimum(n_sc, n_at)
    kf = float(topk)

    def lanes_x(v):
        return jnp.concatenate([v] * nh, axis=1)

    def to_dense(rep):
        return jnp.concatenate(
            [rep[g * 128:(g + 1) * 128, :].T[0:8, :] for g in range(tq // 128)], axis=1)

    def to_rows(dense):
        return jnp.concatenate(
            [jnp.broadcast_to(dense[0:1, g * 128:(g + 1) * 128], (128, 128)).T
             for g in range(tq // 128)], axis=0)

    def reduce_scores(vec, fn, init, comb, red):
        vec_rows = to_rows(vec)
        outs = []
        for g in range(tq // rg):
            rows = slice(g * rg, (g + 1) * rg)
            bv = vec_rows[rows]

            def body(t, acc, rows=rows, bv=bv):
                for hl in range(nh):
                    acc = comb(acc, fn(s_scr[t, rows, hl * 128:(hl + 1) * 128], bv))
                return acc

            acc = lax.fori_loop(0, n_tiles, body, jnp.full((rg, 128), init, f32))
            outs.append(jnp.broadcast_to(red(acc, axis=1, keepdims=True), (rg, 128)))
        return to_dense(jnp.concatenate(outs, axis=0))

    def reduce_lists(vec, fn, init, comb, red):
        accs = [jnp.full((8, tq), init, f32) for _ in range(4)]
        for c in range(LANE_LIST * 128 // 8):
            accs[c % 4] = comb(accs[c % 4], fn(cand_t_scr[c * 8:(c + 1) * 8, :], vec))
        acc = comb(comb(accs[0], accs[1]), comb(accs[2], accs[3]))
        return jnp.broadcast_to(red(acc, axis=0, keepdims=True), (8, tq))

    def any_row(mask):
        return jnp.max(jnp.where(mask, 1.0, 0.0)) > 0.0

    @pl.when(jnp.logical_and(j == 0, has_attn))
    def _init_attention():
        m_scr[...] = jnp.full(m_scr.shape, NEG, f32)
        l_scr[...] = jnp.zeros(l_scr.shape, f32)
        acc_scr[...] = jnp.zeros(acc_scr.shape, f32)

    @pl.when(jnp.logical_and(j == 0, has_scores))
    def _init_scores():
        w = wi_ref[0] * IDX_SCALE
        for h in range(N_IDX_HEADS):
            wb_scr[h] = jnp.broadcast_to(w[:, h:h + 1], (tq, 128))
        cand_scr[...] = jnp.full(cand_scr.shape, -jnp.inf, f32)

    def score_tile(s, causal):
        t = t0 + s
        kt_tile = kit_ref[0, t]
        acc = None
        for h in range(N_IDX_HEADS):
            x = jnp.dot(qi_ref[0, h], kt_tile, preferred_element_type=f32)
            term = jnp.maximum(x, 0.0) * lanes_x(wb_scr[h])
            acc = term if acc is None else acc + term
        kpos = t * kt + lax.broadcasted_iota(jnp.int32, (1, kt), 1)
        acc = acc + jnp.where(kpos >= FRONT_PAD, 0.0, -jnp.inf)
        if causal:
            qpos1 = i * tq + lax.broadcasted_iota(jnp.int32, (tq, 1), 0)
            acc = jnp.where(kpos <= qpos1, acc, -jnp.inf)
        s_scr[t] = acc
        for g in range(tq // 8):
            rows = slice(g * 8, (g + 1) * 8)
            lists = [cand_scr[k, rows, :] for k in range(LANE_LIST)]
            for hl in range(nh):
                x = acc[rows, hl * 128:(hl + 1) * 128]
                for k in range(LANE_LIST):
                    top = jnp.maximum(lists[k], x)
                    x = jnp.minimum(lists[k], x)
                    lists[k] = top
            for k in range(LANE_LIST):
                cand_scr[k, rows, :] = lists[k]

    def attend_mask(s, buf):
        t = t0 + s
        thr = thr_scr[...]
        for hl in range(nh):
            lanes = slice(hl * 128, (hl + 1) * 128)
            bias_scr[buf, :, lanes] = jnp.where(s_scr[t, :, lanes] >= thr, 0.0, NEG)

    def attend_heads(s, n):
        ones_cols = jnp.ones((n * kt, HEAD_DIM), bf16)
        row0 = pl.multiple_of(s * kt, kt)
        bias = jnp.concatenate([bias_scr[u] for u in range(n)], axis=1)
        for h in range(N_HEADS):
            cols = slice(h * HEAD_DIM, (h + 1) * HEAD_DIM)
            qh = q_ref[0, :, cols]
            kh = k_ref[0, pl.ds(row0, n * kt), cols]
            vh = v_ref[0, pl.ds(row0, n * kt), cols]
            lg = lax.dot_general(qh, kh, (((1,), (1,)), ((), ())),
                                 preferred_element_type=f32) + bias
            m_old = m_scr[h]
            m_new = jnp.maximum(m_old, jnp.max(lg, axis=1, keepdims=True))
            alpha = jnp.exp2(m_old - m_new)
            p = jnp.exp2(lg - jnp.concatenate([m_new] * (n * nh), axis=1))
            v_ext = jnp.concatenate([vh, ones_cols], axis=1)
            pv = jnp.dot(p.astype(bf16), v_ext, preferred_element_type=f32)
            l_scr[h] = alpha * l_scr[h] + pv[:, HEAD_DIM:]
            acc_scr[:, cols] = alpha * acc_scr[:, cols] + pv[:, :HEAD_DIM]
            m_scr[h] = m_new

    def pair_body(p, carry):
        attend_mask(2 * p, 0)
        attend_mask(2 * p + 1, 1)
        attend_heads(2 * p, 2)
        score_tile(2 * p, causal=False)
        score_tile(2 * p + 1, causal=False)
        return carry

    def both_body(s, carry):
        attend_mask(s, 0)
        attend_heads(s, 1)
        score_tile(s, causal=False)
        return carry

    def attend_body(s, carry):
        attend_mask(s, 0)
        attend_heads(s, 1)
        return carry

    def score_body(s, carry):
        score_tile(s, causal=True)
        return carry

    n_pairs = n_both // 2
    lax.fori_loop(0, n_pairs, pair_body, 0)
    lax.fori_loop(2 * n_pairs, n_both, both_body, 0)
    lax.fori_loop(n_both, n_at, attend_body, 0)
    lax.fori_loop(n_both, n_sc, score_body, 0)

    @pl.when(jnp.logical_and(has_scores, j == (n_tiles - 1) // g_tiles))
    def _threshold():
        qpos = i * tq + lax.broadcasted_iota(jnp.int32, (8, tq), 1)
        n_vis = (qpos - (FRONT_PAD - 1)).astype(f32)
        short = n_vis <= kf

        def search(reduce):
            def count_ge(v):
                return reduce(v, lambda blk, bv: jnp.where(blk >= bv, 1.0, 0.0), 0.0, jnp.add, jnp.sum)

            def count_gt(v):
                return reduce(v, lambda blk, bv: jnp.where(blk > bv, 1.0, 0.0), 0.0, jnp.add, jnp.sum)

            zero = jnp.zeros((8, tq), f32)
            cge0 = count_ge(zero)
            cgt0 = count_gt(zero)
            at_zero = jnp.logical_and(jnp.logical_not(short),
                                      jnp.logical_and(cgt0 < kf, cge0 >= kf))
            res0 = jnp.where(jnp.logical_or(short, at_zero), 1.0, 0.0)
            thr0 = jnp.where(short, BELOW_ALL, zero)
            need0 = jnp.where(jnp.logical_and(at_zero, cge0 > kf), kf - cgt0, KEEP_ALL)
            lo0 = jnp.where(cgt0 >= kf, jnp.maximum(lo_lists, zero), lo_lists)
            hi0 = jnp.where(cge0 < kf, jnp.minimum(hi_lists, zero), hi_lists)

            def bis_cond(st):
                it, _, _, _, _, active = st
                return jnp.logical_and(it < BISECT_CAP, jnp.max(active) > 0.0)

            def bis_step(st):
                it, lo, hi, thr, resolved, active = st
                piv = lo + (hi - lo) * 0.5
                splits = jnp.logical_and(piv > lo, piv < hi)
                c = count_ge(piv)
                upd = jnp.logical_and(active > 0.0, splits)
                hit = jnp.logical_and(upd, c == kf)
                thr = jnp.where(hit, piv, thr)
                resolved = jnp.where(hit, 1.0, resolved)
                lo = jnp.where(jnp.logical_and(upd, c > kf), piv, lo)
                hi = jnp.where(jnp.logical_and(upd, c < kf), piv, hi)
                active = jnp.where(jnp.logical_and(upd, jnp.logical_not(hit)), 1.0, 0.0)
                return it + 1, lo, hi, thr, resolved, active

            def bis_body(st):
                return bis_step(bis_step(st))

            st = (jnp.int32(0), lo0, hi0, thr0, res0, 1.0 - res0)
            _, lo, hi, thr, resolved, _ = lax.while_loop(bis_cond, bis_body, st)

            thr_scr[...] = to_rows(thr)
            need_scr[...] = to_rows(need0)
            flag_scr[0] = jnp.where(any_row(need0 < KEEP_ALL), 1, 0).astype(jnp.int32)
            unresolved = resolved < 0.5

            @pl.when(any_row(unresolved))
            def _fallback():
                v0 = reduce(hi, lambda blk, bv: jnp.where(blk <= bv, blk, -jnp.inf),
                            -jnp.inf, jnp.maximum, jnp.max)

                def fb_cond(st):
                    _, cge, _ = st
                    return any_row(jnp.logical_and(unresolved, cge < kf))

                def fb_body(st):
                    v, cge, cgt = st
                    more = jnp.logical_and(unresolved, cge < kf)
                    v2 = reduce(v, lambda blk, bv: jnp.where(blk < bv, blk, -jnp.inf),
                                -jnp.inf, jnp.maximum, jnp.max)
                    return (jnp.where(more, v2, v), jnp.where(more, count_ge(v2), cge),
                            jnp.where(more, count_gt(v2), cgt))

                v, cge, cgt = lax.while_loop(fb_cond, fb_body, (v0, count_ge(v0), count_gt(v0)))
                need = kf - cgt
                partial = jnp.logical_and(unresolved, (cge - cgt) > need)
                thr_scr[...] = to_rows(jnp.where(unresolved, v, thr))
                need_scr[...] = to_rows(jnp.where(partial, need, need0))

                @pl.when(any_row(partial))
                def _():
                    flag_scr[0] = jnp.int32(1)

        for k in range(LANE_LIST):
            for g in range(tq // 128):
                cand_t_scr[k * 128:(k + 1) * 128, g * 128:(g + 1) * 128] = (
                    cand_scr[k, g * 128:(g + 1) * 128, :].T)

        second = cand_t_scr[128:256, :]
        lo_lists = jnp.broadcast_to(jnp.min(second, axis=0, keepdims=True), (8, tq))
        hi_lists = jnp.broadcast_to(jnp.max(second, axis=0, keepdims=True), (8, tq))

        search(reduce_lists)

        @pl.when(any_row(cand_scr[LANE_LIST - 1] >= thr_scr[...]))
        def _():
            search(reduce_scores)

        @pl.when(flag_scr[0] > 0)
        def _():
            thr = lanes_x(thr_scr[...])
            need = lanes_x(need_scr[...])

            def strike(t, seen):
                sc = s_scr[t]
                tie = sc == thr
                counts = jnp.dot(jnp.where(tie, 1.0, 0.0).astype(bf16), tri_ref[...],
                                 preferred_element_type=f32)
                over = (counts[:, :kt] + lanes_x(seen)) > need
                s_scr[t] = jnp.where(jnp.logical_and(tie, over), -jnp.inf, sc)
                return seen + counts[:, kt:]

            lax.fori_loop(0, n_tiles, strike, jnp.zeros((tq, 128), f32))

    @pl.when(jnp.logical_and(has_attn, j == nkb - 1))
    def _finish():
        for h in range(N_HEADS):
            cols = slice(h * HEAD_DIM, (h + 1) * HEAD_DIM)
            o_ref[0, :, cols] = (acc_scr[:, cols] / l_scr[h]).astype(o_ref.dtype)


def _attention(qi4, wi, kit, qkv, *, seq, lp, topk):
    b = qkv.shape[0]
    tq = ROW_TILE
    nt = lp // tq
    g_tiles = max(g for g in range(1, MAX_KV_TILES_PER_STEP + 1) if nt % g == 0)
    nkb = nt // g_tiles
    nq = seq // tq

    def kv_block(iq, j):
        return jnp.minimum(j, iq // g_tiles)

    def score_tile_idx(iq):
        return jnp.minimum(iq + 1, nq)

    ka = lax.broadcasted_iota(jnp.int32, (tq, tq + 128), 0)
    kb = lax.broadcasted_iota(jnp.int32, (tq, tq + 128), 1)
    tri = jnp.where(jnp.logical_or(ka <= kb, kb >= tq), 1.0, 0.0).astype(bf16)

    kern = functools.partial(_attn_kernel, topk=topk, g_tiles=g_tiles)
    return pl.pallas_call(
        kern,
        grid=(b, nq + 1, nkb),
        in_specs=[
            pl.BlockSpec((1, N_IDX_HEADS, tq, IDX_DIM),
                         lambda bb, iq, j: (bb, 0, score_tile_idx(iq), 0)),
            pl.BlockSpec((1, tq, N_IDX_HEADS), lambda bb, iq, j: (bb, score_tile_idx(iq), 0)),
            pl.BlockSpec((1, nt, IDX_DIM, tq), lambda bb, iq, j: (bb, 0, 0, 0),
                         pipeline_mode=pl.Buffered(1)),
            pl.BlockSpec((tq, tq + 128), lambda bb, iq, j: (0, 0)),
            pl.BlockSpec((1, tq, ATTN_WIDTH), lambda bb, iq, j: (bb, iq, 0)),
            pl.BlockSpec((1, g_tiles * tq, ATTN_WIDTH), lambda bb, iq, j: (bb, kv_block(iq, j), 1)),
            pl.BlockSpec((1, g_tiles * tq, ATTN_WIDTH), lambda bb, iq, j: (bb, kv_block(iq, j), 2)),
        ],
        out_specs=pl.BlockSpec((1, tq, ATTN_WIDTH), lambda bb, iq, j: (bb, jnp.maximum(iq - 1, 0), 0)),
        out_shape=jax.ShapeDtypeStruct((b, seq, ATTN_WIDTH), bf16),
        scratch_shapes=[
            pltpu.VMEM((nt, tq, tq), f32),
            pltpu.VMEM((LANE_LIST, tq, 128), f32),
            pltpu.VMEM((LANE_LIST * 128, tq), f32),
            pltpu.VMEM((2, tq, tq), f32),
            pltpu.VMEM((N_IDX_HEADS, tq, 128), f32),
            pltpu.VMEM((tq, 128), f32),
            pltpu.VMEM((tq, 128), f32),
            pltpu.VMEM((N_HEADS, tq, 128), f32),
            pltpu.VMEM((N_HEADS, tq, 128), f32),
            pltpu.VMEM((tq, ATTN_WIDTH), f32),
            pltpu.SMEM((1,), jnp.int32),
        ],
        compiler_params=pltpu.CompilerParams(
            dimension_semantics=("arbitrary", "arbitrary", "arbitrary"),
            vmem_limit_bytes=VMEM_LIMIT),
        name="dsa_attention",
    )(qi4, wi, kit, tri, qkv, qkv, qkv)


def _mix_kernel(x_ref, ya_ref, cv_ref, cvp_ref, gt_ref, cw_ref, wao_ref, wco_ref, wout_ref,
                o_ref, u_scr):
    tm = x_ref.shape[1]
    c = cw_ref.shape[1]
    hr = cvp_ref.shape[1]
    cu = cv_ref[0, :, 0:c].astype(f32)
    cb = cv_ref[0, :, c:2 * c].astype(f32)
    cc = cv_ref[0, :, 2 * c:3 * c].astype(f32)
    u_scr[0:hr, :] = cvp_ref[0, :, 2 * c:3 * c].astype(f32) * cvp_ref[0, :, 0:c].astype(f32)
    u_scr[hr:hr + tm, :] = cc * cu
    w = cw_ref[...]
    conv = (w[2:3] * u_scr[hr:hr + tm, :] + w[1:2] * u_scr[hr - 1:hr - 1 + tm, :]
            + w[0:1] * u_scr[hr - 2:hr - 2 + tm, :])
    y_conv = jnp.dot((cb * conv).astype(bf16), wco_ref[...], preferred_element_type=f32)
    y_attn = jnp.dot(ya_ref[0], wao_ref[...], preferred_element_type=f32)
    d = wout_ref.shape[0]
    mixed = (_sigmoid(gt_ref[0, :, 0:d].astype(f32)) * y_attn
             + _sigmoid(gt_ref[0, :, d:2 * d].astype(f32)) * y_conv)
    o_ref[0] = x_ref[0] + jnp.dot(mixed.astype(bf16), wout_ref[...], preferred_element_type=f32)


def _mix(x, y_attn, conv3, gates3, conv_w, wao, wco, wout):
    b, seq, d = x.shape
    tm = ROW_TILE
    c = conv_w.shape[1]
    const = lambda bb, i: (0, 0)
    return pl.pallas_call(
        _mix_kernel,
        grid=(b, seq // tm),
        in_specs=[
            pl.BlockSpec((1, tm, d), lambda bb, i: (bb, i, 0)),
            pl.BlockSpec((1, tm, ATTN_WIDTH), lambda bb, i: (bb, i, 0)),
            pl.BlockSpec((1, tm, 3 * c), lambda bb, i: (bb, i + 1, 0)),
            pl.BlockSpec((1, HALO_ROWS, 3 * c),
                         lambda bb, i: (bb, (i + 1) * (tm // HALO_ROWS) - 1, 0)),
            pl.BlockSpec((1, tm, 2 * d), lambda bb, i: (bb, i + 1, 0)),
            pl.BlockSpec((CONV_K, c), const),
            pl.BlockSpec(wao.shape, const),
            pl.BlockSpec(wco.shape, const),
            pl.BlockSpec(wout.shape, const),
        ],
        out_specs=pl.BlockSpec((1, tm, d), lambda bb, i: (bb, i, 0)),
        out_shape=jax.ShapeDtypeStruct((b, seq, d), f32),
        scratch_shapes=[pltpu.VMEM((tm + HALO_ROWS, c), f32)],
        compiler_params=pltpu.CompilerParams(
            dimension_semantics=("arbitrary", "arbitrary"), vmem_limit_bytes=VMEM_LIMIT),
        name="mix_merge",
    )(x, y_attn, conv3, conv3, gates3, conv_w, wao, wco, wout)


def _ffn_kernel(h_ref, g1_ref, wg_ref, wu_ref, wd_ref, g2_ref, o_ref):
    h = h_ref[0]
    f = _rms(h, g1_ref[...]).astype(bf16)
    gate = jnp.dot(f, wg_ref[...], preferred_element_type=f32)
    up = jnp.dot(f, wu_ref[...], preferred_element_type=f32)
    act = (gate * _sigmoid(gate)) * up
    h2 = h + jnp.dot(act.astype(bf16), wd_ref[...], preferred_element_type=f32)
    o_ref[0] = _rms(h2, g2_ref[...])


def _ffn(h1, g1, wg, wu, wd, g2):
    b, seq, d = h1.shape
    tm = 2 * ROW_TILE if seq % (2 * ROW_TILE) == 0 else ROW_TILE
    const = lambda bb, i: (0, 0)
    resident = lambda w: pl.BlockSpec(w.shape, const, pipeline_mode=pl.Buffered(1))
    return pl.pallas_call(
        _ffn_kernel,
        grid=(b, seq // tm),
        in_specs=[
            pl.BlockSpec((1, tm, d), lambda bb, i: (bb, i, 0)),
            pl.BlockSpec((1, d), const),
            resident(wg), resident(wu), resident(wd),
            pl.BlockSpec((1, d), const),
        ],
        out_specs=pl.BlockSpec((1, tm, d), lambda bb, i: (bb, i, 0)),
        out_shape=jax.ShapeDtypeStruct((b, seq, d), f32),
        compiler_params=pltpu.CompilerParams(
            dimension_semantics=("arbitrary", "arbitrary"), vmem_limit_bytes=VMEM_LIMIT),
        name="ffn_final",
    )(h1, g1, wg, wu, wd, g2)


def kernel(x, meta_tokens, norm_mix_g, w_in, w_attn_out, conv_w, w_conv_out, w_out,
           norm_ffn_g, w_gate, w_up, w_down, norm_final_g):
    b, seq, d = x.shape
    assert w_in.shape[0] == 1, "single-layer block"
    assert seq % ROW_TILE == 0 and meta_tokens.shape[0] == N_META
    c = conv_w.shape[2]
    lp = ROW_TILE + seq
    l_real = N_META + seq
    topk = min(TOPK_MAX, l_real // 4)
    nt = lp // ROW_TILE

    meta = jnp.broadcast_to(meta_tokens[None].astype(x.dtype), (b, N_META, d))
    hp = jnp.concatenate([jnp.zeros((b, FRONT_PAD, d), x.dtype), meta, x], axis=1)
    h2d = hp.reshape(b * lp, d)

    w = w_in[0]
    o_idx = 3 * ATTN_WIDTH
    n_idx = N_IDX_HEADS * IDX_DIM + IDX_DIM + N_IDX_HEADS
    o_conv = o_idx + n_idx
    o_gate = o_conv + 3 * c
    n_idx_pad = N_IDX_HEADS * IDX_DIM + 128
    w_qkv = w[:, :o_idx].astype(bf16)
    w_idx = jnp.pad(w[:, o_idx:o_conv], ((0, 0), (0, n_idx_pad - n_idx))).astype(bf16)
    w_conv = w[:, o_conv:o_gate].astype(bf16)
    w_gates = w[:, o_gate:].astype(bf16)
    g_mix = norm_mix_g[0][None]

    tm = 512 if (b * lp) % 512 == 0 else ROW_TILE
    qkv_scale = jnp.concatenate([jnp.full((1, ATTN_WIDTH), LOG2E * HEAD_DIM ** -0.5, f32),
                                 jnp.ones((1, 2 * ATTN_WIDTH), f32)], axis=1)
    qkv, conv3, gates3 = _rms_proj(h2d, g_mix, w_qkv, w_conv, w_gates, qkv_scale, tm)
    qkv = qkv.reshape(b, lp, 3 * ATTN_WIDTH)
    conv3 = conv3.reshape(b, lp, 3 * c)
    gates3 = gates3.reshape(b, lp, 2 * d)
    qi4, kit, wi = _rms_proj_idx(h2d, g_mix, w_idx, b=b, lp=lp)

    y_attn = _attention(qi4, wi, kit, qkv, seq=seq, lp=lp, topk=topk)

    h1 = _mix(x, y_attn, conv3, gates3, conv_w[0], w_attn_out[0].astype(bf16),
              w_conv_out[0].astype(bf16), w_out[0].astype(bf16))
    return _ffn(h1, norm_ffn_g[0][None], w_gate[0].astype(bf16), w_up[0].astype(bf16),
                w_down[0].astype(bf16), norm_final_g[None])
```

```python
import functools

import jax
import jax.numpy as jnp
from jax import lax
from jax.experimental import pallas as pl
from jax.experimental.pallas import tpu as pltpu

N_META = 16
N_HEADS = 8
HEAD_DIM = 128
ATTN_WIDTH = N_HEADS * HEAD_DIM
N_IDX_HEADS = 8
IDX_DIM = 64
TOPK_MAX = 256
CONV_K = 3
EPS = 1e-6
IDX_SCALE = (N_IDX_HEADS ** -0.5) * (IDX_DIM ** -0.5)

ROW_TILE = 256
FRONT_PAD = ROW_TILE - N_META
MAX_KV_TILES_PER_STEP = 13
BISECT_CAP = 32
BISECT_UNTESTED = 12
NEG = -1e30
KEEP_ALL = 1e9
BELOW_ALL = -3e38
HALO_ROWS = 16
LANE_LIST = 12
LOG2E = 1.4426950408889634
VMEM_LIMIT = 58 * 1024 * 1024

f32 = jnp.float32
bf16 = jnp.bfloat16


def _rms(x, g):
    return (x * lax.rsqrt(jnp.mean(x * x, axis=-1, keepdims=True) + EPS)) * g


def _sigmoid(x):
    return 1.0 / (1.0 + jnp.exp(-x))


def _rms_proj_kernel(h_ref, g_ref, wq_ref, wc_ref, wg_ref, qs_ref, oq_ref, oc_ref, og_ref):
    a = _rms(h_ref[...], g_ref[...]).astype(bf16)
    q = jnp.dot(a, wq_ref[...], preferred_element_type=f32)
    oq_ref[...] = (q * qs_ref[...]).astype(oq_ref.dtype)
    oc_ref[...] = jnp.dot(a, wc_ref[...], preferred_element_type=f32).astype(oc_ref.dtype)
    og_ref[...] = jnp.dot(a, wg_ref[...], preferred_element_type=f32).astype(og_ref.dtype)


def _rms_proj(h2d, g, w_qkv, w_conv, w_gates, q_scale, tm):
    rows, d = h2d.shape
    const = lambda i: (0, 0)
    resident = lambda w: pl.BlockSpec(w.shape, const, pipeline_mode=pl.Buffered(1))
    widths = (w_qkv.shape[1], w_conv.shape[1], w_gates.shape[1])
    return pl.pallas_call(
        _rms_proj_kernel,
        grid=(rows // tm,),
        in_specs=[
            pl.BlockSpec((tm, d), lambda i: (i, 0)),
            pl.BlockSpec((1, d), const),
            resident(w_qkv), resident(w_conv), resident(w_gates),
            pl.BlockSpec((1, widths[0]), const),
        ],
        out_specs=[pl.BlockSpec((tm, n), lambda i: (i, 0)) for n in widths],
        out_shape=[jax.ShapeDtypeStruct((rows, n), bf16) for n in widths],
        compiler_params=pltpu.CompilerParams(
            dimension_semantics=("arbitrary",), vmem_limit_bytes=VMEM_LIMIT),
        name="rms_proj",
    )(h2d, g, w_qkv, w_conv, w_gates, q_scale)


def _rms_proj_idx_kernel(h_ref, g_ref, w_ref, qi_ref, kit_ref, wi_ref):
    a = _rms(h_ref[...], g_ref[...]).astype(bf16)
    y = jnp.dot(a, w_ref[...], preferred_element_type=f32)
    nqi = N_IDX_HEADS * IDX_DIM
    for h in range(N_IDX_HEADS):
        qi_ref[0, h] = y[:, h * IDX_DIM:(h + 1) * IDX_DIM].astype(bf16)
    kit_ref[0, 0] = y[:, nqi:nqi + 128].T[:IDX_DIM, :].astype(bf16)
    wi_ref[0] = y[:, nqi + IDX_DIM:nqi + IDX_DIM + N_IDX_HEADS]


def _rms_proj_idx(h2d, g, w, *, b, lp):
    d = h2d.shape[1]
    n = w.shape[1]
    tm = ROW_TILE
    nt = lp // tm
    return pl.pallas_call(
        _rms_proj_idx_kernel,
        grid=(b, nt),
        in_specs=[
            pl.BlockSpec((tm, d), lambda bb, i: (bb * nt + i, 0)),
            pl.BlockSpec((1, d), lambda bb, i: (0, 0)),
            pl.BlockSpec((d, n), lambda bb, i: (0, 0)),
        ],
        out_specs=[
            pl.BlockSpec((1, N_IDX_HEADS, tm, IDX_DIM), lambda bb, i: (bb, 0, i, 0)),
            pl.BlockSpec((1, 1, IDX_DIM, tm), lambda bb, i: (bb, i, 0, 0)),
            pl.BlockSpec((1, tm, N_IDX_HEADS), lambda bb, i: (bb, i, 0)),
        ],
        out_shape=[
            jax.ShapeDtypeStruct((b, N_IDX_HEADS, lp, IDX_DIM), bf16),
            jax.ShapeDtypeStruct((b, nt, IDX_DIM, tm), bf16),
            jax.ShapeDtypeStruct((b, lp, N_IDX_HEADS), f32),
        ],
        compiler_params=pltpu.CompilerParams(
            dimension_semantics=("arbitrary", "arbitrary"), vmem_limit_bytes=VMEM_LIMIT),
        name="rms_proj_idx",
    )(h2d, g, w)


def _attn_kernel(qi_ref, wi_ref, kit_ref, tri_ref, q_ref, k_ref, v_ref, o_ref,
                 s_scr, cand_scr, cand_t_scr, bias_scr, wb_scr, thr_scr,
                 need_scr, m_scr, l_scr, acc_scr, flag_scr, *, topk, g_tiles):
    tq = ROW_TILE
    kt = ROW_TILE
    rg = 128
    nh = kt // 128
    iq = pl.program_id(1)
    j = pl.program_id(2)
    nq = pl.num_programs(1) - 1
    nkb = pl.num_programs(2)
    i = iq + 1
    n_tiles = i + 1
    has_scores = iq < nq
    has_attn = iq >= 1
    t0 = j * g_tiles
    n_sc = jnp.where(has_scores, jnp.clip(n_tiles - t0, 0, g_tiles), 0)
    n_at = jnp.where(has_attn, jnp.clip(iq + 1 - t0, 0, g_tiles), 0)
    n_both = jnp.minimum(n_sc, n_at)
    kf = float(topk)

    def lanes_x(v):
        return jnp.concatenate([v] * nh, axis=1)

    def to_dense(rep):
        return jnp.concatenate(
            [rep[g * 128:(g + 1) * 128, :].T[0:8, :] for g in range(tq // 128)], axis=1)

    def to_rows(dense):
        return jnp.concatenate(
            [jnp.broadcast_to(dense[0:1, g * 128:(g + 1) * 128], (128, 128)).T
             for g in range(tq // 128)], axis=0)

    def reduce_scores(vec, fn, init, comb, red):
        vec_rows = to_rows(vec)
        outs = []
        for g in range(tq // rg):
            rows = slice(g * rg, (g + 1) * rg)
            bv = vec_rows[rows]

            def body(t, acc, rows=rows, bv=bv):
                for hl in range(nh):
                    acc = comb(acc, fn(s_scr[t, rows, hl * 128:(hl + 1) * 128], bv))
                return acc

            acc = lax.fori_loop(0, n_tiles, body, jnp.full((rg, 128), init, f32))
            outs.append(jnp.broadcast_to(red(acc, axis=1, keepdims=True), (rg, 128)))
        return to_dense(jnp.concatenate(outs, axis=0))

    def reduce_lists(vec, fn, init, comb, red):
        accs = [jnp.full((8, tq), init, f32) for _ in range(4)]
        for c in range(LANE_LIST * 128 // 8):
            accs[c % 4] = comb(accs[c % 4], fn(cand_t_scr[c * 8:(c + 1) * 8, :], vec))
        acc = comb(comb(accs[0], accs[1]), comb(accs[2], accs[3]))
        return jnp.broadcast_to(red(acc, axis=0, keepdims=True), (8, tq))

    def any_row(mask):
        return jnp.max(jnp.where(mask, 1.0, 0.0)) > 0.0

    @pl.when(jnp.logical_and(j == 0, has_attn))
    def _init_attention():
        m_scr[...] = jnp.full(m_scr.shape, NEG, f32)
        l_scr[...] = jnp.zeros(l_scr.shape, f32)
        acc_scr[...] = jnp.zeros(acc_scr.shape, f32)

    @pl.when(jnp.logical_and(j == 0, has_scores))
    def _init_scores():
        w = wi_ref[0] * IDX_SCALE
        for h in range(N_IDX_HEADS):
            wb_scr[h] = jnp.broadcast_to(w[:, h:h + 1], (tq, 128))
        cand_scr[...] = jnp.full(cand_scr.shape, -jnp.inf, f32)

    def score_tile(s, causal):
        t = t0 + s
        kt_tile = kit_ref[0, t]
        acc = None
        for h in range(N_IDX_HEADS):
            x = jnp.dot(qi_ref[0, h], kt_tile, preferred_element_type=f32)
            term = jnp.maximum(x, 0.0) * lanes_x(wb_scr[h])
            acc = term if acc is None else acc + term
        kpos = t * kt + lax.broadcasted_iota(jnp.int32, (1, kt), 1)
        acc = acc + jnp.where(kpos >= FRONT_PAD, 0.0, -jnp.inf)
        if causal:
            qpos1 = i * tq + lax.broadcasted_iota(jnp.int32, (tq, 1), 0)
            acc = jnp.where(kpos <= qpos1, acc, -jnp.inf)
        s_scr[t] = acc
        for g in range(tq // 8):
            rows = slice(g * 8, (g + 1) * 8)
            lists = [cand_scr[k, rows, :] for k in range(LANE_LIST)]
            for hl in range(nh):
                x = acc[rows, hl * 128:(hl + 1) * 128]
                for k in range(LANE_LIST):
                    top = jnp.maximum(lists[k], x)
                    x = jnp.minimum(lists[k], x)
                    lists[k] = top
            for k in range(LANE_LIST):
                cand_scr[k, rows, :] = lists[k]

    def attend_mask(s, buf):
        t = t0 + s
        thr = thr_scr[...]
        for hl in range(nh):
            lanes = slice(hl * 128, (hl + 1) * 128)
            bias_scr[buf, :, lanes] = jnp.where(s_scr[t, :, lanes] >= thr, 0.0, NEG)

    def attend_heads(s, n):
        ones_cols = jnp.ones((n * kt, HEAD_DIM), bf16)
        row0 = pl.multiple_of(s * kt, kt)
        bias = jnp.concatenate([bias_scr[u] for u in range(n)], axis=1)
        for h in range(N_HEADS):
            cols = slice(h * HEAD_DIM, (h + 1) * HEAD_DIM)
            qh = q_ref[0, :, cols]
            kh = k_ref[0, pl.ds(row0, n * kt), cols]
            vh = v_ref[0, pl.ds(row0, n * kt), cols]
            lg = lax.dot_general(qh, kh, (((1,), (1,)), ((), ())),
                                 preferred_element_type=f32) + bias
            m_old = m_scr[h]
            m_new = jnp.maximum(m_old, jnp.max(lg, axis=1, keepdims=True))
            alpha = jnp.exp2(m_old - m_new)
            p = jnp.exp2(lg - jnp.concatenate([m_new] * (n * nh), axis=1))
            v_ext = jnp.concatenate([vh, ones_cols], axis=1)
            pv = jnp.dot(p.astype(bf16), v_ext, preferred_element_type=f32)
            l_scr[h] = alpha * l_scr[h] + pv[:, HEAD_DIM:]
            acc_scr[:, cols] = alpha * acc_scr[:, cols] + pv[:, :HEAD_DIM]
            m_scr[h] = m_new

    def pair_body(p, carry):
        attend_mask(2 * p, 0)
        attend_mask(2 * p + 1, 1)
        attend_heads(2 * p, 2)
        score_tile(2 * p, causal=False)
        score_tile(2 * p + 1, causal=False)
        return carry

    def both_body(s, carry):
        attend_mask(s, 0)
        attend_heads(s, 1)
        score_tile(s, causal=False)
        return carry

    def attend_body(s, carry):
        attend_mask(s, 0)
        attend_heads(s, 1)
        return carry

    def score_body(s, carry):
        score_tile(s, causal=True)
        return carry

    n_pairs = n_both // 2
    lax.fori_loop(0, n_pairs, pair_body, 0)
    lax.fori_loop(2 * n_pairs, n_both, both_body, 0)
    lax.fori_loop(n_both, n_at, attend_body, 0)
    lax.fori_loop(n_both, n_sc, score_body, 0)

    @pl.when(jnp.logical_and(has_scores, j == (n_tiles - 1) // g_tiles))
    def _threshold():
        qpos = i * tq + lax.broadcasted_iota(jnp.int32, (8, tq), 1)
        n_vis = (qpos - (FRONT_PAD - 1)).astype(f32)
        short = n_vis <= kf

        def search(reduce):
            def count_ge(v):
                return reduce(v, lambda blk, bv: jnp.where(blk >= bv, 1.0, 0.0), 0.0, jnp.add, jnp.sum)

            def count_gt(v):
                return reduce(v, lambda blk, bv: jnp.where(blk > bv, 1.0, 0.0), 0.0, jnp.add, jnp.sum)

            zero = jnp.zeros((8, tq), f32)
            cge0 = count_ge(zero)
            cgt0 = count_gt(zero)
            at_zero = jnp.logical_and(jnp.logical_not(short),
                                      jnp.logical_and(cgt0 < kf, cge0 >= kf))
            res0 = jnp.where(jnp.logical_or(short, at_zero), 1.0, 0.0)
            thr0 = jnp.where(short, BELOW_ALL, zero)
            need0 = jnp.where(jnp.logical_and(at_zero, cge0 > kf), kf - cgt0, KEEP_ALL)
            lo0 = jnp.where(cgt0 >= kf, jnp.maximum(lo_lists, zero), lo_lists)
            hi0 = jnp.where(cge0 < kf, jnp.minimum(hi_lists, zero), hi_lists)

            def bis_cond(st):
                it, _, _, _, _, active = st
                return jnp.logical_and(it < BISECT_CAP, jnp.max(active) > 0.0)

            def bis_step(st):
                it, lo, hi, thr, resolved, active = st
                piv = lo + (hi - lo) * 0.5
                splits = jnp.logical_and(piv > lo, piv < hi)
                c = count_ge(piv)
                upd = jnp.logical_and(active > 0.0, splits)
                hit = jnp.logical_and(upd, c == kf)
                thr = jnp.where(hit, piv, thr)
                resolved = jnp.where(hit, 1.0, resolved)
                lo = jnp.where(jnp.logical_and(upd, c > kf), piv, lo)
                hi = jnp.where(jnp.logical_and(upd, c < kf), piv, hi)
                active = jnp.where(jnp.logical_and(upd, jnp.logical_not(hit)), 1.0, 0.0)
                return it + 1, lo, hi, thr, resolved, active

            def bis_body(st):
                return bis_step(bis_step(st))

            st = (jnp.int32(0), lo0, hi0, thr0, res0, 1.0 - res0)
            st = lax.fori_loop(0, BISECT_UNTESTED // 2, lambda _, s_: bis_body(s_), st)
            _, lo, hi, thr, resolved, _ = lax.while_loop(bis_cond, bis_body, st)

            thr_scr[...] = to_rows(thr)
            need_scr[...] = to_rows(need0)
            flag_scr[0] = jnp.where(any_row(need0 < KEEP_ALL), 1, 0).astype(jnp.int32)
            unresolved = resolved < 0.5

            @pl.when(any_row(unresolved))
            def _fallback():
                v0 = reduce(hi, lambda blk, bv: jnp.where(blk <= bv, blk, -jnp.inf),
                            -jnp.inf, jnp.maximum, jnp.max)

                def fb_cond(st):
                    _, cge, _ = st
                    return any_row(jnp.logical_and(unresolved, cge < kf))

                def fb_body(st):
                    v, cge, cgt = st
                    more = jnp.logical_and(unresolved, cge < kf)
                    v2 = reduce(v, lambda blk, bv: jnp.where(blk < bv, blk, -jnp.inf),
                                -jnp.inf, jnp.maximum, jnp.max)
                    return (jnp.where(more, v2, v), jnp.where(more, count_ge(v2), cge),
                            jnp.where(more, count_gt(v2), cgt))

                v, cge, cgt = lax.while_loop(fb_cond, fb_body, (v0, count_ge(v0), count_gt(v0)))
                need = kf - cgt
                partial = jnp.logical_and(unresolved, (cge - cgt) > need)
                thr_scr[...] = to_rows(jnp.where(unresolved, v, thr))
                need_scr[...] = to_rows(jnp.where(partial, need, need0))

                @pl.when(any_row(partial))
                def _():
                    flag_scr[0] = jnp.int32(1)

        for k in range(LANE_LIST):
            for g in range(tq // 128):
                cand_t_scr[k * 128:(k + 1) * 128, g * 128:(g + 1) * 128] = (
                    cand_scr[k, g * 128:(g + 1) * 128, :].T)

        second = cand_t_scr[128:256, :]
        lo_lists = jnp.broadcast_to(jnp.min(second, axis=0, keepdims=True), (8, tq))
        hi_lists = jnp.broadcast_to(jnp.max(second, axis=0, keepdims=True), (8, tq))

        search(reduce_lists)

        @pl.when(any_row(cand_scr[LANE_LIST - 1] >= thr_scr[...]))
        def _():
            search(reduce_scores)

        @pl.when(flag_scr[0] > 0)
        def _():
            thr = lanes_x(thr_scr[...])
            need = lanes_x(need_scr[...])

            def strike(t, seen):
                sc = s_scr[t]
                tie = sc == thr
                counts = jnp.dot(jnp.where(tie, 1.0, 0.0).astype(bf16), tri_ref[...],
                                 preferred_element_type=f32)
                over = (counts[:, :kt] + lanes_x(seen)) > need
                s_scr[t] = jnp.where(jnp.logical_and(tie, over), -jnp.inf, sc)
                return seen + counts[:, kt:]

            lax.fori_loop(0, n_tiles, strike, jnp.zeros((tq, 128), f32))

    @pl.when(jnp.logical_and(has_attn, j == nkb - 1))
    def _finish():
        for h in range(N_HEADS):
            cols = slice(h * HEAD_DIM, (h + 1) * HEAD_DIM)
            o_ref[0, :, cols] = (acc_scr[:, cols] / l_scr[h]).astype(o_ref.dtype)


def _attention(qi4, wi, kit, qkv, *, seq, lp, topk):
    b = qkv.shape[0]
    tq = ROW_TILE
    nt = lp // tq
    g_tiles = max(g for g in range(1, MAX_KV_TILES_PER_STEP + 1) if nt % g == 0)
    nkb = nt // g_tiles
    nq = seq // tq

    def kv_block(iq, j):
        return jnp.minimum(j, iq // g_tiles)

    def score_tile_idx(iq):
        return jnp.minimum(iq + 1, nq)

    ka = lax.broadcasted_iota(jnp.int32, (tq, tq + 128), 0)
    kb = lax.broadcasted_iota(jnp.int32, (tq, tq + 128), 1)
    tri = jnp.where(jnp.logical_or(ka <= kb, kb >= tq), 1.0, 0.0).astype(bf16)

    kern = functools.partial(_attn_kernel, topk=topk, g_tiles=g_tiles)
    return pl.pallas_call(
        kern,
        grid=(b, nq + 1, nkb),
        in_specs=[
            pl.BlockSpec((1, N_IDX_HEADS, tq, IDX_DIM),
                         lambda bb, iq, j: (bb, 0, score_tile_idx(iq), 0)),
            pl.BlockSpec((1, tq, N_IDX_HEADS), lambda bb, iq, j: (bb, score_tile_idx(iq), 0)),
            pl.BlockSpec((1, nt, IDX_DIM, tq), lambda bb, iq, j: (bb, 0, 0, 0),
                         pipeline_mode=pl.Buffered(1)),
            pl.BlockSpec((tq, tq + 128), lambda bb, iq, j: (0, 0)),
            pl.BlockSpec((1, tq, ATTN_WIDTH), lambda bb, iq, j: (bb, iq, 0)),
            pl.BlockSpec((1, g_tiles * tq, ATTN_WIDTH), lambda bb, iq, j: (bb, kv_block(iq, j), 1)),
            pl.BlockSpec((1, g_tiles * tq, ATTN_WIDTH), lambda bb, iq, j: (bb, kv_block(iq, j), 2)),
        ],
        out_specs=pl.BlockSpec((1, tq, ATTN_WIDTH), lambda bb, iq, j: (bb, jnp.maximum(iq - 1, 0), 0)),
        out_shape=jax.ShapeDtypeStruct((b, seq, ATTN_WIDTH), bf16),
        scratch_shapes=[
            pltpu.VMEM((nt, tq, tq), f32),
            pltpu.VMEM((LANE_LIST, tq, 128), f32),
            pltpu.VMEM((LANE_LIST * 128, tq), f32),
            pltpu.VMEM((2, tq, tq), f32),
            pltpu.VMEM((N_IDX_HEADS, tq, 128), f32),
            pltpu.VMEM((tq, 128), f32),
            pltpu.VMEM((tq, 128), f32),
            pltpu.VMEM((N_HEADS, tq, 128), f32),
            pltpu.VMEM((N_HEADS, tq, 128), f32),
            pltpu.VMEM((tq, ATTN_WIDTH), f32),
            pltpu.SMEM((1,), jnp.int32),
        ],
        compiler_params=pltpu.CompilerParams(
            dimension_semantics=("arbitrary", "arbitrary", "arbitrary"),
            vmem_limit_bytes=VMEM_LIMIT),
        name="dsa_attention",
    )(qi4, wi, kit, tri, qkv, qkv, qkv)


def _mix_kernel(x_ref, ya_ref, cv_ref, cvp_ref, gt_ref, cw_ref, wao_ref, wco_ref, wout_ref,
                o_ref, u_scr):
    tm = x_ref.shape[1]
    c = cw_ref.shape[1]
    hr = cvp_ref.shape[1]
    cu = cv_ref[0, :, 0:c].astype(f32)
    cb = cv_ref[0, :, c:2 * c].astype(f32)
    cc = cv_ref[0, :, 2 * c:3 * c].astype(f32)
    u_scr[0:hr, :] = cvp_ref[0, :, 2 * c:3 * c].astype(f32) * cvp_ref[0, :, 0:c].astype(f32)
    u_scr[hr:hr + tm, :] = cc * cu
    w = cw_ref[...]
    conv = (w[2:3] * u_scr[hr:hr + tm, :] + w[1:2] * u_scr[hr - 1:hr - 1 + tm, :]
            + w[0:1] * u_scr[hr - 2:hr - 2 + tm, :])
    y_conv = jnp.dot((cb * conv).astype(bf16), wco_ref[...], preferred_element_type=f32)
    y_attn = jnp.dot(ya_ref[0], wao_ref[...], preferred_element_type=f32)
    d = wout_ref.shape[0]
    mixed = (_sigmoid(gt_ref[0, :, 0:d].astype(f32)) * y_attn
             + _sigmoid(gt_ref[0, :, d:2 * d].astype(f32)) * y_conv)
    o_ref[0] = x_ref[0] + jnp.dot(mixed.astype(bf16), wout_ref[...], preferred_element_type=f32)


def _mix(x, y_attn, conv3, gates3, conv_w, wao, wco, wout):
    b, seq, d = x.shape
    tm = ROW_TILE
    c = conv_w.shape[1]
    const = lambda bb, i: (0, 0)
    return pl.pallas_call(
        _mix_kernel,
        grid=(b, seq // tm),
        in_specs=[
            pl.BlockSpec((1, tm, d), lambda bb, i: (bb, i, 0)),
            pl.BlockSpec((1, tm, ATTN_WIDTH), lambda bb, i: (bb, i, 0)),
            pl.BlockSpec((1, tm, 3 * c), lambda bb, i: (bb, i + 1, 0)),
            pl.BlockSpec((1, HALO_ROWS, 3 * c),
                         lambda bb, i: (bb, (i + 1) * (tm // HALO_ROWS) - 1, 0)),
            pl.BlockSpec((1, tm, 2 * d), lambda bb, i: (bb, i + 1, 0)),
            pl.BlockSpec((CONV_K, c), const),
            pl.BlockSpec(wao.shape, const),
            pl.BlockSpec(wco.shape, const),
            pl.BlockSpec(wout.shape, const),
        ],
        out_specs=pl.BlockSpec((1, tm, d), lambda bb, i: (bb, i, 0)),
        out_shape=jax.ShapeDtypeStruct((b, seq, d), f32),
        scratch_shapes=[pltpu.VMEM((tm + HALO_ROWS, c), f32)],
        compiler_params=pltpu.CompilerParams(
            dimension_semantics=("arbitrary", "arbitrary"), vmem_limit_bytes=VMEM_LIMIT),
        name="mix_merge",
    )(x, y_attn, conv3, conv3, gates3, conv_w, wao, wco, wout)


def _ffn_kernel(h_ref, g1_ref, wg_ref, wu_ref, wd_ref, g2_ref, o_ref):
    h = h_ref[0]
    f = _rms(h, g1_ref[...]).astype(bf16)
    gate = jnp.dot(f, wg_ref[...], preferred_element_type=f32)
    up = jnp.dot(f, wu_ref[...], preferred_element_type=f32)
    act = (gate * _sigmoid(gate)) * up
    h2 = h + jnp.dot(act.astype(bf16), wd_ref[...], preferred_element_type=f32)
    o_ref[0] = _rms(h2, g2_ref[...])


def _ffn(h1, g1, wg, wu, wd, g2):
    b, seq, d = h1.shape
    tm = 2 * ROW_TILE if seq % (2 * ROW_TILE) == 0 else ROW_TILE
    const = lambda bb, i: (0, 0)
    resident = lambda w: pl.BlockSpec(w.shape, const, pipeline_mode=pl.Buffered(1))
    return pl.pallas_call(
        _ffn_kernel,
        grid=(b, seq // tm),
        in_specs=[
            pl.BlockSpec((1, tm, d), lambda bb, i: (bb, i, 0)),
            pl.BlockSpec((1, d), const),
            resident(wg), resident(wu), resident(wd),
            pl.BlockSpec((1, d), const),
        ],
        out_specs=pl.BlockSpec((1, tm, d), lambda bb, i: (bb, i, 0)),
        out_shape=jax.ShapeDtypeStruct((b, seq, d), f32),
        compiler_params=pltpu.CompilerParams(
            dimension_semantics=("arbitrary", "arbitrary"), vmem_limit_bytes=VMEM_LIMIT),
        name="ffn_final",
    )(h1, g1, wg, wu, wd, g2)


def kernel(x, meta_tokens, norm_mix_g, w_in, w_attn_out, conv_w, w_conv_out, w_out,
           norm_ffn_g, w_gate, w_up, w_down, norm_final_g):
    b, seq, d = x.shape
    assert w_in.shape[0] == 1, "single-layer block"
    assert seq % ROW_TILE == 0 and meta_tokens.shape[0] == N_META
    c = conv_w.shape[2]
    lp = ROW_TILE + seq
    l_real = N_META + seq
    topk = min(TOPK_MAX, l_real // 4)
    nt = lp // ROW_TILE

    meta = jnp.broadcast_to(meta_tokens[None].astype(x.dtype), (b, N_META, d))
    hp = jnp.concatenate([jnp.zeros((b, FRONT_PAD, d), x.dtype), meta, x], axis=1)
    h2d = hp.reshape(b * lp, d)

    w = w_in[0]
    o_idx = 3 * ATTN_WIDTH
    n_idx = N_IDX_HEADS * IDX_DIM + IDX_DIM + N_IDX_HEADS
    o_conv = o_idx + n_idx
    o_gate = o_conv + 3 * c
    n_idx_pad = N_IDX_HEADS * IDX_DIM + 128
    w_qkv = w[:, :o_idx].astype(bf16)
    w_idx = jnp.pad(w[:, o_idx:o_conv], ((0, 0), (0, n_idx_pad - n_idx))).astype(bf16)
    w_conv = w[:, o_conv:o_gate].astype(bf16)
    w_gates = w[:, o_gate:].astype(bf16)
    g_mix = norm_mix_g[0][None]

    tm = 512 if (b * lp) % 512 == 0 else ROW_TILE
    qkv_scale = jnp.concatenate([jnp.full((1, ATTN_WIDTH), LOG2E * HEAD_DIM ** -0.5, f32),
                                 jnp.ones((1, 2 * ATTN_WIDTH), f32)], axis=1)
    qkv, conv3, gates3 = _rms_proj(h2d, g_mix, w_qkv, w_conv, w_gates, qkv_scale, tm)
    qkv = qkv.reshape(b, lp, 3 * ATTN_WIDTH)
    conv3 = conv3.reshape(b, lp, 3 * c)
    gates3 = gates3.reshape(b, lp, 2 * d)
    qi4, kit, wi = _rms_proj_idx(h2d, g_mix, w_idx, b=b, lp=lp)

    y_attn = _attention(qi4, wi, kit, qkv, seq=seq, lp=lp, topk=topk)

    h1 = _mix(x, y_attn, conv3, gates3, conv_w[0], w_attn_out[0].astype(bf16),
              w_conv_out[0].astype(bf16), w_out[0].astype(bf16))
    return _ffn(h1, norm_ffn_g[0][None], w_gate[0].astype(bf16), w_up[0].astype(bf16),
                w_down[0].astype(bf16), norm_final_g[None])
```

```python
import functools

import jax
import jax.numpy as jnp
from jax import lax
from jax.experimental import pallas as pl
from jax.experimental.pallas import tpu as pltpu

N_META = 16
N_HEADS = 8
HEAD_DIM = 128
ATTN_WIDTH = N_HEADS * HEAD_DIM
N_IDX_HEADS = 8
IDX_DIM = 64
TOPK_MAX = 256
CONV_K = 3
EPS = 1e-6
IDX_SCALE = (N_IDX_HEADS ** -0.5) * (IDX_DIM ** -0.5)

ROW_TILE = 256
FRONT_PAD = ROW_TILE - N_META
MAX_KV_TILES_PER_STEP = 13
BISECT_CAP = 32
BISECT_UNTESTED = 16
NEG = -1e30
KEEP_ALL = 1e9
BELOW_ALL = -3e38
HALO_ROWS = 16
LANE_LIST = 12
LOG2E = 1.4426950408889634
VMEM_LIMIT = 58 * 1024 * 1024

f32 = jnp.float32
bf16 = jnp.bfloat16


def _rms(x, g):
    return (x * lax.rsqrt(jnp.mean(x * x, axis=-1, keepdims=True) + EPS)) * g


def _sigmoid(x):
    return 1.0 / (1.0 + jnp.exp(-x))


def _rms_proj_kernel(h_ref, g_ref, wq_ref, wc_ref, wg_ref, qs_ref, oq_ref, oc_ref, og_ref):
    a = _rms(h_ref[...], g_ref[...]).astype(bf16)
    q = jnp.dot(a, wq_ref[...], preferred_element_type=f32)
    oq_ref[...] = (q * qs_ref[...]).astype(oq_ref.dtype)
    oc_ref[...] = jnp.dot(a, wc_ref[...], preferred_element_type=f32).astype(oc_ref.dtype)
    og_ref[...] = jnp.dot(a, wg_ref[...], preferred_element_type=f32).astype(og_ref.dtype)


def _rms_proj(h2d, g, w_qkv, w_conv, w_gates, q_scale, tm):
    rows, d = h2d.shape
    const = lambda i: (0, 0)
    resident = lambda w: pl.BlockSpec(w.shape, const, pipeline_mode=pl.Buffered(1))
    widths = (w_qkv.shape[1], w_conv.shape[1], w_gates.shape[1])
    return pl.pallas_call(
        _rms_proj_kernel,
        grid=(rows // tm,),
        in_specs=[
            pl.BlockSpec((tm, d), lambda i: (i, 0)),
            pl.BlockSpec((1, d), const),
            resident(w_qkv), resident(w_conv), resident(w_gates),
            pl.BlockSpec((1, widths[0]), const),
        ],
        out_specs=[pl.BlockSpec((tm, n), lambda i: (i, 0)) for n in widths],
        out_shape=[jax.ShapeDtypeStruct((rows, n), bf16) for n in widths],
        compiler_params=pltpu.CompilerParams(
            dimension_semantics=("arbitrary",), vmem_limit_bytes=VMEM_LIMIT),
        name="rms_proj",
    )(h2d, g, w_qkv, w_conv, w_gates, q_scale)


def _rms_proj_idx_kernel(h_ref, g_ref, w_ref, qi_ref, kit_ref, wi_ref):
    a = _rms(h_ref[...], g_ref[...]).astype(bf16)
    y = jnp.dot(a, w_ref[...], preferred_element_type=f32)
    nqi = N_IDX_HEADS * IDX_DIM
    for h in range(N_IDX_HEADS):
        qi_ref[0, h] = y[:, h * IDX_DIM:(h + 1) * IDX_DIM].astype(bf16)
    kit_ref[0, 0] = y[:, nqi:nqi + 128].T[:IDX_DIM, :].astype(bf16)
    wi_ref[0] = y[:, nqi + IDX_DIM:nqi + IDX_DIM + N_IDX_HEADS]


def _rms_proj_idx(h2d, g, w, *, b, lp):
    d = h2d.shape[1]
    n = w.shape[1]
    tm = ROW_TILE
    nt = lp // tm
    return pl.pallas_call(
        _rms_proj_idx_kernel,
        grid=(b, nt),
        in_specs=[
            pl.BlockSpec((tm, d), lambda bb, i: (bb * nt + i, 0)),
            pl.BlockSpec((1, d), lambda bb, i: (0, 0)),
            pl.BlockSpec((d, n), lambda bb, i: (0, 0)),
        ],
        out_specs=[
            pl.BlockSpec((1, N_IDX_HEADS, tm, IDX_DIM), lambda bb, i: (bb, 0, i, 0)),
            pl.BlockSpec((1, 1, IDX_DIM, tm), lambda bb, i: (bb, i, 0, 0)),
            pl.BlockSpec((1, tm, N_IDX_HEADS), lambda bb, i: (bb, i, 0)),
        ],
        out_shape=[
            jax.ShapeDtypeStruct((b, N_IDX_HEADS, lp, IDX_DIM), bf16),
            jax.ShapeDtypeStruct((b, nt, IDX_DIM, tm), bf16),
            jax.ShapeDtypeStruct((b, lp, N_IDX_HEADS), f32),
        ],
        compiler_params=pltpu.CompilerParams(
            dimension_semantics=("arbitrary", "arbitrary"), vmem_limit_bytes=VMEM_LIMIT),
        name="rms_proj_idx",
    )(h2d, g, w)


def _attn_kernel(qi_ref, wi_ref, kit_ref, tri_ref, q_ref, k_ref, v_ref, o_ref,
                 s_scr, cand_scr, cand_t_scr, bias_scr, wb_scr, thr_scr,
                 need_scr, m_scr, l_scr, acc_scr, flag_scr, *, topk, g_tiles):
    tq = ROW_TILE
    kt = ROW_TILE
    rg = 128
    nh = kt // 128
    iq = pl.program_id(1)
    j = pl.program_id(2)
    nq = pl.num_programs(1) - 1
    nkb = pl.num_programs(2)
    i = iq + 1
    n_tiles = i + 1
    has_scores = iq < nq
    has_attn = iq >= 1
    t0 = j * g_tiles
    n_sc = jnp.where(has_scores, jnp.clip(n_tiles - t0, 0, g_tiles), 0)
    n_at = jnp.where(has_attn, jnp.clip(iq + 1 - t0, 0, g_tiles), 0)
    n_both = jnp.minimum(n_sc, n_at)
    kf = float(topk)

    def lanes_x(v):
        return jnp.concatenate([v] * nh, axis=1)

    def to_dense(rep):
        return jnp.concatenate(
            [rep[g * 128:(g + 1) * 128, :].T[0:8, :] for g in range(tq // 128)], axis=1)

    def to_rows(dense):
        return jnp.concatenate(
            [jnp.broadcast_to(dense[0:1, g * 128:(g + 1) * 128], (128, 128)).T
             for g in range(tq // 128)], axis=0)

    def reduce_scores(vec, fn, init, comb, red):
        vec_rows = to_rows(vec)
        outs = []
        for g in range(tq // rg):
            rows = slice(g * rg, (g + 1) * rg)
            bv = vec_rows[rows]

            def body(t, acc, rows=rows, bv=bv):
                for hl in range(nh):
                    acc = comb(acc, fn(s_scr[t, rows, hl * 128:(hl + 1) * 128], bv))
                return acc

            acc = lax.fori_loop(0, n_tiles, body, jnp.full((rg, 128), init, f32))
            outs.append(jnp.broadcast_to(red(acc, axis=1, keepdims=True), (rg, 128)))
        return to_dense(jnp.concatenate(outs, axis=0))

    def reduce_lists(vec, fn, init, comb, red):
        accs = [jnp.full((8, tq), init, f32) for _ in range(4)]
        for c in range(LANE_LIST * 128 // 8):
            accs[c % 4] = comb(accs[c % 4], fn(cand_t_scr[c * 8:(c + 1) * 8, :], vec))
        acc = comb(comb(accs[0], accs[1]), comb(accs[2], accs[3]))
        return jnp.broadcast_to(red(acc, axis=0, keepdims=True), (8, tq))

    def any_row(mask):
        return jnp.max(jnp.where(mask, 1.0, 0.0)) > 0.0

    @pl.when(jnp.logical_and(j == 0, has_attn))
    def _init_attention():
        m_scr[...] = jnp.full(m_scr.shape, NEG, f32)
        l_scr[...] = jnp.zeros(l_scr.shape, f32)
        acc_scr[...] = jnp.zeros(acc_scr.shape, f32)

    @pl.when(jnp.logical_and(j == 0, has_scores))
    def _init_scores():
        w = wi_ref[0] * IDX_SCALE
        for h in range(N_IDX_HEADS):
            wb_scr[h] = jnp.broadcast_to(w[:, h:h + 1], (tq, 128))
        cand_scr[...] = jnp.full(cand_scr.shape, -jnp.inf, f32)

    def score_tile(s, causal):
        t = t0 + s
        kt_tile = kit_ref[0, t]
        acc = None
        for h in range(N_IDX_HEADS):
            x = jnp.dot(qi_ref[0, h], kt_tile, preferred_element_type=f32)
            term = jnp.maximum(x, 0.0) * lanes_x(wb_scr[h])
            acc = term if acc is None else acc + term
        kpos = t * kt + lax.broadcasted_iota(jnp.int32, (1, kt), 1)
        acc = acc + jnp.where(kpos >= FRONT_PAD, 0.0, -jnp.inf)
        if causal:
            qpos1 = i * tq + lax.broadcasted_iota(jnp.int32, (tq, 1), 0)
            acc = jnp.where(kpos <= qpos1, acc, -jnp.inf)
        s_scr[t] = acc
        for g in range(tq // 8):
            rows = slice(g * 8, (g + 1) * 8)
            lists = [cand_scr[k, rows, :] for k in range(LANE_LIST)]
            for hl in range(nh):
                x = acc[rows, hl * 128:(hl + 1) * 128]
                for k in range(LANE_LIST):
                    top = jnp.maximum(lists[k], x)
                    x = jnp.minimum(lists[k], x)
                    lists[k] = top
            for k in range(LANE_LIST):
                cand_scr[k, rows, :] = lists[k]

    def attend_mask(s, buf):
        t = t0 + s
        thr = thr_scr[...]
        for hl in range(nh):
            lanes = slice(hl * 128, (hl + 1) * 128)
            bias_scr[buf, :, lanes] = jnp.where(s_scr[t, :, lanes] >= thr, 0.0, NEG)

    def attend_heads(s, n):
        ones_cols = jnp.ones((n * kt, HEAD_DIM), bf16)
        row0 = pl.multiple_of(s * kt, kt)
        bias = jnp.concatenate([bias_scr[u] for u in range(n)], axis=1)
        for h in range(N_HEADS):
            cols = slice(h * HEAD_DIM, (h + 1) * HEAD_DIM)
            qh = q_ref[0, :, cols]
            kh = k_ref[0, pl.ds(row0, n * kt), cols]
            vh = v_ref[0, pl.ds(row0, n * kt), cols]
            lg = lax.dot_general(qh, kh, (((1,), (1,)), ((), ())),
                                 preferred_element_type=f32) + bias
            m_old = m_scr[h]
            m_new = jnp.maximum(m_old, jnp.max(lg, axis=1, keepdims=True))
            alpha = jnp.exp2(m_old - m_new)
            p = jnp.exp2(lg - jnp.concatenate([m_new] * (n * nh), axis=1))
            v_ext = jnp.concatenate([vh, ones_cols], axis=1)
            pv = jnp.dot(p.astype(bf16), v_ext, preferred_element_type=f32)
            l_scr[h] = alpha * l_scr[h] + pv[:, HEAD_DIM:]
            acc_scr[:, cols] = alpha * acc_scr[:, cols] + pv[:, :HEAD_DIM]
            m_scr[h] = m_new

    def pair_body(p, carry):
        attend_mask(2 * p, 0)
        attend_mask(2 * p + 1, 1)
        attend_heads(2 * p, 2)
        score_tile(2 * p, causal=False)
        score_tile(2 * p + 1, causal=False)
        return carry

    def both_body(s, carry):
        attend_mask(s, 0)
        attend_heads(s, 1)
        score_tile(s, causal=False)
        return carry

    def attend_body(s, carry):
        attend_mask(s, 0)
        attend_heads(s, 1)
        return carry

    def score_body(s, carry):
        score_tile(s, causal=True)
        return carry

    n_pairs = n_both // 2
    lax.fori_loop(0, n_pairs, pair_body, 0)
    lax.fori_loop(2 * n_pairs, n_both, both_body, 0)
    lax.fori_loop(n_both, n_at, attend_body, 0)
    lax.fori_loop(n_both, n_sc, score_body, 0)

    @pl.when(jnp.logical_and(has_scores, j == (n_tiles - 1) // g_tiles))
    def _threshold():
        qpos = i * tq + lax.broadcasted_iota(jnp.int32, (8, tq), 1)
        n_vis = (qpos - (FRONT_PAD - 1)).astype(f32)
        short = n_vis <= kf

        def search(reduce):
            def count_ge(v):
                return reduce(v, lambda blk, bv: jnp.where(blk >= bv, 1.0, 0.0), 0.0, jnp.add, jnp.sum)

            def count_gt(v):
                return reduce(v, lambda blk, bv: jnp.where(blk > bv, 1.0, 0.0), 0.0, jnp.add, jnp.sum)

            zero = jnp.zeros((8, tq), f32)
            cge0 = count_ge(zero)
            cgt0 = count_gt(zero)
            at_zero = jnp.logical_and(jnp.logical_not(short),
                                      jnp.logical_and(cgt0 < kf, cge0 >= kf))
            res0 = jnp.where(jnp.logical_or(short, at_zero), 1.0, 0.0)
            thr0 = jnp.where(short, BELOW_ALL, zero)
            need0 = jnp.where(jnp.logical_and(at_zero, cge0 > kf), kf - cgt0, KEEP_ALL)
            lo0 = jnp.where(cgt0 >= kf, jnp.maximum(lo_lists, zero), lo_lists)
            hi0 = jnp.where(cge0 < kf, jnp.minimum(hi_lists, zero), hi_lists)

            def bis_cond(st):
                it, _, _, _, _, active = st
                return jnp.logical_and(it < BISECT_CAP, jnp.max(active) > 0.0)

            def bis_step(st):
                it, lo, hi, thr, resolved, active = st
                piv = lo + (hi - lo) * 0.5
                splits = jnp.logical_and(piv > lo, piv < hi)
                c = count_ge(piv)
                upd = jnp.logical_and(active > 0.0, splits)
                hit = jnp.logical_and(upd, c == kf)
                thr = jnp.where(hit, piv, thr)
                resolved = jnp.where(hit, 1.0, resolved)
                lo = jnp.where(jnp.logical_and(upd, c > kf), piv, lo)
                hi = jnp.where(jnp.logical_and(upd, c < kf), piv, hi)
                active = jnp.where(jnp.logical_and(upd, jnp.logical_not(hit)), 1.0, 0.0)
                return it + 1, lo, hi, thr, resolved, active

            def bis_body(st):
                return bis_step(bis_step(st))

            st = (jnp.int32(0), lo0, hi0, thr0, res0, 1.0 - res0)
            st = lax.fori_loop(0, BISECT_UNTESTED // 2, lambda _, s_: bis_body(s_), st)
            _, lo, hi, thr, resolved, _ = lax.while_loop(bis_cond, bis_body, st)

            thr_scr[...] = to_rows(thr)
            need_scr[...] = to_rows(need0)
            flag_scr[0] = jnp.where(any_row(need0 < KEEP_ALL), 1, 0).astype(jnp.int32)
            unresolved = resolved < 0.5

            @pl.when(any_row(unresolved))
            def _fallback():
                v0 = reduce(hi, lambda blk, bv: jnp.where(blk <= bv, blk, -jnp.inf),
                            -jnp.inf, jnp.maximum, jnp.max)

                def fb_cond(st):
                    _, cge, _ = st
                    return any_row(jnp.logical_and(unresolved, cge < kf))

                def fb_body(st):
                    v, cge, cgt = st
                    more = jnp.logical_and(unresolved, cge < kf)
                    v2 = reduce(v, lambda blk, bv: jnp.where(blk < bv, blk, -jnp.inf),
                                -jnp.inf, jnp.maximum, jnp.max)
                    return (jnp.where(more, v2, v), jnp.where(more, count_ge(v2), cge),
                            jnp.where(more, count_gt(v2), cgt))

                v, cge, cgt = lax.while_loop(fb_cond, fb_body, (v0, count_ge(v0), count_gt(v0)))
                need = kf - cgt
                partial = jnp.logical_and(unresolved, (cge - cgt) > need)
                thr_scr[...] = to_rows(jnp.where(unresolved, v, thr))
                need_scr[...] = to_rows(jnp.where(partial, need, need0))

                @pl.when(any_row(partial))
                def _():
                    flag_scr[0] = jnp.int32(1)

        for k in range(LANE_LIST):
            for g in range(tq // 128):
                cand_t_scr[k * 128:(k + 1) * 128, g * 128:(g + 1) * 128] = (
                    cand_scr[k, g * 128:(g + 1) * 128, :].T)

        second = cand_t_scr[128:256, :]
        lo_lists = jnp.broadcast_to(jnp.min(second, axis=0, keepdims=True), (8, tq))
        hi_lists = jnp.broadcast_to(jnp.max(second, axis=0, keepdims=True), (8, tq))

        search(reduce_lists)

        @pl.when(any_row(cand_scr[LANE_LIST - 1] >= thr_scr[...]))
        def _():
            search(reduce_scores)

        @pl.when(flag_scr[0] > 0)
        def _():
            thr = lanes_x(thr_scr[...])
            need = lanes_x(need_scr[...])

            def strike(t, seen):
                sc = s_scr[t]
                tie = sc == thr
                counts = jnp.dot(jnp.where(tie, 1.0, 0.0).astype(bf16), tri_ref[...],
                                 preferred_element_type=f32)
                over = (counts[:, :kt] + lanes_x(seen)) > need
                s_scr[t] = jnp.where(jnp.logical_and(tie, over), -jnp.inf, sc)
                return seen + counts[:, kt:]

            lax.fori_loop(0, n_tiles, strike, jnp.zeros((tq, 128), f32))

    @pl.when(jnp.logical_and(has_attn, j == nkb - 1))
    def _finish():
        for h in range(N_HEADS):
            cols = slice(h * HEAD_DIM, (h + 1) * HEAD_DIM)
            o_ref[0, :, cols] = (acc_scr[:, cols] / l_scr[h]).astype(o_ref.dtype)


def _attention(qi4, wi, kit, qkv, *, seq, lp, topk):
    b = qkv.shape[0]
    tq = ROW_TILE
    nt = lp // tq
    g_tiles = max(g for g in range(1, MAX_KV_TILES_PER_STEP + 1) if nt % g == 0)
    nkb = nt // g_tiles
    nq = seq // tq

    def kv_block(iq, j):
        return jnp.minimum(j, iq // g_tiles)

    def score_tile_idx(iq):
        return jnp.minimum(iq + 1, nq)

    ka = lax.broadcasted_iota(jnp.int32, (tq, tq + 128), 0)
    kb = lax.broadcasted_iota(jnp.int32, (tq, tq + 128), 1)
    tri = jnp.where(jnp.logical_or(ka <= kb, kb >= tq), 1.0, 0.0).astype(bf16)

    kern = functools.partial(_attn_kernel, topk=topk, g_tiles=g_tiles)
    return pl.pallas_call(
        kern,
        grid=(b, nq + 1, nkb),
        in_specs=[
            pl.BlockSpec((1, N_IDX_HEADS, tq, IDX_DIM),
                         lambda bb, iq, j: (bb, 0, score_tile_idx(iq), 0)),
            pl.BlockSpec((1, tq, N_IDX_HEADS), lambda bb, iq, j: (bb, score_tile_idx(iq), 0)),
            pl.BlockSpec((1, nt, IDX_DIM, tq), lambda bb, iq, j: (bb, 0, 0, 0),
                         pipeline_mode=pl.Buffered(1)),
            pl.BlockSpec((tq, tq + 128), lambda bb, iq, j: (0, 0)),
            pl.BlockSpec((1, tq, ATTN_WIDTH), lambda bb, iq, j: (bb, iq, 0)),
            pl.BlockSpec((1, g_tiles * tq, ATTN_WIDTH), lambda bb, iq, j: (bb, kv_block(iq, j), 1)),
            pl.BlockSpec((1, g_tiles * tq, ATTN_WIDTH), lambda bb, iq, j: (bb, kv_block(iq, j), 2)),
        ],
        out_specs=pl.BlockSpec((1, tq, ATTN_WIDTH), lambda bb, iq, j: (bb, jnp.maximum(iq - 1, 0), 0)),
        out_shape=jax.ShapeDtypeStruct((b, seq, ATTN_WIDTH), bf16),
        scratch_shapes=[
            pltpu.VMEM((nt, tq, tq), f32),
            pltpu.VMEM((LANE_LIST, tq, 128), f32),
            pltpu.VMEM((LANE_LIST * 128, tq), f32),
            pltpu.VMEM((2, tq, tq), f32),
            pltpu.VMEM((N_IDX_HEADS, tq, 128), f32),
            pltpu.VMEM((tq, 128), f32),
            pltpu.VMEM((tq, 128), f32),
            pltpu.VMEM((N_HEADS, tq, 128), f32),
            pltpu.VMEM((N_HEADS, tq, 128), f32),
            pltpu.VMEM((tq, ATTN_WIDTH), f32),
            pltpu.SMEM((1,), jnp.int32),
        ],
        compiler_params=pltpu.CompilerParams(
            dimension_semantics=("arbitrary", "arbitrary", "arbitrary"),
            vmem_limit_bytes=VMEM_LIMIT),
        name="dsa_attention",
    )(qi4, wi, kit, tri, qkv, qkv, qkv)


def _mix_kernel(x_ref, ya_ref, cv_ref, cvp_ref, gt_ref, cw_ref, wao_ref, wco_ref, wout_ref,
                o_ref, u_scr):
    tm = x_ref.shape[1]
    c = cw_ref.shape[1]
    hr = cvp_ref.shape[1]
    cu = cv_ref[0, :, 0:c].astype(f32)
    cb = cv_ref[0, :, c:2 * c].astype(f32)
    cc = cv_ref[0, :, 2 * c:3 * c].astype(f32)
    u_scr[0:hr, :] = cvp_ref[0, :, 2 * c:3 * c].astype(f32) * cvp_ref[0, :, 0:c].astype(f32)
    u_scr[hr:hr + tm, :] = cc * cu
    w = cw_ref[...]
    conv = (w[2:3] * u_scr[hr:hr + tm, :] + w[1:2] * u_scr[hr - 1:hr - 1 + tm, :]
            + w[0:1] * u_scr[hr - 2:hr - 2 + tm, :])
    y_conv = jnp.dot((cb * conv).astype(bf16), wco_ref[...], preferred_element_type=f32)
    y_attn = jnp.dot(ya_ref[0], wao_ref[...], preferred_element_type=f32)
    d = wout_ref.shape[0]
    mixed = (_sigmoid(gt_ref[0, :, 0:d].astype(f32)) * y_attn
             + _sigmoid(gt_ref[0, :, d:2 * d].astype(f32)) * y_conv)
    o_ref[0] = x_ref[0] + jnp.dot(mixed.astype(bf16), wout_ref[...], preferred_element_type=f32)


def _mix(x, y_attn, conv3, gates3, conv_w, wao, wco, wout):
    b, seq, d = x.shape
    tm = ROW_TILE
    c = conv_w.shape[1]
    const = lambda bb, i: (0, 0)
    return pl.pallas_call(
        _mix_kernel,
        grid=(b, seq // tm),
        in_specs=[
            pl.BlockSpec((1, tm, d), lambda bb, i: (bb, i, 0)),
            pl.BlockSpec((1, tm, ATTN_WIDTH), lambda bb, i: (bb, i, 0)),
            pl.BlockSpec((1, tm, 3 * c), lambda bb, i: (bb, i + 1, 0)),
            pl.BlockSpec((1, HALO_ROWS, 3 * c),
                         lambda bb, i: (bb, (i + 1) * (tm // HALO_ROWS) - 1, 0)),
            pl.BlockSpec((1, tm, 2 * d), lambda bb, i: (bb, i + 1, 0)),
            pl.BlockSpec((CONV_K, c), const),
            pl.BlockSpec(wao.shape, const),
            pl.BlockSpec(wco.shape, const),
            pl.BlockSpec(wout.shape, const),
        ],
        out_specs=pl.BlockSpec((1, tm, d), lambda bb, i: (bb, i, 0)),
        out_shape=jax.ShapeDtypeStruct((b, seq, d), f32),
        scratch_shapes=[pltpu.VMEM((tm + HALO_ROWS, c), f32)],
        compiler_params=pltpu.CompilerParams(
            dimension_semantics=("arbitrary", "arbitrary"), vmem_limit_bytes=VMEM_LIMIT),
        name="mix_merge",
    )(x, y_attn, conv3, conv3, gates3, conv_w, wao, wco, wout)


def _ffn_kernel(h_ref, g1_ref, wg_ref, wu_ref, wd_ref, g2_ref, o_ref):
    h = h_ref[0]
    f = _rms(h, g1_ref[...]).astype(bf16)
    gate = jnp.dot(f, wg_ref[...], preferred_element_type=f32)
    up = jnp.dot(f, wu_ref[...], preferred_element_type=f32)
    act = (gate * _sigmoid(gate)) * up
    h2 = h + jnp.dot(act.astype(bf16), wd_ref[...], preferred_element_type=f32)
    o_ref[0] = _rms(h2, g2_ref[...])


def _ffn(h1, g1, wg, wu, wd, g2):
    b, seq, d = h1.shape
    tm = 2 * ROW_TILE if seq % (2 * ROW_TILE) == 0 else ROW_TILE
    const = lambda bb, i: (0, 0)
    resident = lambda w: pl.BlockSpec(w.shape, const, pipeline_mode=pl.Buffered(1))
    return pl.pallas_call(
        _ffn_kernel,
        grid=(b, seq // tm),
        in_specs=[
            pl.BlockSpec((1, tm, d), lambda bb, i: (bb, i, 0)),
            pl.BlockSpec((1, d), const),
            resident(wg), resident(wu), resident(wd),
            pl.BlockSpec((1, d), const),
        ],
        out_specs=pl.BlockSpec((1, tm, d), lambda bb, i: (bb, i, 0)),
        out_shape=jax.ShapeDtypeStruct((b, seq, d), f32),
        compiler_params=pltpu.CompilerParams(
            dimension_semantics=("arbitrary", "arbitrary"), vmem_limit_bytes=VMEM_LIMIT),
        name="ffn_final",
    )(h1, g1, wg, wu, wd, g2)


def kernel(x, meta_tokens, norm_mix_g, w_in, w_attn_out, conv_w, w_conv_out, w_out,
           norm_ffn_g, w_gate, w_up, w_down, norm_final_g):
    b, seq, d = x.shape
    assert w_in.shape[0] == 1, "single-layer block"
    assert seq % ROW_TILE == 0 and meta_tokens.shape[0] == N_META
    c = conv_w.shape[2]
    lp = ROW_TILE + seq
    l_real = N_META + seq
    topk = min(TOPK_MAX, l_real // 4)
    nt = lp // ROW_TILE

    meta = jnp.broadcast_to(meta_tokens[None].astype(x.dtype), (b, N_META, d))
    hp = jnp.concatenate([jnp.zeros((b, FRONT_PAD, d), x.dtype), meta, x], axis=1)
    h2d = hp.reshape(b * lp, d)

    w = w_in[0]
    o_idx = 3 * ATTN_WIDTH
    n_idx = N_IDX_HEADS * IDX_DIM + IDX_DIM + N_IDX_HEADS
    o_conv = o_idx + n_idx
    o_gate = o_conv + 3 * c
    n_idx_pad = N_IDX_HEADS * IDX_DIM + 128
    w_qkv = w[:, :o_idx].astype(bf16)
    w_idx = jnp.pad(w[:, o_idx:o_conv], ((0, 0), (0, n_idx_pad - n_idx))).astype(bf16)
    w_conv = w[:, o_conv:o_gate].astype(bf16)
    w_gates = w[:, o_gate:].astype(bf16)
    g_mix = norm_mix_g[0][None]

    tm = 512 if (b * lp) % 512 == 0 else ROW_TILE
    qkv_scale = jnp.concatenate([jnp.full((1, ATTN_WIDTH), LOG2E * HEAD_DIM ** -0.5, f32),
                                 jnp.ones((1, 2 * ATTN_WIDTH), f32)], axis=1)
    qkv, conv3, gates3 = _rms_proj(h2d, g_mix, w_qkv, w_conv, w_gates, qkv_scale, tm)
    qkv = qkv.reshape(b, lp, 3 * ATTN_WIDTH)
    conv3 = conv3.reshape(b, lp, 3 * c)
    gates3 = gates3.reshape(b, lp, 2 * d)
    qi4, kit, wi = _rms_proj_idx(h2d, g_mix, w_idx, b=b, lp=lp)

    y_attn = _attention(qi4, wi, kit, qkv, seq=seq, lp=lp, topk=topk)

    h1 = _mix(x, y_attn, conv3, gates3, conv_w[0], w_attn_out[0].astype(bf16),
              w_conv_out[0].astype(bf16), w_out[0].astype(bf16))
    return _ffn(h1, norm_ffn_g[0][None], w_gate[0].astype(bf16), w_up[0].astype(bf16),
                w_down[0].astype(bf16), norm_final_g[None])
```

```python
import functools

import jax
import jax.numpy as jnp
from jax import lax
from jax.experimental import pallas as pl
from jax.experimental.pallas import tpu as pltpu

N_META = 16
N_HEADS = 8
HEAD_DIM = 128
ATTN_WIDTH = N_HEADS * HEAD_DIM
N_IDX_HEADS = 8
IDX_DIM = 64
TOPK_MAX = 256
CONV_K = 3
EPS = 1e-6
IDX_SCALE = (N_IDX_HEADS ** -0.5) * (IDX_DIM ** -0.5)

ROW_TILE = 256
FRONT_PAD = ROW_TILE - N_META
MAX_KV_TILES_PER_STEP = 13
BISECT_CAP = 32
BISECT_UNTESTED = 16
NEG = -1e30
KEEP_ALL = 1e9
BELOW_ALL = -3e38
HALO_ROWS = 16
LANE_LIST = 12
LOG2E = 1.4426950408889634
VMEM_LIMIT = 58 * 1024 * 1024

f32 = jnp.float32
bf16 = jnp.bfloat16


def _rms(x, g):
    return (x * lax.rsqrt(jnp.mean(x * x, axis=-1, keepdims=True) + EPS)) * g


def _sigmoid(x):
    return 1.0 / (1.0 + jnp.exp(-x))


def _rms_proj_kernel(h_ref, g_ref, wq_ref, wc_ref, wg_ref, qs_ref, oq_ref, oc_ref, og_ref):
    a = _rms(h_ref[...], g_ref[...]).astype(bf16)
    q = jnp.dot(a, wq_ref[...], preferred_element_type=f32)
    oq_ref[...] = (q * qs_ref[...]).astype(oq_ref.dtype)
    oc_ref[...] = jnp.dot(a, wc_ref[...], preferred_element_type=f32).astype(oc_ref.dtype)
    og_ref[...] = jnp.dot(a, wg_ref[...], preferred_element_type=f32).astype(og_ref.dtype)


def _rms_proj(h2d, g, w_qkv, w_conv, w_gates, q_scale, tm):
    rows, d = h2d.shape
    const = lambda i: (0, 0)
    resident = lambda w: pl.BlockSpec(w.shape, const, pipeline_mode=pl.Buffered(1))
    widths = (w_qkv.shape[1], w_conv.shape[1], w_gates.shape[1])
    return pl.pallas_call(
        _rms_proj_kernel,
        grid=(rows // tm,),
        in_specs=[
            pl.BlockSpec((tm, d), lambda i: (i, 0)),
            pl.BlockSpec((1, d), const),
            resident(w_qkv), resident(w_conv), resident(w_gates),
            pl.BlockSpec((1, widths[0]), const),
        ],
        out_specs=[pl.BlockSpec((tm, n), lambda i: (i, 0)) for n in widths],
        out_shape=[jax.ShapeDtypeStruct((rows, n), bf16) for n in widths],
        compiler_params=pltpu.CompilerParams(
            dimension_semantics=("arbitrary",), vmem_limit_bytes=VMEM_LIMIT),
        name="rms_proj",
    )(h2d, g, w_qkv, w_conv, w_gates, q_scale)


def _rms_proj_idx_kernel(h_ref, g_ref, w_ref, qi_ref, kit_ref, wi_ref):
    a = _rms(h_ref[...], g_ref[...]).astype(bf16)
    y = jnp.dot(a, w_ref[...], preferred_element_type=f32)
    nqi = N_IDX_HEADS * IDX_DIM
    for h in range(N_IDX_HEADS):
        qi_ref[0, h] = y[:, h * IDX_DIM:(h + 1) * IDX_DIM].astype(bf16)
    kit_ref[0, 0] = y[:, nqi:nqi + 128].T[:IDX_DIM, :].astype(bf16)
    wi_ref[0] = y[:, nqi + IDX_DIM:nqi + IDX_DIM + N_IDX_HEADS]


def _rms_proj_idx(h2d, g, w, *, b, lp):
    d = h2d.shape[1]
    n = w.shape[1]
    tm = ROW_TILE
    nt = lp // tm
    return pl.pallas_call(
        _rms_proj_idx_kernel,
        grid=(b, nt),
        in_specs=[
            pl.BlockSpec((tm, d), lambda bb, i: (bb * nt + i, 0)),
            pl.BlockSpec((1, d), lambda bb, i: (0, 0)),
            pl.BlockSpec((d, n), lambda bb, i: (0, 0)),
        ],
        out_specs=[
            pl.BlockSpec((1, N_IDX_HEADS, tm, IDX_DIM), lambda bb, i: (bb, 0, i, 0)),
            pl.BlockSpec((1, 1, IDX_DIM, tm), lambda bb, i: (bb, i, 0, 0)),
            pl.BlockSpec((1, tm, N_IDX_HEADS), lambda bb, i: (bb, i, 0)),
        ],
        out_shape=[
            jax.ShapeDtypeStruct((b, N_IDX_HEADS, lp, IDX_DIM), bf16),
            jax.ShapeDtypeStruct((b, nt, IDX_DIM, tm), bf16),
            jax.ShapeDtypeStruct((b, lp, N_IDX_HEADS), f32),
        ],
        compiler_params=pltpu.CompilerParams(
            dimension_semantics=("arbitrary", "arbitrary"), vmem_limit_bytes=VMEM_LIMIT),
        name="rms_proj_idx",
    )(h2d, g, w)


def _attn_kernel(qi_ref, wi_ref, kit_ref, tri_ref, q_ref, k_ref, v_ref, o_ref,
                 s_scr, cand_scr, cand_t_scr, bias_scr, wb_scr, thr_scr,
                 need_scr, m_scr, l_scr, acc_scr, flag_scr, *, topk, g_tiles):
    tq = ROW_TILE
    kt = ROW_TILE
    rg = 128
    nh = kt // 128
    iq = pl.program_id(1)
    j = pl.program_id(2)
    nq = pl.num_programs(1) - 1
    nkb = pl.num_programs(2)
    i = iq + 1
    n_tiles = i + 1
    has_scores = iq < nq
    has_attn = iq >= 1
    downwards = iq % 2 == 1
    jb = jnp.where(downwards, nkb - 1 - j, j)
    t0 = jb * g_tiles
    n_sc = jnp.where(has_scores, jnp.clip(n_tiles - t0, 0, g_tiles), 0)
    n_at = jnp.where(has_attn, jnp.clip(iq + 1 - t0, 0, g_tiles), 0)
    n_both = jnp.minimum(n_sc, n_at)
    kf = float(topk)

    def lanes_x(v):
        return jnp.concatenate([v] * nh, axis=1)

    def to_dense(rep):
        return jnp.concatenate(
            [rep[g * 128:(g + 1) * 128, :].T[0:8, :] for g in range(tq // 128)], axis=1)

    def to_rows(dense):
        return jnp.concatenate(
            [jnp.broadcast_to(dense[0:1, g * 128:(g + 1) * 128], (128, 128)).T
             for g in range(tq // 128)], axis=0)

    def reduce_scores(vec, fn, init, comb, red):
        vec_rows = to_rows(vec)
        outs = []
        for g in range(tq // rg):
            rows = slice(g * rg, (g + 1) * rg)
            bv = vec_rows[rows]

            def body(t, acc, rows=rows, bv=bv):
                for hl in range(nh):
                    acc = comb(acc, fn(s_scr[t, rows, hl * 128:(hl + 1) * 128], bv))
                return acc

            acc = lax.fori_loop(0, n_tiles, body, jnp.full((rg, 128), init, f32))
            outs.append(jnp.broadcast_to(red(acc, axis=1, keepdims=True), (rg, 128)))
        return to_dense(jnp.concatenate(outs, axis=0))

    def reduce_lists(vec, fn, init, comb, red):
        accs = [jnp.full((8, tq), init, f32) for _ in range(4)]
        for c in range(LANE_LIST * 128 // 8):
            accs[c % 4] = comb(accs[c % 4], fn(cand_t_scr[c * 8:(c + 1) * 8, :], vec))
        acc = comb(comb(accs[0], accs[1]), comb(accs[2], accs[3]))
        return jnp.broadcast_to(red(acc, axis=0, keepdims=True), (8, tq))

    def any_row(mask):
        return jnp.max(jnp.where(mask, 1.0, 0.0)) > 0.0

    @pl.when(jnp.logical_and(j == 0, has_attn))
    def _init_attention():
        m_scr[...] = jnp.full(m_scr.shape, NEG, f32)
        l_scr[...] = jnp.zeros(l_scr.shape, f32)
        acc_scr[...] = jnp.zeros(acc_scr.shape, f32)

    @pl.when(jnp.logical_and(j == 0, has_scores))
    def _init_scores():
        w = wi_ref[0] * IDX_SCALE
        for h in range(N_IDX_HEADS):
            wb_scr[h] = jnp.broadcast_to(w[:, h:h + 1], (tq, 128))
        cand_scr[...] = jnp.full(cand_scr.shape, -jnp.inf, f32)

    def score_tile(s, causal):
        t = t0 + s
        kt_tile = kit_ref[0, t]
        acc = None
        for h in range(N_IDX_HEADS):
            x = jnp.dot(qi_ref[0, h], kt_tile, preferred_element_type=f32)
            term = jnp.maximum(x, 0.0) * lanes_x(wb_scr[h])
            acc = term if acc is None else acc + term
        kpos = t * kt + lax.broadcasted_iota(jnp.int32, (1, kt), 1)
        acc = acc + jnp.where(kpos >= FRONT_PAD, 0.0, -jnp.inf)
        if causal:
            qpos1 = i * tq + lax.broadcasted_iota(jnp.int32, (tq, 1), 0)
            acc = jnp.where(kpos <= qpos1, acc, -jnp.inf)
        s_scr[t] = acc
        for g in range(tq // 8):
            rows = slice(g * 8, (g + 1) * 8)
            lists = [cand_scr[k, rows, :] for k in range(LANE_LIST)]
            for hl in range(nh):
                x = acc[rows, hl * 128:(hl + 1) * 128]
                for k in range(LANE_LIST):
                    top = jnp.maximum(lists[k], x)
                    x = jnp.minimum(lists[k], x)
                    lists[k] = top
            for k in range(LANE_LIST):
                cand_scr[k, rows, :] = lists[k]

    def attend_mask(s, buf):
        t = t0 + s
        thr = thr_scr[...]
        for hl in range(nh):
            lanes = slice(hl * 128, (hl + 1) * 128)
            bias_scr[buf, :, lanes] = jnp.where(s_scr[t, :, lanes] >= thr, 0.0, NEG)

    def attend_heads(s, n):
        ones_cols = jnp.ones((n * kt, HEAD_DIM), bf16)
        row0 = pl.multiple_of(s * kt, kt)
        bias = jnp.concatenate([bias_scr[u] for u in range(n)], axis=1)
        for h in range(N_HEADS):
            cols = slice(h * HEAD_DIM, (h + 1) * HEAD_DIM)
            qh = q_ref[0, :, cols]
            kh = k_ref[0, pl.ds(row0, n * kt), cols]
            vh = v_ref[0, pl.ds(row0, n * kt), cols]
            lg = lax.dot_general(qh, kh, (((1,), (1,)), ((), ())),
                                 preferred_element_type=f32) + bias
            m_old = m_scr[h]
            m_new = jnp.maximum(m_old, jnp.max(lg, axis=1, keepdims=True))
            alpha = jnp.exp2(m_old - m_new)
            p = jnp.exp2(lg - jnp.concatenate([m_new] * (n * nh), axis=1))
            v_ext = jnp.concatenate([vh, ones_cols], axis=1)
            pv = jnp.dot(p.astype(bf16), v_ext, preferred_element_type=f32)
            l_scr[h] = alpha * l_scr[h] + pv[:, HEAD_DIM:]
            acc_scr[:, cols] = alpha * acc_scr[:, cols] + pv[:, :HEAD_DIM]
            m_scr[h] = m_new

    def pair_body(p, carry):
        attend_mask(2 * p, 0)
        attend_mask(2 * p + 1, 1)
        attend_heads(2 * p, 2)
        score_tile(2 * p, causal=False)
        score_tile(2 * p + 1, causal=False)
        return carry

    def both_body(s, carry):
        attend_mask(s, 0)
        attend_heads(s, 1)
        score_tile(s, causal=False)
        return carry

    def attend_body(s, carry):
        attend_mask(s, 0)
        attend_heads(s, 1)
        return carry

    def score_body(s, carry):
        score_tile(s, causal=True)
        return carry

    n_pairs = n_both // 2
    lax.fori_loop(0, n_pairs, pair_body, 0)
    lax.fori_loop(2 * n_pairs, n_both, both_body, 0)
    lax.fori_loop(n_both, n_at, attend_body, 0)
    lax.fori_loop(n_both, n_sc, score_body, 0)

    last_score_block = jnp.where(downwards, 0, (n_tiles - 1) // g_tiles)

    @pl.when(jnp.logical_and(has_scores, jb == last_score_block))
    def _threshold():
        qpos = i * tq + lax.broadcasted_iota(jnp.int32, (8, tq), 1)
        n_vis = (qpos - (FRONT_PAD - 1)).astype(f32)
        short = n_vis <= kf

        def search(reduce):
            def count_ge(v):
                return reduce(v, lambda blk, bv: jnp.where(blk >= bv, 1.0, 0.0), 0.0, jnp.add, jnp.sum)

            def count_gt(v):
                return reduce(v, lambda blk, bv: jnp.where(blk > bv, 1.0, 0.0), 0.0, jnp.add, jnp.sum)

            zero = jnp.zeros((8, tq), f32)
            cge0 = count_ge(zero)
            cgt0 = count_gt(zero)
            at_zero = jnp.logical_and(jnp.logical_not(short),
                                      jnp.logical_and(cgt0 < kf, cge0 >= kf))
            res0 = jnp.where(jnp.logical_or(short, at_zero), 1.0, 0.0)
            thr0 = jnp.where(short, BELOW_ALL, zero)
            need0 = jnp.where(jnp.logical_and(at_zero, cge0 > kf), kf - cgt0, KEEP_ALL)
            lo0 = jnp.where(cgt0 >= kf, jnp.maximum(lo_lists, zero), lo_lists)
            hi0 = jnp.where(cge0 < kf, jnp.minimum(hi_lists, zero), hi_lists)

            def bis_cond(st):
                it, _, _, _, _, active = st
                return jnp.logical_and(it < BISECT_CAP, jnp.max(active) > 0.0)

            def bis_step(st):
                it, lo, hi, thr, resolved, active = st
                piv = lo + (hi - lo) * 0.5
                splits = jnp.logical_and(piv > lo, piv < hi)
                c = count_ge(piv)
                upd = jnp.logical_and(active > 0.0, splits)
                hit = jnp.logical_and(upd, c == kf)
                thr = jnp.where(hit, piv, thr)
                resolved = jnp.where(hit, 1.0, resolved)
                lo = jnp.where(jnp.logical_and(upd, c > kf), piv, lo)
                hi = jnp.where(jnp.logical_and(upd, c < kf), piv, hi)
                active = jnp.where(jnp.logical_and(upd, jnp.logical_not(hit)), 1.0, 0.0)
                return it + 1, lo, hi, thr, resolved, active

            def bis_body(st):
                return bis_step(bis_step(st))

            st = (jnp.int32(0), lo0, hi0, thr0, res0, 1.0 - res0)
            st = lax.fori_loop(0, BISECT_UNTESTED // 2, lambda _, s_: bis_body(s_), st)
            _, lo, hi, thr, resolved, _ = lax.while_loop(bis_cond, bis_body, st)

            thr_scr[...] = to_rows(thr)
            need_scr[...] = to_rows(need0)
            flag_scr[0] = jnp.where(any_row(need0 < KEEP_ALL), 1, 0).astype(jnp.int32)
            unresolved = resolved < 0.5

            @pl.when(any_row(unresolved))
            def _fallback():
                v0 = reduce(hi, lambda blk, bv: jnp.where(blk <= bv, blk, -jnp.inf),
                            -jnp.inf, jnp.maximum, jnp.max)

                def fb_cond(st):
                    _, cge, _ = st
                    return any_row(jnp.logical_and(unresolved, cge < kf))

                def fb_body(st):
                    v, cge, cgt = st
                    more = jnp.logical_and(unresolved, cge < kf)
                    v2 = reduce(v, lambda blk, bv: jnp.where(blk < bv, blk, -jnp.inf),
                                -jnp.inf, jnp.maximum, jnp.max)
                    return (jnp.where(more, v2, v), jnp.where(more, count_ge(v2), cge),
                            jnp.where(more, count_gt(v2), cgt))

                v, cge, cgt = lax.while_loop(fb_cond, fb_body, (v0, count_ge(v0), count_gt(v0)))
                need = kf - cgt
                partial = jnp.logical_and(unresolved, (cge - cgt) > need)
                thr_scr[...] = to_rows(jnp.where(unresolved, v, thr))
                need_scr[...] = to_rows(jnp.where(partial, need, need0))

                @pl.when(any_row(partial))
                def _():
                    flag_scr[0] = jnp.int32(1)

        for k in range(LANE_LIST):
            for g in range(tq // 128):
                cand_t_scr[k * 128:(k + 1) * 128, g * 128:(g + 1) * 128] = (
                    cand_scr[k, g * 128:(g + 1) * 128, :].T)

        second = cand_t_scr[128:256, :]
        lo_lists = jnp.broadcast_to(jnp.min(second, axis=0, keepdims=True), (8, tq))
        hi_lists = jnp.broadcast_to(jnp.max(second, axis=0, keepdims=True), (8, tq))

        search(reduce_lists)

        @pl.when(any_row(cand_scr[LANE_LIST - 1] >= thr_scr[...]))
        def _():
            search(reduce_scores)

        @pl.when(flag_scr[0] > 0)
        def _():
            thr = lanes_x(thr_scr[...])
            need = lanes_x(need_scr[...])

            def strike(t, seen):
                sc = s_scr[t]
                tie = sc == thr
                counts = jnp.dot(jnp.where(tie, 1.0, 0.0).astype(bf16), tri_ref[...],
                                 preferred_element_type=f32)
                over = (counts[:, :kt] + lanes_x(seen)) > need
                s_scr[t] = jnp.where(jnp.logical_and(tie, over), -jnp.inf, sc)
                return seen + counts[:, kt:]

            lax.fori_loop(0, n_tiles, strike, jnp.zeros((tq, 128), f32))

    @pl.when(jnp.logical_and(has_attn, j == nkb - 1))
    def _finish():
        for h in range(N_HEADS):
            cols = slice(h * HEAD_DIM, (h + 1) * HEAD_DIM)
            o_ref[0, :, cols] = (acc_scr[:, cols] / l_scr[h]).astype(o_ref.dtype)


def _attention(qi4, wi, kit, qkv, *, seq, lp, topk):
    b = qkv.shape[0]
    tq = ROW_TILE
    nt = lp // tq
    g_tiles = max(g for g in range(1, MAX_KV_TILES_PER_STEP + 1) if nt % g == 0)
    nkb = nt // g_tiles
    nq = seq // tq

    def kv_block(iq, j):
        jb = jnp.where(iq % 2 == 1, nkb - 1 - j, j)
        return jnp.minimum(jb, iq // g_tiles)

    def score_tile_idx(iq):
        return jnp.minimum(iq + 1, nq)

    ka = lax.broadcasted_iota(jnp.int32, (tq, tq + 128), 0)
    kb = lax.broadcasted_iota(jnp.int32, (tq, tq + 128), 1)
    tri = jnp.where(jnp.logical_or(ka <= kb, kb >= tq), 1.0, 0.0).astype(bf16)

    kern = functools.partial(_attn_kernel, topk=topk, g_tiles=g_tiles)
    return pl.pallas_call(
        kern,
        grid=(b, nq + 1, nkb),
        in_specs=[
            pl.BlockSpec((1, N_IDX_HEADS, tq, IDX_DIM),
                         lambda bb, iq, j: (bb, 0, score_tile_idx(iq), 0)),
            pl.BlockSpec((1, tq, N_IDX_HEADS), lambda bb, iq, j: (bb, score_tile_idx(iq), 0)),
            pl.BlockSpec((1, nt, IDX_DIM, tq), lambda bb, iq, j: (bb, 0, 0, 0),
                         pipeline_mode=pl.Buffered(1)),
            pl.BlockSpec((tq, tq + 128), lambda bb, iq, j: (0, 0)),
            pl.BlockSpec((1, tq, ATTN_WIDTH), lambda bb, iq, j: (bb, iq, 0)),
            pl.BlockSpec((1, g_tiles * tq, ATTN_WIDTH), lambda bb, iq, j: (bb, kv_block(iq, j), 1)),
            pl.BlockSpec((1, g_tiles * tq, ATTN_WIDTH), lambda bb, iq, j: (bb, kv_block(iq, j), 2)),
        ],
        out_specs=pl.BlockSpec((1, tq, ATTN_WIDTH), lambda bb, iq, j: (bb, jnp.maximum(iq - 1, 0), 0)),
        out_shape=jax.ShapeDtypeStruct((b, seq, ATTN_WIDTH), bf16),
        scratch_shapes=[
            pltpu.VMEM((nt, tq, tq), f32),
            pltpu.VMEM((LANE_LIST, tq, 128), f32),
            pltpu.VMEM((LANE_LIST * 128, tq), f32),
            pltpu.VMEM((2, tq, tq), f32),
            pltpu.VMEM((N_IDX_HEADS, tq, 128), f32),
            pltpu.VMEM((tq, 128), f32),
            pltpu.VMEM((tq, 128), f32),
            pltpu.VMEM((N_HEADS, tq, 128), f32),
            pltpu.VMEM((N_HEADS, tq, 128), f32),
            pltpu.VMEM((tq, ATTN_WIDTH), f32),
            pltpu.SMEM((1,), jnp.int32),
        ],
        compiler_params=pltpu.CompilerParams(
            dimension_semantics=("arbitrary", "arbitrary", "arbitrary"),
            vmem_limit_bytes=VMEM_LIMIT),
        name="dsa_attention",
    )(qi4, wi, kit, tri, qkv, qkv, qkv)


def _mix_kernel(x_ref, ya_ref, cv_ref, cvp_ref, gt_ref, cw_ref, wao_ref, wco_ref, wout_ref,
                o_ref, u_scr):
    tm = x_ref.shape[1]
    c = cw_ref.shape[1]
    hr = cvp_ref.shape[1]
    cu = cv_ref[0, :, 0:c].astype(f32)
    cb = cv_ref[0, :, c:2 * c].astype(f32)
    cc = cv_ref[0, :, 2 * c:3 * c].astype(f32)
    u_scr[0:hr, :] = cvp_ref[0, :, 2 * c:3 * c].astype(f32) * cvp_ref[0, :, 0:c].astype(f32)
    u_scr[hr:hr + tm, :] = cc * cu
    w = cw_ref[...]
    conv = (w[2:3] * u_scr[hr:hr + tm, :] + w[1:2] * u_scr[hr - 1:hr - 1 + tm, :]
            + w[0:1] * u_scr[hr - 2:hr - 2 + tm, :])
    y_conv = jnp.dot((cb * conv).astype(bf16), wco_ref[...], preferred_element_type=f32)
    y_attn = jnp.dot(ya_ref[0], wao_ref[...], preferred_element_type=f32)
    d = wout_ref.shape[0]
    mixed = (_sigmoid(gt_ref[0, :, 0:d].astype(f32)) * y_attn
             + _sigmoid(gt_ref[0, :, d:2 * d].astype(f32)) * y_conv)
    o_ref[0] = x_ref[0] + jnp.dot(mixed.astype(bf16), wout_ref[...], preferred_element_type=f32)


def _mix(x, y_attn, conv3, gates3, conv_w, wao, wco, wout):
    b, seq, d = x.shape
    tm = ROW_TILE
    c = conv_w.shape[1]
    const = lambda bb, i: (0, 0)
    return pl.pallas_call(
        _mix_kernel,
        grid=(b, seq // tm),
        in_specs=[
            pl.BlockSpec((1, tm, d), lambda bb, i: (bb, i, 0)),
            pl.BlockSpec((1, tm, ATTN_WIDTH), lambda bb, i: (bb, i, 0)),
            pl.BlockSpec((1, tm, 3 * c), lambda bb, i: (bb, i + 1, 0)),
            pl.BlockSpec((1, HALO_ROWS, 3 * c),
                         lambda bb, i: (bb, (i + 1) * (tm // HALO_ROWS) - 1, 0)),
            pl.BlockSpec((1, tm, 2 * d), lambda bb, i: (bb, i + 1, 0)),
            pl.BlockSpec((CONV_K, c), const),
            pl.BlockSpec(wao.shape, const),
            pl.BlockSpec(wco.shape, const),
            pl.BlockSpec(wout.shape, const),
        ],
        out_specs=pl.BlockSpec((1, tm, d), lambda bb, i: (bb, i, 0)),
        out_shape=jax.ShapeDtypeStruct((b, seq, d), f32),
        scratch_shapes=[pltpu.VMEM((tm + HALO_ROWS, c), f32)],
        compiler_params=pltpu.CompilerParams(
            dimension_semantics=("arbitrary", "arbitrary"), vmem_limit_bytes=VMEM_LIMIT),
        name="mix_merge",
    )(x, y_attn, conv3, conv3, gates3, conv_w, wao, wco, wout)


def _ffn_kernel(h_ref, g1_ref, wg_ref, wu_ref, wd_ref, g2_ref, o_ref):
    h = h_ref[0]
    f = _rms(h, g1_ref[...]).astype(bf16)
    gate = jnp.dot(f, wg_ref[...], preferred_element_type=f32)
    up = jnp.dot(f, wu_ref[...], preferred_element_type=f32)
    act = (gate * _sigmoid(gate)) * up
    h2 = h + jnp.dot(act.astype(bf16), wd_ref[...], preferred_element_type=f32)
    o_ref[0] = _rms(h2, g2_ref[...])


def _ffn(h1, g1, wg, wu, wd, g2):
    b, seq, d = h1.shape
    tm = 2 * ROW_TILE if seq % (2 * ROW_TILE) == 0 else ROW_TILE
    const = lambda bb, i: (0, 0)
    resident = lambda w: pl.BlockSpec(w.shape, const, pipeline_mode=pl.Buffered(1))
    return pl.pallas_call(
        _ffn_kernel,
        grid=(b, seq // tm),
        in_specs=[
            pl.BlockSpec((1, tm, d), lambda bb, i: (bb, i, 0)),
            pl.BlockSpec((1, d), const),
            resident(wg), resident(wu), resident(wd),
            pl.BlockSpec((1, d), const),
        ],
        out_specs=pl.BlockSpec((1, tm, d), lambda bb, i: (bb, i, 0)),
        out_shape=jax.ShapeDtypeStruct((b, seq, d), f32),
        compiler_params=pltpu.CompilerParams(
            dimension_semantics=("arbitrary", "arbitrary"), vmem_limit_bytes=VMEM_LIMIT),
        name="ffn_final",
    )(h1, g1, wg, wu, wd, g2)


def kernel(x, meta_tokens, norm_mix_g, w_in, w_attn_out, conv_w, w_conv_out, w_out,
           norm_ffn_g, w_gate, w_up, w_down, norm_final_g):
    b, seq, d = x.shape
    assert w_in.shape[0] == 1, "single-layer block"
    assert seq % ROW_TILE == 0 and meta_tokens.shape[0] == N_META
    c = conv_w.shape[2]
    lp = ROW_TILE + seq
    l_real = N_META + seq
    topk = min(TOPK_MAX, l_real // 4)
    nt = lp // ROW_TILE

    meta = jnp.broadcast_to(meta_tokens[None].astype(x.dtype), (b, N_META, d))
    hp = jnp.concatenate([jnp.zeros((b, FRONT_PAD, d), x.dtype), meta, x], axis=1)
    h2d = hp.reshape(b * lp, d)

    w = w_in[0]
    o_idx = 3 * ATTN_WIDTH
    n_idx = N_IDX_HEADS * IDX_DIM + IDX_DIM + N_IDX_HEADS
    o_conv = o_idx + n_idx
    o_gate = o_conv + 3 * c
    n_idx_pad = N_IDX_HEADS * IDX_DIM + 128
    w_qkv = w[:, :o_idx].astype(bf16)
    w_idx = jnp.pad(w[:, o_idx:o_conv], ((0, 0), (0, n_idx_pad - n_idx))).astype(bf16)
    w_conv = w[:, o_conv:o_gate].astype(bf16)
    w_gates = w[:, o_gate:].astype(bf16)
    g_mix = norm_mix_g[0][None]

    tm = 512 if (b * lp) % 512 == 0 else ROW_TILE
    qkv_scale = jnp.concatenate([jnp.full((1, ATTN_WIDTH), LOG2E * HEAD_DIM ** -0.5, f32),
                                 jnp.ones((1, 2 * ATTN_WIDTH), f32)], axis=1)
    qkv, conv3, gates3 = _rms_proj(h2d, g_mix, w_qkv, w_conv, w_gates, qkv_scale, tm)
    qkv = qkv.reshape(b, lp, 3 * ATTN_WIDTH)
    conv3 = conv3.reshape(b, lp, 3 * c)
    gates3 = gates3.reshape(b, lp, 2 * d)
    qi4, kit, wi = _rms_proj_idx(h2d, g_mix, w_idx, b=b, lp=lp)

    y_attn = _attention(qi4, wi, kit, qkv, seq=seq, lp=lp, topk=topk)

    h1 = _mix(x, y_attn, conv3, gates3, conv_w[0], w_attn_out[0].astype(bf16),
              w_conv_out[0].astype(bf16), w_out[0].astype(bf16))
    return _ffn(h1, norm_ffn_g[0][None], w_gate[0].astype(bf16), w_up[0].astype(bf16),
                w_down[0].astype(bf16), norm_final_g[None])
```

```python
import functools

import jax
import jax.numpy as jnp
from jax import lax
from jax.experimental import pallas as pl
from jax.experimental.pallas import tpu as pltpu

N_META = 16
N_HEADS = 8
HEAD_DIM = 128
ATTN_WIDTH = N_HEADS * HEAD_DIM
N_IDX_HEADS = 8
IDX_DIM = 64
TOPK_MAX = 256
CONV_K = 3
EPS = 1e-6
IDX_SCALE = (N_IDX_HEADS ** -0.5) * (IDX_DIM ** -0.5)

ROW_TILE = 256
FRONT_PAD = ROW_TILE - N_META
MAX_KV_TILES_PER_STEP = 13
BISECT_CAP = 32
BISECT_UNTESTED = 16
NEG = -1e30
KEEP_ALL = 1e9
BELOW_ALL = -3e38
HALO_ROWS = 16
LANE_LIST = 12
LOG2E = 1.4426950408889634
VMEM_LIMIT = 58 * 1024 * 1024

f32 = jnp.float32
bf16 = jnp.bfloat16


def _rms(x, g):
    return (x * lax.rsqrt(jnp.mean(x * x, axis=-1, keepdims=True) + EPS)) * g


def _sigmoid(x):
    return 1.0 / (1.0 + jnp.exp(-x))


def _rms_proj_kernel(h_ref, g_ref, wq_ref, wc_ref, wg_ref, qs_ref, oq_ref, oc_ref, og_ref):
    a = _rms(h_ref[...], g_ref[...]).astype(bf16)
    q = jnp.dot(a, wq_ref[...], preferred_element_type=f32)
    oq_ref[...] = (q * qs_ref[...]).astype(oq_ref.dtype)
    oc_ref[...] = jnp.dot(a, wc_ref[...], preferred_element_type=f32).astype(oc_ref.dtype)
    og_ref[...] = jnp.dot(a, wg_ref[...], preferred_element_type=f32).astype(og_ref.dtype)


def _rms_proj(h2d, g, w_qkv, w_conv, w_gates, q_scale, tm):
    rows, d = h2d.shape
    const = lambda i: (0, 0)
    resident = lambda w: pl.BlockSpec(w.shape, const, pipeline_mode=pl.Buffered(1))
    widths = (w_qkv.shape[1], w_conv.shape[1], w_gates.shape[1])
    return pl.pallas_call(
        _rms_proj_kernel,
        grid=(rows // tm,),
        in_specs=[
            pl.BlockSpec((tm, d), lambda i: (i, 0)),
            pl.BlockSpec((1, d), const),
            resident(w_qkv), resident(w_conv), resident(w_gates),
            pl.BlockSpec((1, widths[0]), const),
        ],
        out_specs=[pl.BlockSpec((tm, n), lambda i: (i, 0)) for n in widths],
        out_shape=[jax.ShapeDtypeStruct((rows, n), bf16) for n in widths],
        compiler_params=pltpu.CompilerParams(
            dimension_semantics=("arbitrary",), vmem_limit_bytes=VMEM_LIMIT),
        name="rms_proj",
    )(h2d, g, w_qkv, w_conv, w_gates, q_scale)


def _rms_proj_idx_kernel(h_ref, g_ref, w_ref, qi_ref, kit_ref, wi_ref):
    a = _rms(h_ref[...], g_ref[...]).astype(bf16)
    y = jnp.dot(a, w_ref[...], preferred_element_type=f32)
    nqi = N_IDX_HEADS * IDX_DIM
    for h in range(N_IDX_HEADS):
        qi_ref[0, h] = y[:, h * IDX_DIM:(h + 1) * IDX_DIM].astype(bf16)
    kit_ref[0, 0] = y[:, nqi:nqi + 128].T[:IDX_DIM, :].astype(bf16)
    wi_ref[0] = y[:, nqi + IDX_DIM:nqi + IDX_DIM + N_IDX_HEADS]


def _rms_proj_idx(h2d, g, w, *, b, lp):
    d = h2d.shape[1]
    n = w.shape[1]
    tm = ROW_TILE
    nt = lp // tm
    return pl.pallas_call(
        _rms_proj_idx_kernel,
        grid=(b, nt),
        in_specs=[
            pl.BlockSpec((tm, d), lambda bb, i: (bb * nt + i, 0)),
            pl.BlockSpec((1, d), lambda bb, i: (0, 0)),
            pl.BlockSpec((d, n), lambda bb, i: (0, 0)),
        ],
        out_specs=[
            pl.BlockSpec((1, N_IDX_HEADS, tm, IDX_DIM), lambda bb, i: (bb, 0, i, 0)),
            pl.BlockSpec((1, 1, IDX_DIM, tm), lambda bb, i: (bb, i, 0, 0)),
            pl.BlockSpec((1, tm, N_IDX_HEADS), lambda bb, i: (bb, i, 0)),
        ],
        out_shape=[
            jax.ShapeDtypeStruct((b, N_IDX_HEADS, lp, IDX_DIM), bf16),
            jax.ShapeDtypeStruct((b, nt, IDX_DIM, tm), bf16),
            jax.ShapeDtypeStruct((b, lp, N_IDX_HEADS), f32),
        ],
        compiler_params=pltpu.CompilerParams(
            dimension_semantics=("arbitrary", "arbitrary"), vmem_limit_bytes=VMEM_LIMIT),
        name="rms_proj_idx",
    )(h2d, g, w)


def _attn_kernel(qi_ref, wi_ref, kit_ref, tri_ref, q_ref, k_ref, v_ref, o_ref,
                 s_scr, cand_scr, cand_t_scr, bias_scr, wb_scr, thr_scr,
                 need_scr, m_scr, l_scr, acc_scr, flag_scr, *, topk, g_tiles):
    tq = ROW_TILE
    kt = ROW_TILE
    rg = 128
    nh = kt // 128
    iq = pl.program_id(1)
    j = pl.program_id(2)
    nq = pl.num_programs(1) - 1
    nkb = pl.num_programs(2)
    i = iq + 1
    n_tiles = i + 1
    has_scores = iq < nq
    has_attn = iq >= 1
    downwards = iq % 2 == 1
    jb = jnp.where(downwards, nkb - 1 - j, j)
    t0 = jb * g_tiles
    n_sc = jnp.where(has_scores, jnp.clip(n_tiles - t0, 0, g_tiles), 0)
    n_at = jnp.where(has_attn, jnp.clip(iq + 1 - t0, 0, g_tiles), 0)
    n_both = jnp.minimum(n_sc, n_at)
    kf = float(topk)

    def lanes_x(v):
        return jnp.concatenate([v] * nh, axis=1)

    def to_dense(rep):
        return jnp.concatenate(
            [rep[g * 128:(g + 1) * 128, :].T[0:8, :] for g in range(tq // 128)], axis=1)

    def to_rows(dense):
        return jnp.concatenate(
            [jnp.broadcast_to(dense[0:1, g * 128:(g + 1) * 128], (128, 128)).T
             for g in range(tq // 128)], axis=0)

    def reduce_scores(vec, fn, init, comb, red):
        vec_rows = to_rows(vec)
        outs = []
        for g in range(tq // rg):
            rows = slice(g * rg, (g + 1) * rg)
            bv = vec_rows[rows]

            def body(t, acc, rows=rows, bv=bv):
                for hl in range(nh):
                    acc = comb(acc, fn(s_scr[t, rows, hl * 128:(hl + 1) * 128], bv))
                return acc

            acc = lax.fori_loop(0, n_tiles, body, jnp.full((rg, 128), init, f32))
            outs.append(jnp.broadcast_to(red(acc, axis=1, keepdims=True), (rg, 128)))
        return to_dense(jnp.concatenate(outs, axis=0))

    def reduce_lists(vec, fn, init, comb, red):
        accs = [jnp.full((8, tq), init, f32) for _ in range(4)]
        for c in range(LANE_LIST * 128 // 8):
            accs[c % 4] = comb(accs[c % 4], fn(cand_t_scr[c * 8:(c + 1) * 8, :], vec))
        acc = comb(comb(accs[0], accs[1]), comb(accs[2], accs[3]))
        return jnp.broadcast_to(red(acc, axis=0, keepdims=True), (8, tq))

    def any_row(mask):
        return jnp.max(jnp.where(mask, 1.0, 0.0)) > 0.0

    @pl.when(jnp.logical_and(j == 0, has_attn))
    def _init_attention():
        m_scr[...] = jnp.full(m_scr.shape, NEG, f32)
        l_scr[...] = jnp.zeros(l_scr.shape, f32)
        acc_scr[...] = jnp.zeros(acc_scr.shape, f32)

    @pl.when(jnp.logical_and(j == 0, has_scores))
    def _init_scores():
        w = wi_ref[0] * IDX_SCALE
        for h in range(N_IDX_HEADS):
            wb_scr[h] = jnp.broadcast_to(w[:, h:h + 1], (tq, 128))
        cand_scr[...] = jnp.full(cand_scr.shape, -jnp.inf, f32)

    def score_tile(s, causal):
        t = t0 + s
        kt_tile = kit_ref[0, t]
        acc = None
        for h in range(N_IDX_HEADS):
            x = jnp.dot(qi_ref[0, h], kt_tile, preferred_element_type=f32)
            term = jnp.maximum(x, 0.0) * lanes_x(wb_scr[h])
            acc = term if acc is None else acc + term
        kpos = t * kt + lax.broadcasted_iota(jnp.int32, (1, kt), 1)
        acc = acc + jnp.where(kpos >= FRONT_PAD, 0.0, -jnp.inf)
        if causal:
            qpos1 = i * tq + lax.broadcasted_iota(jnp.int32, (tq, 1), 0)
            acc = jnp.where(kpos <= qpos1, acc, -jnp.inf)
        s_scr[t] = acc
        for g in range(tq // 8):
            rows = slice(g * 8, (g + 1) * 8)
            lists = [cand_scr[k, rows, :] for k in range(LANE_LIST)]
            for hl in range(nh):
                x = acc[rows, hl * 128:(hl + 1) * 128]
                for k in range(LANE_LIST):
                    top = jnp.maximum(lists[k], x)
                    x = jnp.minimum(lists[k], x)
                    lists[k] = top
            for k in range(LANE_LIST):
                cand_scr[k, rows, :] = lists[k]

    def attend_mask(s, buf):
        t = t0 + s
        thr = thr_scr[...]
        for hl in range(nh):
            lanes = slice(hl * 128, (hl + 1) * 128)
            bias_scr[buf, :, lanes] = jnp.where(s_scr[t, :, lanes] >= thr, 0.0, NEG)

    def attend_heads(s, n):
        ones_cols = jnp.ones((n * kt, HEAD_DIM), bf16)
        row0 = pl.multiple_of(s * kt, kt)
        for h in range(N_HEADS):
            cols = slice(h * HEAD_DIM, (h + 1) * HEAD_DIM)
            qh = q_ref[0, :, cols]
            kh = k_ref[0, pl.ds(row0, n * kt), cols]
            vh = v_ref[0, pl.ds(row0, n * kt), cols]
            qk = lax.dot_general(qh, kh, (((1,), (1,)), ((), ())), preferred_element_type=f32)
            lgs = [qk[:, u * kt:(u + 1) * kt] + bias_scr[u] for u in range(n)]
            m_old = m_scr[h]
            m_new = m_old
            for lg in lgs:
                m_new = jnp.maximum(m_new, jnp.max(lg, axis=1, keepdims=True))
            alpha = jnp.exp2(m_old - m_new)
            p = jnp.concatenate([jnp.exp2(lg - lanes_x(m_new)) for lg in lgs], axis=1)
            v_ext = jnp.concatenate([vh, ones_cols], axis=1)
            pv = jnp.dot(p.astype(bf16), v_ext, preferred_element_type=f32)
            l_scr[h] = alpha * l_scr[h] + pv[:, HEAD_DIM:]
            acc_scr[:, cols] = alpha * acc_scr[:, cols] + pv[:, :HEAD_DIM]
            m_scr[h] = m_new

    def pair_body(p, carry):
        attend_mask(2 * p, 0)
        attend_mask(2 * p + 1, 1)
        attend_heads(2 * p, 2)
        score_tile(2 * p, causal=False)
        score_tile(2 * p + 1, causal=False)
        return carry

    def both_body(s, carry):
        attend_mask(s, 0)
        attend_heads(s, 1)
        score_tile(s, causal=False)
        return carry

    def attend_body(s, carry):
        attend_mask(s, 0)
        attend_heads(s, 1)
        return carry

    def score_body(s, carry):
        score_tile(s, causal=True)
        return carry

    n_pairs = n_both // 2
    lax.fori_loop(0, n_pairs, pair_body, 0)
    lax.fori_loop(2 * n_pairs, n_both, both_body, 0)
    lax.fori_loop(n_both, n_at, attend_body, 0)
    lax.fori_loop(n_both, n_sc, score_body, 0)

    last_score_block = jnp.where(downwards, 0, (n_tiles - 1) // g_tiles)

    @pl.when(jnp.logical_and(has_scores, jb == last_score_block))
    def _threshold():
        qpos = i * tq + lax.broadcasted_iota(jnp.int32, (8, tq), 1)
        n_vis = (qpos - (FRONT_PAD - 1)).astype(f32)
        short = n_vis <= kf

        def search(reduce):
            def count_ge(v):
                return reduce(v, lambda blk, bv: jnp.where(blk >= bv, 1.0, 0.0), 0.0, jnp.add, jnp.sum)

            def count_gt(v):
                return reduce(v, lambda blk, bv: jnp.where(blk > bv, 1.0, 0.0), 0.0, jnp.add, jnp.sum)

            zero = jnp.zeros((8, tq), f32)
            cge0 = count_ge(zero)
            cgt0 = count_gt(zero)
            at_zero = jnp.logical_and(jnp.logical_not(short),
                                      jnp.logical_and(cgt0 < kf, cge0 >= kf))
            res0 = jnp.where(jnp.logical_or(short, at_zero), 1.0, 0.0)
            thr0 = jnp.where(short, BELOW_ALL, zero)
            need0 = jnp.where(jnp.logical_and(at_zero, cge0 > kf), kf - cgt0, KEEP_ALL)
            lo0 = jnp.where(cgt0 >= kf, jnp.maximum(lo_lists, zero), lo_lists)
            hi0 = jnp.where(cge0 < kf, jnp.minimum(hi_lists, zero), hi_lists)

            def bis_cond(st):
                it, _, _, _, _, active = st
                return jnp.logical_and(it < BISECT_CAP, jnp.max(active) > 0.0)

            def bis_step(st):
                it, lo, hi, thr, resolved, active = st
                piv = lo + (hi - lo) * 0.5
                splits = jnp.logical_and(piv > lo, piv < hi)
                c = count_ge(piv)
                upd = jnp.logical_and(active > 0.0, splits)
                hit = jnp.logical_and(upd, c == kf)
                thr = jnp.where(hit, piv, thr)
                resolved = jnp.where(hit, 1.0, resolved)
                lo = jnp.where(jnp.logical_and(upd, c > kf), piv, lo)
                hi = jnp.where(jnp.logical_and(upd, c < kf), piv, hi)
                active = jnp.where(jnp.logical_and(upd, jnp.logical_not(hit)), 1.0, 0.0)
                return it + 1, lo, hi, thr, resolved, active

            def bis_body(st):
                return bis_step(bis_step(st))

            st = (jnp.int32(0), lo0, hi0, thr0, res0, 1.0 - res0)
            st = lax.fori_loop(0, BISECT_UNTESTED // 2, lambda _, s_: bis_body(s_), st)
            _, lo, hi, thr, resolved, _ = lax.while_loop(bis_cond, bis_body, st)

            thr_scr[...] = to_rows(thr)
            need_scr[...] = to_rows(need0)
            flag_scr[0] = jnp.where(any_row(need0 < KEEP_ALL), 1, 0).astype(jnp.int32)
            unresolved = resolved < 0.5

            @pl.when(any_row(unresolved))
            def _fallback():
                v0 = reduce(hi, lambda blk, bv: jnp.where(blk <= bv, blk, -jnp.inf),
                            -jnp.inf, jnp.maximum, jnp.max)

                def fb_cond(st):
                    _, cge, _ = st
                    return any_row(jnp.logical_and(unresolved, cge < kf))

                def fb_body(st):
                    v, cge, cgt = st
                    more = jnp.logical_and(unresolved, cge < kf)
                    v2 = reduce(v, lambda blk, bv: jnp.where(blk < bv, blk, -jnp.inf),
                                -jnp.inf, jnp.maximum, jnp.max)
                    return (jnp.where(more, v2, v), jnp.where(more, count_ge(v2), cge),
                            jnp.where(more, count_gt(v2), cgt))

                v, cge, cgt = lax.while_loop(fb_cond, fb_body, (v0, count_ge(v0), count_gt(v0)))
                need = kf - cgt
                partial = jnp.logical_and(unresolved, (cge - cgt) > need)
                thr_scr[...] = to_rows(jnp.where(unresolved, v, thr))
                need_scr[...] = to_rows(jnp.where(partial, need, need0))

                @pl.when(any_row(partial))
                def _():
                    flag_scr[0] = jnp.int32(1)

        for k in range(LANE_LIST):
            for g in range(tq // 128):
                cand_t_scr[k * 128:(k + 1) * 128, g * 128:(g + 1) * 128] = (
                    cand_scr[k, g * 128:(g + 1) * 128, :].T)

        second = cand_t_scr[128:256, :]
        lo_lists = jnp.broadcast_to(jnp.min(second, axis=0, keepdims=True), (8, tq))
        hi_lists = jnp.broadcast_to(jnp.max(second, axis=0, keepdims=True), (8, tq))

        search(reduce_lists)

        @pl.when(any_row(cand_scr[LANE_LIST - 1] >= thr_scr[...]))
        def _():
            search(reduce_scores)

        @pl.when(flag_scr[0] > 0)
        def _():
            thr = lanes_x(thr_scr[...])
            need = lanes_x(need_scr[...])

            def strike(t, seen):
                sc = s_scr[t]
                tie = sc == thr
                counts = jnp.dot(jnp.where(tie, 1.0, 0.0).astype(bf16), tri_ref[...],
                                 preferred_element_type=f32)
                over = (counts[:, :kt] + lanes_x(seen)) > need
                s_scr[t] = jnp.where(jnp.logical_and(tie, over), -jnp.inf, sc)
                return seen + counts[:, kt:]

            lax.fori_loop(0, n_tiles, strike, jnp.zeros((tq, 128), f32))

    @pl.when(jnp.logical_and(has_attn, j == nkb - 1))
    def _finish():
        for h in range(N_HEADS):
            cols = slice(h * HEAD_DIM, (h + 1) * HEAD_DIM)
            o_ref[0, :, cols] = (acc_scr[:, cols] / l_scr[h]).astype(o_ref.dtype)


def _attention(qi4, wi, kit, qkv, *, seq, lp, topk):
    b = qkv.shape[0]
    tq = ROW_TILE
    nt = lp // tq
    g_tiles = max(g for g in range(1, MAX_KV_TILES_PER_STEP + 1) if nt % g == 0)
    nkb = nt // g_tiles
    nq = seq // tq

    def kv_block(iq, j):
        jb = jnp.where(iq % 2 == 1, nkb - 1 - j, j)
        return jnp.minimum(jb, iq // g_tiles)

    def score_tile_idx(iq):
        return jnp.minimum(iq + 1, nq)

    ka = lax.broadcasted_iota(jnp.int32, (tq, tq + 128), 0)
    kb = lax.broadcasted_iota(jnp.int32, (tq, tq + 128), 1)
    tri = jnp.where(jnp.logical_or(ka <= kb, kb >= tq), 1.0, 0.0).astype(bf16)

    kern = functools.partial(_attn_kernel, topk=topk, g_tiles=g_tiles)
    return pl.pallas_call(
        kern,
        grid=(b, nq + 1, nkb),
        in_specs=[
            pl.BlockSpec((1, N_IDX_HEADS, tq, IDX_DIM),
                         lambda bb, iq, j: (bb, 0, score_tile_idx(iq), 0)),
            pl.BlockSpec((1, tq, N_IDX_HEADS), lambda bb, iq, j: (bb, score_tile_idx(iq), 0)),
            pl.BlockSpec((1, nt, IDX_DIM, tq), lambda bb, iq, j: (bb, 0, 0, 0),
                         pipeline_mode=pl.Buffered(1)),
            pl.BlockSpec((tq, tq + 128), lambda bb, iq, j: (0, 0)),
            pl.BlockSpec((1, tq, ATTN_WIDTH), lambda bb, iq, j: (bb, iq, 0)),
            pl.BlockSpec((1, g_tiles * tq, ATTN_WIDTH), lambda bb, iq, j: (bb, kv_block(iq, j), 1)),
            pl.BlockSpec((1, g_tiles * tq, ATTN_WIDTH), lambda bb, iq, j: (bb, kv_block(iq, j), 2)),
        ],
        out_specs=pl.BlockSpec((1, tq, ATTN_WIDTH), lambda bb, iq, j: (bb, jnp.maximum(iq - 1, 0), 0)),
        out_shape=jax.ShapeDtypeStruct((b, seq, ATTN_WIDTH), bf16),
        scratch_shapes=[
            pltpu.VMEM((nt, tq, tq), f32),
            pltpu.VMEM((LANE_LIST, tq, 128), f32),
            pltpu.VMEM((LANE_LIST * 128, tq), f32),
            pltpu.VMEM((2, tq, tq), f32),
            pltpu.VMEM((N_IDX_HEADS, tq, 128), f32),
            pltpu.VMEM((tq, 128), f32),
            pltpu.VMEM((tq, 128), f32),
            pltpu.VMEM((N_HEADS, tq, 128), f32),
            pltpu.VMEM((N_HEADS, tq, 128), f32),
            pltpu.VMEM((tq, ATTN_WIDTH), f32),
            pltpu.SMEM((1,), jnp.int32),
        ],
        compiler_params=pltpu.CompilerParams(
            dimension_semantics=("arbitrary", "arbitrary", "arbitrary"),
            vmem_limit_bytes=VMEM_LIMIT),
        name="dsa_attention",
    )(qi4, wi, kit, tri, qkv, qkv, qkv)


def _mix_kernel(x_ref, ya_ref, cv_ref, cvp_ref, gt_ref, cw_ref, wao_ref, wco_ref, wout_ref,
                o_ref, u_scr):
    tm = x_ref.shape[1]
    c = cw_ref.shape[1]
    hr = cvp_ref.shape[1]
    cu = cv_ref[0, :, 0:c].astype(f32)
    cb = cv_ref[0, :, c:2 * c].astype(f32)
    cc = cv_ref[0, :, 2 * c:3 * c].astype(f32)
    u_scr[0:hr, :] = cvp_ref[0, :, 2 * c:3 * c].astype(f32) * cvp_ref[0, :, 0:c].astype(f32)
    u_scr[hr:hr + tm, :] = cc * cu
    w = cw_ref[...]
    conv = (w[2:3] * u_scr[hr:hr + tm, :] + w[1:2] * u_scr[hr - 1:hr - 1 + tm, :]
            + w[0:1] * u_scr[hr - 2:hr - 2 + tm, :])
    y_conv = jnp.dot((cb * conv).astype(bf16), wco_ref[...], preferred_element_type=f32)
    y_attn = jnp.dot(ya_ref[0], wao_ref[...], preferred_element_type=f32)
    d = wout_ref.shape[0]
    mixed = (_sigmoid(gt_ref[0, :, 0:d].astype(f32)) * y_attn
             + _sigmoid(gt_ref[0, :, d:2 * d].astype(f32)) * y_conv)
    o_ref[0] = x_ref[0] + jnp.dot(mixed.astype(bf16), wout_ref[...], preferred_element_type=f32)


def _mix(x, y_attn, conv3, gates3, conv_w, wao, wco, wout):
    b, seq, d = x.shape
    tm = ROW_TILE
    c = conv_w.shape[1]
    const = lambda bb, i: (0, 0)
    return pl.pallas_call(
        _mix_kernel,
        grid=(b, seq // tm),
        in_specs=[
            pl.BlockSpec((1, tm, d), lambda bb, i: (bb, i, 0)),
            pl.BlockSpec((1, tm, ATTN_WIDTH), lambda bb, i: (bb, i, 0)),
            pl.BlockSpec((1, tm, 3 * c), lambda bb, i: (bb, i + 1, 0)),
            pl.BlockSpec((1, HALO_ROWS, 3 * c),
                         lambda bb, i: (bb, (i + 1) * (tm // HALO_ROWS) - 1, 0)),
            pl.BlockSpec((1, tm, 2 * d), lambda bb, i: (bb, i + 1, 0)),
            pl.BlockSpec((CONV_K, c), const),
            pl.BlockSpec(wao.shape, const),
            pl.BlockSpec(wco.shape, const),
            pl.BlockSpec(wout.shape, const),
        ],
        out_specs=pl.BlockSpec((1, tm, d), lambda bb, i: (bb, i, 0)),
        out_shape=jax.ShapeDtypeStruct((b, seq, d), f32),
        scratch_shapes=[pltpu.VMEM((tm + HALO_ROWS, c), f32)],
        compiler_params=pltpu.CompilerParams(
            dimension_semantics=("arbitrary", "arbitrary"), vmem_limit_bytes=VMEM_LIMIT),
        name="mix_merge",
    )(x, y_attn, conv3, conv3, gates3, conv_w, wao, wco, wout)


def _ffn_kernel(h_ref, g1_ref, wg_ref, wu_ref, wd_ref, g2_ref, o_ref):
    h = h_ref[0]
    f = _rms(h, g1_ref[...]).astype(bf16)
    gate = jnp.dot(f, wg_ref[...], preferred_element_type=f32)
    up = jnp.dot(f, wu_ref[...], preferred_element_type=f32)
    act = (gate * _sigmoid(gate)) * up
    h2 = h + jnp.dot(act.astype(bf16), wd_ref[...], preferred_element_type=f32)
    o_ref[0] = _rms(h2, g2_ref[...])


def _ffn(h1, g1, wg, wu, wd, g2):
    b, seq, d = h1.shape
    tm = 2 * ROW_TILE if seq % (2 * ROW_TILE) == 0 else ROW_TILE
    const = lambda bb, i: (0, 0)
    resident = lambda w: pl.BlockSpec(w.shape, const, pipeline_mode=pl.Buffered(1))
    return pl.pallas_call(
        _ffn_kernel,
        grid=(b, seq // tm),
        in_specs=[
            pl.BlockSpec((1, tm, d), lambda bb, i: (bb, i, 0)),
            pl.BlockSpec((1, d), const),
            resident(wg), resident(wu), resident(wd),
            pl.BlockSpec((1, d), const),
        ],
        out_specs=pl.BlockSpec((1, tm, d), lambda bb, i: (bb, i, 0)),
        out_shape=jax.ShapeDtypeStruct((b, seq, d), f32),
        compiler_params=pltpu.CompilerParams(
            dimension_semantics=("arbitrary", "arbitrary"), vmem_limit_bytes=VMEM_LIMIT),
        name="ffn_final",
    )(h1, g1, wg, wu, wd, g2)


def kernel(x, meta_tokens, norm_mix_g, w_in, w_attn_out, conv_w, w_conv_out, w_out,
           norm_ffn_g, w_gate, w_up, w_down, norm_final_g):
    b, seq, d = x.shape
    assert w_in.shape[0] == 1, "single-layer block"
    assert seq % ROW_TILE == 0 and meta_tokens.shape[0] == N_META
    c = conv_w.shape[2]
    lp = ROW_TILE + seq
    l_real = N_META + seq
    topk = min(TOPK_MAX, l_real // 4)
    nt = lp // ROW_TILE

    meta = jnp.broadcast_to(meta_tokens[None].astype(x.dtype), (b, N_META, d))
    hp = jnp.concatenate([jnp.zeros((b, FRONT_PAD, d), x.dtype), meta, x], axis=1)
    h2d = hp.reshape(b * lp, d)

    w = w_in[0]
    o_idx = 3 * ATTN_WIDTH
    n_idx = N_IDX_HEADS * IDX_DIM + IDX_DIM + N_IDX_HEADS
    o_conv = o_idx + n_idx
    o_gate = o_conv + 3 * c
    n_idx_pad = N_IDX_HEADS * IDX_DIM + 128
    w_qkv = w[:, :o_idx].astype(bf16)
    w_idx = jnp.pad(w[:, o_idx:o_conv], ((0, 0), (0, n_idx_pad - n_idx))).astype(bf16)
    w_conv = w[:, o_conv:o_gate].astype(bf16)
    w_gates = w[:, o_gate:].astype(bf16)
    g_mix = norm_mix_g[0][None]

    tm = 512 if (b * lp) % 512 == 0 else ROW_TILE
    qkv_scale = jnp.concatenate([jnp.full((1, ATTN_WIDTH), LOG2E * HEAD_DIM ** -0.5, f32),
                                 jnp.ones((1, 2 * ATTN_WIDTH), f32)], axis=1)
    qkv, conv3, gates3 = _rms_proj(h2d, g_mix, w_qkv, w_conv, w_gates, qkv_scale, tm)
    qkv = qkv.reshape(b, lp, 3 * ATTN_WIDTH)
    conv3 = conv3.reshape(b, lp, 3 * c)
    gates3 = gates3.reshape(b, lp, 2 * d)
    qi4, kit, wi = _rms_proj_idx(h2d, g_mix, w_idx, b=b, lp=lp)

    y_attn = _attention(qi4, wi, kit, qkv, seq=seq, lp=lp, topk=topk)

    h1 = _mix(x, y_attn, conv3, gates3, conv_w[0], w_attn_out[0].astype(bf16),
              w_conv_out[0].astype(bf16), w_out[0].astype(bf16))
    return _ffn(h1, norm_ffn_g[0][None], w_gate[0].astype(bf16), w_up[0].astype(bf16),
                w_down[0].astype(bf16), norm_final_g[None])
```
